```python
import math
import jax
import jax.numpy as jnp
from jax import lax
import numpy as np


D_MODEL = 1024
BATCH = 8
SEQ = 4096
DEPTH = 4

GRID_W = 64
CTX_LEN = 256
N_MIXERS = 3
EXPAND = 2
D_INNER = EXPAND * D_MODEL
NORM_EPS = 1e-6

RWKV_HEAD = 64
RWKV_HEADS = D_INNER // RWKV_HEAD
RWKV_DECAY_LORA = 64
RWKV_ICLR_LORA = 64
RWKV_VRES_LORA = 32
RWKV_N_MU = 6
RWKV_GN_EPS = 64e-5

HYENA_EMB = 33
HYENA_FILTER_WIDTH = 64
HYENA_SIN_W = 1.0
HYENA_FAST_DECAY = 0.3
HYENA_SLOW_DECAY = 1.5
HYENA_TARGET = 1e-2

HGRN_KEY = 128
HGRN_HEADS = D_INNER // HGRN_KEY
HGRN_VAL = D_INNER // HGRN_HEADS
HGRN_CHUNK = 32

kernel_name = 'hybrid_rwkv7_hyena_hgrn2_prefix_backbone'


def rms_norm(x, g, eps=NORM_EPS):
    xf = x.astype(jnp.float32)
    y = xf * lax.rsqrt(jnp.mean(xf * xf, axis=-1, keepdims=True) + eps)
    return (y * g.astype(jnp.float32)).astype(x.dtype)


def adaln(cond, w, b):
    return jnp.split(jax.nn.silu(cond) @ w + b, 3, axis=-1)


def split_heads(a, n_heads):
    return a.reshape(a.shape[:-1] + (n_heads, a.shape[-1] // n_heads))


def merge_heads(a):
    return a.reshape(a.shape[:-2] + (a.shape[-2] * a.shape[-1],))


def grid_shift(h):
    b, t, d = h.shape
    rows = t // GRID_W
    g = h.reshape(b, rows, GRID_W, d)
    q = d // 4
    left = jnp.pad(g[:, :, :-1, :q], ((0, 0), (0, 0), (1, 0), (0, 0)))
    right = jnp.pad(g[:, :, 1:, q:2 * q], ((0, 0), (0, 0), (0, 1), (0, 0)))
    up = jnp.pad(g[:, :-1, :, 2 * q:3 * q], ((0, 0), (1, 0), (0, 0), (0, 0)))
    down = jnp.pad(g[:, 1:, :, 3 * q:], ((0, 0), (0, 1), (0, 0), (0, 0)))
    return jnp.concatenate([left, right, up, down], axis=-1).reshape(b, t, d)


def seq_shift(h):
    half = h.shape[-1] // 2
    prev = jnp.pad(h[:, :-1, :half], ((0, 0), (1, 0), (0, 0)))
    nxt = jnp.pad(h[:, 1:, half:], ((0, 0), (0, 1), (0, 0)))
    return jnp.concatenate([prev, nxt], axis=-1)


def rwkv7_col_groups(vres):
    widths = [D_INNER] * 4 + [RWKV_DECAY_LORA] * 2 + [RWKV_ICLR_LORA] * 2
    groups = [0, 1, 2, 3, 4, 4, 5, 5]
    if vres:
        widths.append(RWKV_VRES_LORA)
        groups.append(2)
    return np.repeat(np.asarray(groups), np.asarray(widths))


def rwkv7_prepare(h, h_shift, p, v_first):
    e = D_INNER
    vres = 'v0' in p
    mu_cols = p['mu'][rwkv7_col_groups(vres)].T
    proj = h @ p['w_in'] + (h_shift - h) @ (p['w_in'] * mu_cols)
    r, k, v, gate = (proj[..., j * e:(j + 1) * e] for j in range(4))
    lo = proj[..., 4 * e:]
    nd, na = 2 * RWKV_DECAY_LORA, 2 * RWKV_ICLR_LORA
    w_lo = split_heads(lo[..., :nd], 2)
    a_lo = split_heads(lo[..., nd:nd + na], 2)
    if vres:
        v = v + (v_first - v) * jax.nn.sigmoid(p['v0'] + lo[..., nd + na:] @ p['v2'])
    else:
        v_first = v
    kk = split_heads(k * p['k_k'], RWKV_HEADS).astype(jnp.float32)
    kk = kk / jnp.maximum(jnp.sqrt(jnp.sum(kk * kk, axis=-1, keepdims=True)), 1e-12)
    return (split_heads(r, RWKV_HEADS), k, split_heads(v, RWKV_HEADS), gate, kk, w_lo, a_lo, v_first)


def rwkv7_direction(k, w_lo, a_lo, p, z):
    w_raw = -jax.nn.softplus(-(p['w0'][z] + jnp.tanh(w_lo[..., z, :]) @ p['w2'][z])) - 0.5
    decay = jnp.exp(-jnp.exp(w_raw.astype(jnp.float32)))
    a = jax.nn.sigmoid(p['a0'][z] + a_lo[..., z, :] @ p['a2'][z])
    k_dir = k * (1.0 + (a - 1.0) * p['k_a'])
    return (split_heads(decay, RWKV_HEADS), split_heads(k_dir, RWKV_HEADS), split_heads(a, RWKV_HEADS))


def rwkv7_scan(r, decay, k, v, kk, a, s0, reverse):
    xs = tuple(jnp.moveaxis(t.astype(jnp.float32), 1, 0) for t in (r, decay, k, v, -kk, kk * a))

    def step(s, inp):
        r_t, w_t, k_t, v_t, a_t, b_t = inp
        sa = jnp.einsum('bhvk,bhk->bhv', s, a_t)
        s = s * w_t[:, :, None, :] + sa[..., None] * b_t[:, :, None, :] + v_t[..., None] * k_t[:, :, None, :]
        return s, jnp.einsum('bhvk,bhk->bhv', s, r_t)

    s_last, ys = lax.scan(step, s0, xs, reverse=reverse)
    return jnp.moveaxis(ys, 0, 1), s_last


def rwkv7_readout(y, bonus, gate, p, dtype):
    mean = jnp.mean(y, axis=-1, keepdims=True)
    var = jnp.mean(jnp.square(y - mean), axis=-1, keepdims=True)
    yn = merge_heads((y - mean) * lax.rsqrt(var + RWKV_GN_EPS))
    yn = yn * p['ln_w'].astype(jnp.float32) + p['ln_b'].astype(jnp.float32)
    return ((yn + merge_heads(bonus)) * jax.nn.silu(gate.astype(jnp.float32))).astype(dtype)


def rwkv7_mixer(hx, hc, p, v_first_x, v_first_c, ctx_out):
    rx, kx, vx, gx, kkx, wlx, alx, v_first_x = rwkv7_prepare(hx, grid_shift(hx), p, v_first_x)
    rc, kc, vc, gc, kkc, wlc, alc, v_first_c = rwkv7_prepare(hc, seq_shift(hc), p, v_first_c)
    s0 = jnp.zeros((hx.shape[0], RWKV_HEADS, RWKV_HEAD, RWKV_HEAD), jnp.float32)
    r_k = p['r_k'].astype(jnp.float32)
    yx = bx = yc = bc = 0.0
    for z in range(2):
        dc, kdc, ac = rwkv7_direction(kc, wlc, alc, p, z)
        dx, kdx, ax = rwkv7_direction(kx, wlx, alx, p, z)
        yc_z, s_ctx = rwkv7_scan(rc, dc, kdc, vc, kkc, ac, s0, z == 1)
        yx_z, _ = rwkv7_scan(rx, dx, kdx, vx, kkx, ax, s_ctx, z == 1)
        yx = yx + yx_z
        bx = bx + jnp.sum(rx * kdx * r_k, axis=-1, keepdims=True) * vx
        if ctx_out:
            yc = yc + yc_z
            bc = bc + jnp.sum(rc * kdc * r_k, axis=-1, keepdims=True) * vc
    ux = rwkv7_readout(yx, bx, gx, p, hx.dtype)
    uc = rwkv7_readout(yc, bc, gc, p, hc.dtype) if ctx_out else None
    return ux, uc, v_first_x, v_first_c


def hyena_filters(length, p):
    f32 = jnp.float32
    t = jnp.linspace(0.0, 1.0, length, dtype=f32)[:, None]
    bands = (HYENA_EMB - 1) // 2
    freqs = jnp.linspace(1e-4, bands - 1, bands, dtype=f32)[None, :]
    ang = (2.0 * math.pi / length) * jnp.arange(length, dtype=f32)[:, None] * freqs
    z = jnp.concatenate([t, jnp.cos(ang), -jnp.sin(ang)], axis=-1)
    sf = p['sin_freq'].astype(f32)
    hdn = jnp.sin(sf * (z @ p['f_w1'].astype(f32) + p['f_b1'].astype(f32)))
    hdn = jnp.sin(sf * (hdn @ p['f_w2'].astype(f32) + p['f_b2'].astype(f32)))
    hdn = jnp.sin(sf * (hdn @ p['f_w3'].astype(f32) + p['f_b3'].astype(f32)))
    filt = hdn @ p['f_w4'].astype(f32)
    deltas = jnp.abs(jnp.linspace(math.log(HYENA_TARGET) / HYENA_SLOW_DECAY,
                                  math.log(HYENA_TARGET) / HYENA_FAST_DECAY, D_INNER, dtype=f32))
    window = jnp.exp(-t * deltas)
    return filt.reshape(length, 2, D_INNER) * window[:, None, :]


def bidir_long_conv(u, filt, skip):
    length = u.shape[1]
    hf, hb = filt[:, 0], filt[:, 1]
    k = jnp.concatenate([hf[:1] + hb[:1], hf[1:], jnp.zeros_like(hf[:1]), hb[:0:-1]], axis=0)
    n = 2 * length
    uf32 = u.astype(jnp.float32)
    uf = jnp.fft.rfft(uf32, n=n, axis=1)
    kf = jnp.fft.rfft(k, n=n, axis=0)
    y = jnp.fft.irfft(uf * kf[None], n=n, axis=1)[:, :length]
    return (y + uf32 * skip.astype(jnp.float32)).astype(u.dtype)


def centred_dwconv3(u, w, b):
    c = u.shape[-1]
    y = lax.conv_general_dilated(u, w[:, None, :].astype(u.dtype), window_strides=(1,), padding=((1, 1),),
                                 dimension_numbers=('NWC', 'WIO', 'NWC'), feature_group_count=c)
    return y + b


def hyena_sequence(h, p):
    e = D_INNER
    proj = h @ p['w_in']
    s = centred_dwconv3(proj[..., :3 * e], p['conv_w'], p['conv_b'])
    x0, x1, v = s[..., :e], s[..., e:2 * e], s[..., 2 * e:]
    v = bidir_long_conv(v * x1, hyena_filters(h.shape[1], p), p['filter_bias'])
    return v * x0 * jax.nn.silu(proj[..., 3 * e:])


def hyena_mixer(hx, hc, p, ctx_out):
    ux = hyena_sequence(hx, p)
    uc = hyena_sequence(hc, p) if ctx_out else None
    return ux, uc


def hgrn2_prepare(h, p, lb):
    e = D_INNER
    proj = h @ p['w_in']
    q = split_heads(jax.nn.silu(proj[..., :e]), HGRN_HEADS)
    i = split_heads(proj[..., 3 * e:4 * e], HGRN_HEADS)
    gate = proj[..., 4 * e:]
    dirs = []
    for z in range(2):
        f = lb + (1.0 - lb) * jax.nn.sigmoid(proj[..., (1 + z) * e:(2 + z) * e].astype(jnp.float32))
        dirs.append((split_heads(1.0 - f, HGRN_HEADS), split_heads(jnp.log(f), HGRN_HEADS)))
    return q, i, gate, dirs


def gla_chunkwise(q, k, v, log_f, s0):
    b, t, h, _ = q.shape
    dv = v.shape[-1]
    n = t // HGRN_CHUNK

    def chunks(a):
        return a.astype(jnp.float32).reshape(b, n, HGRN_CHUNK, h, a.shape[-1]).transpose(1, 0, 3, 2, 4)

    causal = jnp.asarray(np.tril(np.ones((HGRN_CHUNK, HGRN_CHUNK), dtype=bool)))[:, :, None]

    def step(s, inp):
        qc, kc, vc, gc = inp
        cum = jnp.cumsum(gc, axis=2)
        last = cum[:, :, -1:, :]
        o = jnp.einsum('bhck,bhkv->bhcv', qc * jnp.exp(cum), s)
        rel = jnp.exp(jnp.where(causal, cum[:, :, :, None, :] - cum[:, :, None, :, :], -jnp.inf))
        att = jnp.einsum('bhtk,bhsk,bhtsk->bhts', qc, kc, rel)
        o = o + jnp.einsum('bhts,bhsv->bhtv', att, vc)
        s = jnp.exp(last[:, :, 0, :, None]) * s + jnp.einsum('bhsk,bhsv->bhkv', kc * jnp.exp(last - cum), vc)
        return s, o

    s_last, o = lax.scan(step, s0, (chunks(q), chunks(k), chunks(v), chunks(log_f)))
    return o.transpose(1, 0, 3, 2, 4).reshape(b, t, h, dv), s_last


def gla_direction(q, k, v, log_f, s0, reverse):
    if reverse:
        o, s = gla_chunkwise(jnp.flip(q, 1), jnp.flip(k, 1), jnp.flip(v, 1), jnp.flip(log_f, 1), s0)
        return jnp.flip(o, 1), s
    return gla_chunkwise(q, k, v, log_f, s0)


def hgrn2_readout(o, gate, g_norm, dtype):
    o = o * lax.rsqrt(jnp.mean(o * o, axis=-1, keepdims=True) + NORM_EPS) * g_norm.astype(jnp.float32)
    return (merge_heads(o) * jax.nn.silu(gate.astype(jnp.float32))).astype(dtype)


def hgrn2_mixer(hx, hc, p, lb, ctx_out):
    qx, ix, gx, dirs_x = hgrn2_prepare(hx, p, lb)
    qc, ic, gc, dirs_c = hgrn2_prepare(hc, p, lb)
    s0 = jnp.zeros((hx.shape[0], HGRN_HEADS, HGRN_KEY, HGRN_VAL), jnp.float32)
    ox = oc = 0.0
    for z in range(2):
        oc_z, s_ctx = gla_direction(qc, dirs_c[z][0], ic, dirs_c[z][1], s0, z == 1)
        ox_z, _ = gla_direction(qx, dirs_x[z][0], ix, dirs_x[z][1], s_ctx, z == 1)
        ox = ox + ox_z
        if ctx_out:
            oc = oc + oc_z
    ux = hgrn2_readout(ox, gx, p['g_norm'], hx.dtype)
    uc = hgrn2_readout(oc, gc, p['g_norm'], hc.dtype) if ctx_out else None
    return ux, uc


def setup_inputs(seed: int = 0) -> dict:
    key = jax.random.key(seed)
    keys = iter(jax.random.split(key, 96))
    f32 = jnp.float32
    e, d = D_INNER, D_MODEL

    def normal(shape, scale):
        return jax.random.normal(next(keys), shape, f32) * scale

    def near_one(shape):
        return 1.0 + normal(shape, 0.05)

    def rwkv(prefix, vres):
        width = 4 * e + 2 * RWKV_DECAY_LORA + 2 * RWKV_ICLR_LORA + (RWKV_VRES_LORA if vres else 0)
        out = {
            prefix + 'w_in': normal((d, width), d ** -0.5),
            prefix + 'mu': jax.random.uniform(next(keys), (RWKV_N_MU, d), f32),
            prefix + 'w0': jnp.linspace(-6.5, -1.5, e, dtype=f32)[None, :] + normal((2, e), 0.1),
            prefix + 'w2': normal((2, RWKV_DECAY_LORA, e), 0.1),
            prefix + 'a0': normal((2, e), 0.1),
            prefix + 'a2': normal((2, RWKV_ICLR_LORA, e), 0.1),
            prefix + 'k_k': 0.85 + normal((e,), 0.05),
            prefix + 'k_a': near_one((e,)),
            prefix + 'r_k': normal((RWKV_HEADS, RWKV_HEAD), 0.1),
            prefix + 'ln_w': near_one((e,)),
            prefix + 'ln_b': normal((e,), 0.02),
        }
        if vres:
            out[prefix + 'v0'] = near_one((e,))
            out[prefix + 'v2'] = normal((RWKV_VRES_LORA, e), 0.1)
        return out

    inputs = {
        'x': normal((BATCH, SEQ, d), 1.0),
        'c': normal((BATCH, d), 1.0),
        'ctx': normal((BATCH, CTX_LEN, d), 1.0),
        'c_ctx': normal((d,), 1.0),
        'ada_w': normal((DEPTH, d, 3 * d), 0.5 * d ** -0.5),
        'ada_b': normal((DEPTH, 3 * d), 0.02),
        'norm_pre': near_one((DEPTH, d)),
        'norm_post': near_one((DEPTH, d)),
        'w_out': normal((DEPTH, e, d), e ** -0.5),
    }
    inputs.update(rwkv('l0_', False))
    inputs.update({
        'l1_w_in': normal((d, 4 * e), d ** -0.5),
        'l1_conv_w': normal((3, 3 * e), 3 ** -0.5),
        'l1_conv_b': normal((3 * e,), 0.02),
        'l1_f_w1': normal((HYENA_EMB, HYENA_FILTER_WIDTH), HYENA_EMB ** -0.5),
        'l1_f_b1': normal((HYENA_FILTER_WIDTH,), 0.1),
        'l1_f_w2': normal((HYENA_FILTER_WIDTH, HYENA_FILTER_WIDTH), HYENA_FILTER_WIDTH ** -0.5),
        'l1_f_b2': normal((HYENA_FILTER_WIDTH,), 0.1),
        'l1_f_w3': normal((HYENA_FILTER_WIDTH, HYENA_FILTER_WIDTH), HYENA_FILTER_WIDTH ** -0.5),
        'l1_f_b3': normal((HYENA_FILTER_WIDTH,), 0.1),
        'l1_f_w4': normal((HYENA_FILTER_WIDTH, 2 * e), HYENA_FILTER_WIDTH ** -0.5),
        'l1_sin_freq': HYENA_SIN_W * near_one((HYENA_FILTER_WIDTH,)),
        'l1_filter_bias': normal((e,), 1.0),
    })
    inputs.update({
        'l2_w_in': normal((d, 5 * e), d ** -0.5),
        'l2_g_norm': near_one((HGRN_VAL,)),
        'hgrn_lb_logits': normal((DEPTH, e), 1.0),
    })
    inputs.update(rwkv('l3_', True))
    return inputs


def reference(x, c, ctx, c_ctx, ada_w, ada_b, norm_pre, norm_post, w_out,
              l0_w_in, l0_mu, l0_w0, l0_w2, l0_a0, l0_a2, l0_k_k, l0_k_a, l0_r_k, l0_ln_w, l0_ln_b,
              l1_w_in, l1_conv_w, l1_conv_b, l1_f_w1, l1_f_b1, l1_f_w2, l1_f_b2, l1_f_w3, l1_f_b3,
              l1_f_w4, l1_sin_freq, l1_filter_bias,
              l2_w_in, l2_g_norm, hgrn_lb_logits,
              l3_w_in, l3_mu, l3_w0, l3_w2, l3_a0, l3_a2, l3_k_k, l3_k_a, l3_r_k, l3_ln_w, l3_ln_b,
              l3_v0, l3_v2):
    rwkv0 = dict(w_in=l0_w_in, mu=l0_mu, w0=l0_w0, w2=l0_w2, a0=l0_a0, a2=l0_a2, k_k=l0_k_k,
                 k_a=l0_k_a, r_k=l0_r_k, ln_w=l0_ln_w, ln_b=l0_ln_b)
    hyena1 = dict(w_in=l1_w_in, conv_w=l1_conv_w, conv_b=l1_conv_b, f_w1=l1_f_w1, f_b1=l1_f_b1,
                  f_w2=l1_f_w2, f_b2=l1_f_b2, f_w3=l1_f_w3, f_b3=l1_f_b3, f_w4=l1_f_w4,
                  sin_freq=l1_sin_freq, filter_bias=l1_filter_bias)
    hgrn2 = dict(w_in=l2_w_in, g_norm=l2_g_norm)
    rwkv3 = dict(w_in=l3_w_in, mu=l3_mu, w0=l3_w0, w2=l3_w2, a0=l3_a0, a2=l3_a2, k_k=l3_k_k,
                 k_a=l3_k_a, r_k=l3_r_k, ln_w=l3_ln_w, ln_b=l3_ln_b, v0=l3_v0, v2=l3_v2)
    layer_params = [rwkv0, hyena1, hgrn2, rwkv3]

    lb_soft = jax.nn.softmax(hgrn_lb_logits.astype(jnp.float32), axis=0)
    lower_bounds = jnp.cumsum(lb_soft, axis=0) - lb_soft[0]

    v_first_x = None
    v_first_c = None
    for l in range(DEPTH):
        kind = l % N_MIXERS
        p = layer_params[l]
        ctx_out = l < DEPTH - 1
        sh_x, sc_x, gt_x = adaln(c[:, None, :], ada_w[l], ada_b[l])
        sh_c, sc_c, gt_c = adaln(c_ctx[None, None, :], ada_w[l], ada_b[l])
        hx = rms_norm(x, norm_pre[l]) * (1.0 + sc_x) + sh_x
        hc = rms_norm(ctx, norm_pre[l]) * (1.0 + sc_c) + sh_c
        if kind == 0:
            ux, uc, v_first_x, v_first_c = rwkv7_mixer(hx, hc, p, v_first_x, v_first_c, ctx_out)
        elif kind == 1:
            ux, uc = hyena_mixer(hx, hc, p, ctx_out)
        else:
            ux, uc = hgrn2_mixer(hx, hc, p, lower_bounds[l], ctx_out)
        x = x + rms_norm(ux @ w_out[l], norm_post[l]) * gt_x
        if ctx_out:
            ctx = ctx + rms_norm(uc @ w_out[l], norm_post[l]) * gt_c
    return x
```

```python
import functools
import math

import jax
import jax.numpy as jnp
import numpy as np
from jax import lax
from jax.experimental import pallas as pl
from jax.experimental.pallas import tpu as pltpu

F32 = jnp.float32
BF16 = jnp.bfloat16
HIGHEST = lax.Precision.HIGHEST

NORM_EPS = 1e-6
GRID_W = 64
TOKEN_BLOCK = 256
LANE = 128
VMEM_LIMIT = 56 * 1024 * 1024

RWKV_HEAD = 64
RWKV_CHUNK = 64
RWKV_LANES = 512
RWKV_GN_EPS = 64e-5
TAIL_WIDTH = 384

HGRN_HEAD = 128
HGRN_SUB = 16
HGRN_LANES = 512

HYENA_EMB = 33
HYENA_FILTER_WIDTH = 64
HYENA_FAST_DECAY = 0.3
HYENA_SLOW_DECAY = 1.5
HYENA_TARGET = 1e-2
FFT_N2 = 64
FFT_LANES = 128


def _cparams(*sem):
    return pltpu.CompilerParams(dimension_semantics=sem, vmem_limit_bytes=VMEM_LIMIT)


def _dot(a, b, precision=None):
    return jnp.dot(a, b, preferred_element_type=F32, precision=precision)


def _dot_nt(a, b, precision=None):
    return lax.dot_general(a, b, (((1,), (1,)), ((), ())),
                           preferred_element_type=F32, precision=precision)


def _dot_tn(a, b):
    return lax.dot_general(a, b, (((0,), (0,)), ((), ())), preferred_element_type=F32)


def _bf(x):
    return x.astype(BF16)


def _split_dot(x, w_bf16):
    hi = x.astype(BF16)
    lo = (x - hi.astype(F32)).astype(BF16)
    return _dot(hi, w_bf16) + _dot(lo, w_bf16)


def _sigmoid(x):
    return 1.0 / (1.0 + jnp.exp(-x))


def _silu(x):
    return x * _sigmoid(x)


def _ada_kernel(c_ref, w_ref, b_ref, o_ref):
    o_ref[0] = _dot(_silu(c_ref[...]), w_ref[0], HIGHEST) + b_ref[0]


def _ada_all(cstack, ada_w, ada_b):
    depth, d, d3 = ada_w.shape
    rows = cstack.shape[0]
    nt = d3 // d
    return pl.pallas_call(
        _ada_kernel,
        grid=(depth, nt),
        in_specs=[
            pl.BlockSpec((rows, d), lambda l, j: (0, 0)),
            pl.BlockSpec((1, d, d), lambda l, j: (l, 0, j)),
            pl.BlockSpec((1, 1, d), lambda l, j: (l, 0, j)),
        ],
        out_specs=pl.BlockSpec((1, rows, d), lambda l, j: (l, 0, j)),
        out_shape=jax.ShapeDtypeStruct((depth, rows, d3), F32),
        compiler_params=_cparams("parallel", "parallel"),
        name="adaln",
    )(cstack, ada_w, ada_b.reshape(depth, 1, d3))


def _norm_kernel(*refs, shift, nxb):
    if shift:
        x_ref, xp_ref, xn_ref, g_ref, mod_ref, h_ref, d_ref = refs
    else:
        x_ref, g_ref, mod_ref, h_ref = refs
    i = pl.program_id(1)
    g = g_ref[...]
    shift_v = mod_ref[0, 0, 0:1, :]
    scale1p = 1.0 + mod_ref[0, 0, 1:2, :]

    def nrm(x):
        ms = jnp.mean(x * x, axis=-1, keepdims=True)
        return x * lax.rsqrt(ms + NORM_EPS) * g * scale1p + shift_v

    h = nrm(x_ref[0])
    h_ref[0] = h.astype(BF16)
    if not shift:
        return
    tb, d = h.shape
    q = d // 4
    row = lax.broadcasted_iota(jnp.int32, (tb, 1), 0)

    @pl.when(i < nxb)
    def _():
        col = row % GRID_W
        left = jnp.where(col > 0, pltpu.roll(h[:, 0:q], 1, 0), 0.0)
        right = jnp.where(col < GRID_W - 1, pltpu.roll(h[:, q:2 * q], tb - 1, 0), 0.0)
        hp = nrm(xp_ref[0])[:, 2 * q:3 * q]
        hn = nrm(xn_ref[0])[:, 3 * q:]
        hp = jnp.where(i > 0, hp, 0.0)
        hn = jnp.where(i < nxb - 1, hn, 0.0)
        up = jnp.concatenate([hp, h[:tb - GRID_W, 2 * q:3 * q]], axis=0)
        down = jnp.concatenate([h[GRID_W:, 3 * q:], hn], axis=0)
        hs = jnp.concatenate([left, right, up, down], axis=-1)
        d_ref[0] = (hs - h).astype(BF16)

    @pl.when(i >= nxb)
    def _():
        half = d // 2
        prev = jnp.where(row > 0, pltpu.roll(h[:, :half], 1, 0), 0.0)
        nxt = jnp.where(row < tb - 1, pltpu.roll(h[:, half:], tb - 1, 0), 0.0)
        hs = jnp.concatenate([prev, nxt], axis=-1)
        d_ref[0] = (hs - h).astype(BF16)


def _pre_norm(x, g, mod, *, shift, nxb):
    b, t, d = x.shape
    tb = TOKEN_BLOCK
    nblk = t // tb
    hb = tb // GRID_W
    nhalo = t // GRID_W
    seg = lambda i: jnp.where(i < nxb, 0, 1)
    x_spec = pl.BlockSpec((1, tb, d), lambda bb, i: (bb, i, 0))
    g_spec = pl.BlockSpec((1, d), lambda bb, i: (0, 0))
    mod_spec = pl.BlockSpec((1, 1, 3, d), lambda bb, i: (bb, seg(i), 0, 0))
    out_spec = pl.BlockSpec((1, tb, d), lambda bb, i: (bb, i, 0))
    if shift:
        in_specs = [
            x_spec,
            pl.BlockSpec((1, GRID_W, d), lambda bb, i: (bb, jnp.maximum(i * hb - 1, 0), 0)),
            pl.BlockSpec((1, GRID_W, d), lambda bb, i: (bb, jnp.minimum((i + 1) * hb, nhalo - 1), 0)),
            g_spec, mod_spec,
        ]
        args = (x, x, x, g, mod)
        out_specs = [out_spec, out_spec]
        out_shape = [jax.ShapeDtypeStruct((b, t, d), BF16)] * 2
    else:
        in_specs = [x_spec, g_spec, mod_spec]
        args = (x, g, mod)
        out_specs = out_spec
        out_shape = jax.ShapeDtypeStruct((b, t, d), BF16)
    return pl.pallas_call(
        functools.partial(_norm_kernel, shift=shift, nxb=nxb),
        grid=(b, nblk),
        in_specs=in_specs,
        out_specs=out_specs,
        out_shape=out_shape,
        compiler_params=_cparams("parallel", "parallel"),
        name="pre_norm_shift" if shift else "pre_norm",
    )(*args)


def _row_tile(m):
    for tm in (1024, 512, 256):
        if m % tm == 0:
            return tm
    raise ValueError(f"token count {m} is not a multiple of {TOKEN_BLOCK}")


def _proj_lerp_kernel(h_ref, d_ref, mu_ref, w_ref, o_ref, lhs_ref, *, tiles_per_group):
    j = pl.program_id(1)

    @pl.when(j % tiles_per_group == 0)
    def _():
        lhs_ref[...] = (h_ref[...].astype(F32) + mu_ref[0] * d_ref[...].astype(F32)).astype(BF16)

    o_ref[...] = _dot(lhs_ref[...], w_ref[...])


def _proj_kernel(h_ref, w_ref, o_ref):
    o_ref[...] = _dot(h_ref[...], w_ref[...])


def _project(h2, w_bf16, *, d2=None, mu=None, group_width=None):
    m, d = h2.shape
    n = w_bf16.shape[1]
    tm = _row_tile(m)
    tn = 1024
    lhs_spec = pl.BlockSpec((tm, d), lambda i, j: (i, 0))
    w_spec = pl.BlockSpec((d, tn), lambda i, j: (0, j))
    o_spec = pl.BlockSpec((tm, tn), lambda i, j: (i, j))
    if d2 is None:
        return pl.pallas_call(
            _proj_kernel, grid=(m // tm, n // tn),
            in_specs=[lhs_spec, w_spec], out_specs=o_spec,
            out_shape=jax.ShapeDtypeStruct((m, n), F32),
            compiler_params=_cparams("parallel", "parallel"),
            name="project",
        )(h2, w_bf16)
    tpg = group_width // tn
    return pl.pallas_call(
        functools.partial(_proj_lerp_kernel, tiles_per_group=tpg),
        grid=(m // tm, n // tn),
        in_specs=[lhs_spec, lhs_spec,
                  pl.BlockSpec((1, 1, d), lambda i, j: (j // tpg, 0, 0)),
                  w_spec],
        out_specs=o_spec,
        out_shape=jax.ShapeDtypeStruct((m, n), F32),
        scratch_shapes=[pltpu.VMEM((tm, d), BF16)],
        compiler_params=_cparams("parallel", "arbitrary"),
        name="project_lerp",
    )(h2, d2, mu, w_bf16)


def _tail_kernel(h_ref, d_ref, w_ref, mu_ref, o_ref):
    w = w_ref[...]
    o_ref[...] = _dot(h_ref[...], _bf(w)) + _dot(d_ref[...], _bf(w * mu_ref[...]))


def _project_tail(h2, d2, w_tail, mu_cols):
    m, d = h2.shape
    n = w_tail.shape[1]
    tm = _row_tile(m)
    lhs_spec = pl.BlockSpec((tm, d), lambda i: (i, 0))
    w_spec = pl.BlockSpec((d, n), lambda i: (0, 0))
    return pl.pallas_call(
        _tail_kernel, grid=(m // tm,),
        in_specs=[lhs_spec, lhs_spec, w_spec, w_spec],
        out_specs=pl.BlockSpec((tm, n), lambda i: (i, 0)),
        out_shape=jax.ShapeDtypeStruct((m, n), F32),
        compiler_params=_cparams("parallel"),
        name="project_tail",
    )(h2, d2, w_tail, mu_cols)


def _out_kernel(u_ref, w_ref, x_ref, g_ref, mod_ref, o_ref):
    y = _dot(_bf(u_ref[0]), w_ref[...])
    ms = jnp.mean(y * y, axis=-1, keepdims=True)
    yn = y * lax.rsqrt(ms + NORM_EPS) * g_ref[...]
    o_ref[0] = x_ref[0] + yn * mod_ref[0, 0, 2:3, :]


def _out_residual(u, w_bf16, x, g, mod, *, nxb, nblk_out):
    b, t, e = u.shape
    d = x.shape[-1]
    tb = TOKEN_BLOCK
    seg = lambda i: jnp.where(i < nxb, 0, 1)
    return pl.pallas_call(
        _out_kernel, grid=(b, nblk_out),
        in_specs=[
            pl.BlockSpec((1, tb, e), lambda bb, i: (bb, i, 0)),
            pl.BlockSpec((e, d), lambda bb, i: (0, 0)),
            pl.BlockSpec((1, tb, d), lambda bb, i: (bb, i, 0)),
            pl.BlockSpec((1, d), lambda bb, i: (0, 0)),
            pl.BlockSpec((1, 1, 3, d), lambda bb, i: (bb, seg(i), 0, 0)),
        ],
        out_specs=pl.BlockSpec((1, tb, d), lambda bb, i: (bb, i, 0)),
        out_shape=jax.ShapeDtypeStruct((b, nblk_out * tb, d), F32),
        compiler_params=_cparams("parallel", "parallel"),
        name="out_residual",
    )(u, w_bf16, x, g, mod)


def _rwkv_scan_kernel(*refs, reverse, vres, last):
    it = iter(refs)
    r_ref, k_ref, v_ref, lo_ref, w2_ref, a2_ref = (next(it) for _ in range(6))
    v2_ref = next(it) if vres else None
    vec_ref = next(it)
    vf_ref = next(it) if vres else None
    if last:
        gate_ref, y0_ref, b0_ref, u_ref = (next(it) for _ in range(4))
    else:
        yo_ref, bo_ref = next(it), next(it)
    s_ref, lw_s, r_s, kd_s, v_s, al_s, be_s, y_s, b_s = it

    i = pl.program_id(2)
    tb, lg = r_s.shape
    npair = lg // LANE
    c = RWKV_CHUNK
    nch = tb // c
    hd = RWKV_HEAD

    @pl.when(i == 0)
    def _():
        s_ref[...] = jnp.zeros_like(s_ref)

    vec = vec_ref[...]
    lo = lo_ref[0]
    lob = _bf(lo)
    r = r_ref[0]
    k = k_ref[0]
    v = v_ref[0]
    zt = vec[0:1] + _dot(_bf(jnp.tanh(lo)), w2_ref[...])
    lw_s[...] = -math.exp(-0.5) * _sigmoid(zt)
    a = _sigmoid(vec[1:2] + _dot(lob, a2_ref[...]))
    if vres:
        v = v + (vf_ref[0] - v) * _sigmoid(vec[5:6] + _dot(lob, v2_ref[...]))
    kdir = k * (1.0 + (a - 1.0) * vec[3:4])
    kkr = k * vec[2:3]
    bd = (lax.broadcasted_iota(jnp.int32, (LANE, LANE), 0) // hd
          == lax.broadcasted_iota(jnp.int32, (LANE, LANE), 1) // hd)
    ones_bd = jnp.where(bd, 1.0, 0.0).astype(BF16)

    def segsum(x):
        return jnp.concatenate(
            [_split_dot(x[:, p * LANE:(p + 1) * LANE], ones_bd) for p in range(npair)], axis=-1)

    kk = kkr / jnp.maximum(jnp.sqrt(segsum(kkr * kkr)), 1e-12)
    b_s[...] = segsum(r * kdir * vec[4:5]) * v
    r_s[...] = r
    kd_s[...] = kdir
    v_s[...] = v
    al_s[...] = -kk
    be_s[...] = kk * a

    t1 = lax.broadcasted_iota(jnp.int32, (c, c), 0)
    s1 = lax.broadcasted_iota(jnp.int32, (c, c), 1)
    t2 = lax.broadcasted_iota(jnp.int32, (c, LANE), 0)
    lane2 = lax.broadcasted_iota(jnp.int32, (c, LANE), 1)
    s2 = lane2 % c
    if reverse:
        incl1, incl2, strict2 = s1 >= t1, s2 >= t2, s2 > t2
    else:
        incl1, incl2, strict2 = s1 <= t1, s2 <= t2, s2 < t2
    tri = jnp.where(incl1, 1.0, 0.0)
    eye = jnp.where(s1 == t1, 1.0, 0.0)
    low = lane2 < hd
    low_x = lax.broadcasted_iota(jnp.int32, (2 * c, LANE), 1) < hd

    def chunk(cc, carry):
        ci = (nch - 1 - cc) if reverse else cc
        rows = pl.ds(pl.multiple_of(ci * c, c), c)
        lw = lw_s[rows, :]
        lc = _dot(tri, lw, HIGHEST)
        ltot = lc[0:1] if reverse else lc[c - 1:c]
        p_in = jnp.exp(lc)
        p_ex = jnp.exp(lc - lw)
        p_inv = jnp.exp(-lc)
        p_rem = jnp.exp(ltot - lc)
        p_all = jnp.exp(ltot)
        al = al_s[rows, :]
        be = be_s[rows, :]
        kd = kd_s[rows, :]
        ab = al * p_ex
        rb = r_s[rows, :] * p_in
        bt = be * p_inv
        kt = kd * p_inv
        bh = be * p_rem
        kh = kd * p_rem
        vv = v_s[rows, :]
        for p in range(npair):
            sl = slice(p * LANE, (p + 1) * LANE)
            x = jnp.concatenate([ab[:, sl], rb[:, sl]], axis=0)
            y = _bf(jnp.concatenate([bt[:, sl], kt[:, sl]], axis=0))
            vp = vv[:, sl]
            vdup = _bf(jnp.concatenate([vp, vp], axis=0))
            sp = s_ref[p]
            xs = _dot_nt(_bf(x), _bf(sp))
            u_pair = None
            gbs = []
            for hh in range(2):
                mine = low_x if hh == 0 else jnp.logical_not(low_x)
                g = _dot_nt(_bf(jnp.where(mine, x, 0.0)), y)
                gt = jnp.where(strict2, g[0:c], 0.0)
                gbs.append(_bf(jnp.where(incl2, g[c:], 0.0)))
                a_ab = gt[:, 0:c]
                rhs = xs[0:c] + _dot(_bf(jnp.where(low, 0.0, gt)), vdup)
                tm = eye + a_ab
                ap = a_ab
                for _ in range(int(math.log2(c)) - 1):
                    apb = _bf(ap)
                    ap = _dot(apb, apb)
                    tm = tm + _dot(_bf(tm), _bf(ap))
                u_h = _dot(_bf(tm), _bf(rhs))
                u_pair = u_h if hh == 0 else jnp.where(low, u_pair, u_h)
            uv = jnp.concatenate([u_pair, vp], axis=0)
            uvb = _bf(uv)
            y0 = xs[c:] + _dot(gbs[0], uvb)
            y1 = xs[c:] + _dot(gbs[1], uvb)
            y_s[rows, sl] = jnp.where(low, y0, y1)
            bk = _bf(jnp.concatenate([bh[:, sl], kh[:, sl]], axis=0))
            upd = _dot(_bf(uv.T), bk)
            s_ref[p] = sp * p_all[:, sl] + jnp.where(bd, upd, 0.0)
        return carry

    lax.fori_loop(0, nch, chunk, 0)

    if not last:
        yo_ref[0] = y_s[...]
        bo_ref[0] = b_s[...]
    else:
        yy = y0_ref[0] + y_s[...]
        mean = segsum(yy) * (1.0 / hd)
        yc = yy - mean
        var = segsum(yc * yc) * (1.0 / hd)
        yn = yc * lax.rsqrt(var + RWKV_GN_EPS) * vec[6:7] + vec[7:8]
        u_ref[0] = ((yn + b0_ref[0] + b_s[...]) * _silu(gate_ref[0])).astype(BF16)


def _rwkv_scan(proj, tail, w2p, a2p, v2p, vec, proj0, y0, b0, *, reverse, vres, last, nxb):
    b, t, e4 = proj.shape
    e = e4 // 4
    tb = TOKEN_BLOCK
    lg = RWKV_LANES
    ng = e // lg
    nblk = t // tb
    assert nblk == nxb + 1, "the context prefix must be exactly one token block"
    if reverse:
        blk = lambda i: jnp.where(i == 0, nxb, nxb - i)
    else:
        blk = lambda i: jnp.where(i == 0, nxb, i - 1)

    def col(off):
        return pl.BlockSpec((1, tb, lg), lambda bb, g, i: (bb, blk(i), off * ng + g))

    wspec = pl.BlockSpec((TAIL_WIDTH, lg), lambda bb, g, i: (0, g))
    in_specs = [col(0), col(1), col(2),
                pl.BlockSpec((1, tb, TAIL_WIDTH), lambda bb, g, i: (bb, blk(i), 0)),
                wspec, wspec]
    args = [proj, proj, proj, tail, w2p, a2p]
    if vres:
        in_specs.append(wspec)
        args.append(v2p)
    in_specs.append(pl.BlockSpec((8, lg), lambda bb, g, i: (0, g)))
    args.append(vec)
    if vres:
        in_specs.append(col(2))
        args.append(proj0)
    act = pl.BlockSpec((1, tb, lg), lambda bb, g, i: (bb, blk(i), g))
    if last:
        in_specs += [col(3), act, act]
        args += [proj, y0, b0]
        out_specs = act
        out_shape = jax.ShapeDtypeStruct((b, t, e), BF16)
    else:
        out_specs = [act, act]
        out_shape = [jax.ShapeDtypeStruct((b, t, e), F32)] * 2
    scratch = [pltpu.VMEM((lg // LANE, LANE, LANE), F32)] + [pltpu.VMEM((tb, lg), F32)] * 8
    return pl.pallas_call(
        functools.partial(_rwkv_scan_kernel, reverse=reverse, vres=vres, last=last),
        grid=(b, ng, nblk),
        in_specs=in_specs, out_specs=out_specs, out_shape=out_shape,
        scratch_shapes=scratch,
        compiler_params=_cparams("parallel", "parallel", "arbitrary"),
        name="rwkv_scan_bwd" if reverse else "rwkv_scan_fwd",
    )(*args)


def _rwkv_layer(x, mod, g_pre, g_post, w_out, p, proj0, *, nxb, nblk_out):
    b, t, d = x.shape
    e = p['k_k'].shape[0]
    vres = 'v0' in p
    h, dl = _pre_norm(x, g_pre, mod, shift=True, nxb=nxb)
    h2, d2 = h.reshape(b * t, d), dl.reshape(b * t, d)
    w_in = p['w_in']
    proj = _project(h2, _bf(w_in[:, :4 * e]), d2=d2, mu=p['mu'][:4, None, :],
                    group_width=e).reshape(b, t, 4 * e)
    n_lo = w_in.shape[1] - 4 * e
    groups = [4] * 128 + [5] * 128 + [2] * (n_lo - 256)
    pad = TAIL_WIDTH - n_lo
    w_tail = jnp.pad(w_in[:, 4 * e:], ((0, 0), (0, pad)))
    mu_cols = jnp.pad(p['mu'][np.asarray(groups)].T, ((0, 0), (0, pad)))
    tail = _project_tail(h2, d2, w_tail, mu_cols).reshape(b, t, TAIL_WIDTH)

    def lora(w, row0):
        return _bf(jnp.pad(w, ((row0, TAIL_WIDTH - row0 - w.shape[0]), (0, 0))))

    zero = jnp.zeros((e,), F32)
    y0 = b0 = None
    for z in range(2):
        vec = jnp.stack([p['w0'][z], p['a0'][z], p['k_k'], p['k_a'], p['r_k'].reshape(e),
                         p['v0'] if vres else zero, p['ln_w'], p['ln_b']])
        out = _rwkv_scan(proj, tail, lora(p['w2'][z], 64 * z), lora(p['a2'][z], 128 + 64 * z),
                         lora(p['v2'], 256) if vres else None, vec, proj0, y0, b0,
                         reverse=(z == 1), vres=vres, last=(z == 1), nxb=nxb)
        if z == 0:
            y0, b0 = out
    x_new = _out_residual(out, _bf(w_out), x, g_post, mod, nxb=nxb, nblk_out=nblk_out)
    return x_new, proj


def _hgrn_scan_kernel(*refs, reverse, last, layer):
    it = iter(refs)
    q_ref, f_ref, i_ref, lb_ref = (next(it) for _ in range(4))
    if last:
        gate_ref, o0_ref, gn_ref, u_ref = (next(it) for _ in range(4))
    else:
        oo_ref = next(it)
    s_ref, gc_s, q_s, k_s, v_s, qg_s, kd_s, pt_s, o_s = it

    i = pl.program_id(2)
    tb, lg = q_s.shape
    nh = lg // HGRN_HEAD
    c = HGRN_SUB
    nsub = tb // c

    @pl.when(i == 0)
    def _():
        s_ref[...] = jnp.zeros_like(s_ref)

    logits = lb_ref[...]
    ex = jnp.exp(logits - jnp.max(logits, axis=0, keepdims=True))
    lb = jnp.sum(ex[1:layer + 1], axis=0, keepdims=True) / jnp.sum(ex, axis=0, keepdims=True)
    f = lb + (1.0 - lb) * _sigmoid(f_ref[0])
    g = jnp.log(f)
    r1 = lax.broadcasted_iota(jnp.int32, (tb, tb), 0)
    c1 = lax.broadcasted_iota(jnp.int32, (tb, tb), 1)
    same = (r1 // c) == (c1 // c)
    before = (c1 >= r1) if reverse else (c1 <= r1)
    gc = _dot(jnp.where(same & before, 1.0, 0.0), g, HIGHEST)
    gtot = _dot(jnp.where(same, 1.0, 0.0), g, HIGHEST)
    q = _silu(q_ref[0])
    kk = 1.0 - f
    gc_s[...] = gc
    q_s[...] = q
    k_s[...] = kk
    v_s[...] = i_ref[0]
    qg_s[...] = q * jnp.exp(gc)
    kd_s[...] = kk * jnp.exp(gtot - gc)
    pt_s[...] = jnp.exp(gtot)

    trow = lax.broadcasted_iota(jnp.int32, (c, 1), 0)

    def sub(jj, carry):
        ji = (nsub - 1 - jj) if reverse else jj
        rows = pl.ds(pl.multiple_of(ji * c, c), c)
        for hd in range(nh):
            sl = slice(hd * HGRN_HEAD, (hd + 1) * HGRN_HEAD)
            st = s_ref[hd]
            o = _dot_nt(_bf(qg_s[rows, sl]), _bf(st))
            gcj = gc_s[rows, sl]
            qj = q_s[rows, sl]
            kj = k_s[rows, sl]
            vj = v_s[rows, sl]
            for s in range(c):
                valid = (trow <= s) if reverse else (trow >= s)
                w = qj * jnp.exp(jnp.where(valid, gcj - gcj[s:s + 1], -1e30)) * kj[s:s + 1]
                o = o + jnp.sum(w, axis=-1, keepdims=True) * vj[s:s + 1]
            o_s[rows, sl] = o
            upd = _dot_tn(_bf(vj), _bf(kd_s[rows, sl]))
            s_ref[hd] = st * pt_s[rows, sl][0:1] + upd
        return carry

    lax.fori_loop(0, nsub, sub, 0)

    if not last:
        oo_ref[0] = o_s[...]
    else:
        o = o0_ref[0] + o_s[...]
        gate = gate_ref[0]
        gn = gn_ref[...]
        for hd in range(nh):
            sl = slice(hd * HGRN_HEAD, (hd + 1) * HGRN_HEAD)
            oh = o[:, sl]
            ms = jnp.mean(oh * oh, axis=-1, keepdims=True)
            u_ref[0, :, sl] = (oh * lax.rsqrt(ms + NORM_EPS) * gn[:, sl] * _silu(gate[:, sl])).astype(BF16)


def _hgrn_scan(proj, lb_logits, gn, o0, *, reverse, last, nxb, layer):
    b, t, e5 = proj.shape
    e = e5 // 5
    tb = TOKEN_BLOCK
    lg = HGRN_LANES
    ng = e // lg
    nblk = t // tb
    assert nblk == nxb + 1, "the context prefix must be exactly one token block"
    if reverse:
        blk = lambda i: jnp.where(i == 0, nxb, nxb - i)
    else:
        blk = lambda i: jnp.where(i == 0, nxb, i - 1)

    def col(off):
        return pl.BlockSpec((1, tb, lg), lambda bb, g, i: (bb, blk(i), off * ng + g))

    row = pl.BlockSpec((1, lg), lambda bb, g, i: (0, g))
    act = pl.BlockSpec((1, tb, lg), lambda bb, g, i: (bb, blk(i), g))
    in_specs = [col(0), col(2 if reverse else 1), col(3),
                pl.BlockSpec((lb_logits.shape[0], lg), lambda bb, g, i: (0, g))]
    args = [proj, proj, proj, lb_logits]
    if last:
        in_specs += [col(4), act, row]
        args += [proj, o0, gn]
        out_shape = jax.ShapeDtypeStruct((b, t, e), BF16)
    else:
        out_shape = jax.ShapeDtypeStruct((b, t, e), F32)
    scratch = [pltpu.VMEM((lg // HGRN_HEAD, HGRN_HEAD, HGRN_HEAD), F32)] + [pltpu.VMEM((tb, lg), F32)] * 8
    return pl.pallas_call(
        functools.partial(_hgrn_scan_kernel, reverse=reverse, last=last, layer=layer),
        grid=(b, ng, nblk),
        in_specs=in_specs, out_specs=act, out_shape=out_shape,
        scratch_shapes=scratch,
        compiler_params=_cparams("parallel", "parallel", "arbitrary"),
        name="hgrn_scan_bwd" if reverse else "hgrn_scan_fwd",
    )(*args)


def _hgrn_layer(x, mod, g_pre, g_post, w_out, p, lb_logits, *, nxb, nblk_out, layer):
    b, t, d = x.shape
    e = lb_logits.shape[1]
    h = _pre_norm(x, g_pre, mod, shift=False, nxb=nxb)
    proj = _project(h.reshape(b * t, d), _bf(p['w_in'])).reshape(b, t, 5 * e)
    gn = jnp.tile(p['g_norm'], e // HGRN_HEAD)[None]
    o0 = _hgrn_scan(proj, lb_logits, gn, None, reverse=False, last=False, nxb=nxb, layer=layer)
    u = _hgrn_scan(proj, lb_logits, gn, o0, reverse=True, last=True, nxb=nxb, layer=layer)
    return _out_residual(u, _bf(w_out), x, g_post, mod, nxb=nxb, nblk_out=nblk_out)


def _hy_pre_kernel(*refs, nxb, nblk):
    cur = refs[0:4]
    prv = refs[4:7]
    nxt = refs[7:10]
    w = refs[10:13]
    bias = refs[13:16]
    u_ref, g0_ref = refs[16], refs[17]
    i = pl.program_id(1)
    tb = cur[0].shape[1]
    row = lax.broadcasted_iota(jnp.int32, (tb, 1), 0)
    has_prev = jnp.logical_and(i != 0, i != nxb)
    has_next = jnp.logical_and(i != nxb - 1, i != nblk - 1)

    def conv(j):
        x = cur[j][0]
        up = jnp.where(row == 0, jnp.where(has_prev, prv[j][0, 7:8, :], 0.0), pltpu.roll(x, 1, 0))
        dn = jnp.where(row == tb - 1, jnp.where(has_next, nxt[j][0, 0:1, :], 0.0),
                       pltpu.roll(x, tb - 1, 0))
        wj = w[j][...]
        return wj[0:1] * up + wj[1:2] * x + wj[2:3] * dn + bias[j][...]

    u_ref[0] = conv(2) * conv(1)
    g0_ref[0] = conv(0) * _silu(cur[3][0])


def _hy_pre(proj, conv_w, conv_b, *, nxb):
    b, t, e4 = proj.shape
    e = e4 // 4
    tb = TOKEN_BLOCK
    lg = 512
    ng = e // lg
    nblk = t // tb
    hb = tb // 8
    n8 = t // 8

    def col(off):
        return pl.BlockSpec((1, tb, lg), lambda bb, i, g: (bb, i, off * ng + g))

    def halo_prev(off):
        return pl.BlockSpec((1, 8, lg), lambda bb, i, g: (bb, jnp.maximum(i * hb - 1, 0), off * ng + g))

    def halo_next(off):
        return pl.BlockSpec((1, 8, lg), lambda bb, i, g: (bb, jnp.minimum((i + 1) * hb, n8 - 1), off * ng + g))

    def wcol(rows, off):
        return pl.BlockSpec((rows, lg), lambda bb, i, g: (0, off * ng + g))

    in_specs = ([col(o) for o in range(4)] + [halo_prev(o) for o in range(3)]
                + [halo_next(o) for o in range(3)] + [wcol(3, o) for o in range(3)]
                + [wcol(1, o) for o in range(3)])
    args = [proj] * 10 + [conv_w] * 3 + [conv_b[None]] * 3
    act = pl.BlockSpec((1, tb, lg), lambda bb, i, g: (bb, i, g))
    return pl.pallas_call(
        functools.partial(_hy_pre_kernel, nxb=nxb, nblk=nblk),
        grid=(b, nblk, ng),
        in_specs=in_specs, out_specs=[act, act],
        out_shape=[jax.ShapeDtypeStruct((b, t, e), F32)] * 2,
        compiler_params=_cparams("parallel", "parallel", "parallel"),
        name="hyena_short_conv",
    )(*args)


def _hy_filter_kernel(fv_ref, w1_ref, b1_ref, w2_ref, b2_ref, w3_ref, b3_ref, sf_ref,
                      w4f_ref, w4b_ref, dl_ref, hf_ref, hb_ref, *, length):
    tl = hf_ref.shape[0]
    n = (pl.program_id(0) * tl + lax.broadcasted_iota(jnp.int32, (tl, 1), 0)).astype(F32)
    t = n * (1.0 / (length - 1))
    lane = lax.broadcasted_iota(jnp.int32, (tl, LANE), 1)
    nb = (HYENA_EMB - 1) // 2
    ang = (2.0 * math.pi / length) * n * fv_ref[...]
    z = jnp.where(lane == 0, t,
                  jnp.where(lane <= nb, jnp.cos(ang),
                            jnp.where(lane <= 2 * nb, -jnp.sin(ang), 0.0)))
    sf = sf_ref[...]
    hdn = jnp.sin(sf * (_dot(z, w1_ref[...], HIGHEST) + b1_ref[...]))
    hdn = jnp.sin(sf * (_dot(hdn, w2_ref[...], HIGHEST) + b2_ref[...]))
    hdn = jnp.sin(sf * (_dot(hdn, w3_ref[...], HIGHEST) + b3_ref[...]))
    window = jnp.exp(-t * dl_ref[...])
    hf_ref[...] = _dot(hdn, w4f_ref[...], HIGHEST) * window
    hb_ref[...] = _dot(hdn, w4b_ref[...], HIGHEST) * window


def _hy_filters(length, p, e):
    tl = min(length, 1024)
    lg = 512
    ng = e // lg
    nb = (HYENA_EMB - 1) // 2
    freqs = np.linspace(1e-4, nb - 1, nb, dtype=np.float32)
    fv = np.zeros((1, LANE), np.float32)
    fv[0, 1:1 + nb] = freqs
    fv[0, 1 + nb:1 + 2 * nb] = freqs
    deltas = np.abs(np.linspace(math.log(HYENA_TARGET) / HYENA_SLOW_DECAY,
                                math.log(HYENA_TARGET) / HYENA_FAST_DECAY, e, dtype=np.float32))[None]
    fw = HYENA_FILTER_WIDTH
    w1 = jnp.pad(p['f_w1'], ((0, LANE - HYENA_EMB), (0, 0)))
    full = lambda shape: pl.BlockSpec(shape, lambda r, g: (0, 0))
    in_specs = [full((1, LANE)), full((LANE, fw)), full((1, fw)), full((fw, fw)), full((1, fw)),
                full((fw, fw)), full((1, fw)), full((1, fw)),
                pl.BlockSpec((fw, lg), lambda r, g: (0, g)),
                pl.BlockSpec((fw, lg), lambda r, g: (0, ng + g)),
                pl.BlockSpec((1, lg), lambda r, g: (0, g))]
    out = pl.BlockSpec((tl, lg), lambda r, g: (r, g))
    return pl.pallas_call(
        functools.partial(_hy_filter_kernel, length=length),
        grid=(length // tl, ng),
        in_specs=in_specs, out_specs=[out, out],
        out_shape=[jax.ShapeDtypeStruct((length, e), F32)] * 2,
        compiler_params=_cparams("parallel", "parallel"),
        name="hyena_filters",
    )(jnp.asarray(fv), w1, p['f_b1'][None], p['f_w2'], p['f_b2'][None], p['f_w3'], p['f_b3'][None],
      p['sin_freq'][None], p['f_w4'], p['f_w4'], jnp.asarray(deltas))


def _cmul(x, h, half):
    xr, xi = x[:half], x[half:]
    hr, hi = h[:half], h[half:]
    return jnp.concatenate([xr * hr - xi * hi, xr * hi + xi * hr], axis=0)


def _conj(x, half):
    return jnp.concatenate([x[:half], -x[half:]], axis=0)


def _long_conv_direct_kernel(u_ref, g0_ref, hf_ref, hb_ref, fb_ref, fd_ref, fi_ref, o_ref, h_s):
    half = fd_ref.shape[0] // 2

    @pl.when(pl.program_id(1) == 0)
    def _():
        h_s[...] = (_dot(fd_ref[...], hf_ref[...], HIGHEST)
                    + _conj(_dot(fd_ref[...], hb_ref[...], HIGHEST), half))

    u = u_ref[0]
    y = _dot(fi_ref[...], _cmul(_dot(fd_ref[...], u, HIGHEST), h_s[...], half), HIGHEST)
    o_ref[0] = (y + u * fb_ref[...]) * g0_ref[0]


def _long_conv_kernel(u_ref, g0_ref, hf_ref, hb_ref, fb_ref, t1_ref, t1t_ref, f2_ref, f2t_ref,
                      o_ref, a_s, h_s):
    n2, rows1, n1h = t1_ref.shape
    n1 = rows1 // 2
    f2 = f2_ref[...]
    f2t = f2t_ref[...]

    def stage1(load):
        def body(m2, carry):
            xg = load(pl.ds(m2, n1h, stride=n2))
            a_s[pl.ds(pl.multiple_of(m2 * rows1, rows1), rows1), :] = _dot(t1_ref[m2], xg, HIGHEST)
            return carry
        lax.fori_loop(0, n2, body, 0)

    def stage2(k1):
        zr = a_s[pl.ds(k1, n2, stride=rows1), :]
        zi = a_s[pl.ds(n1 + k1, n2, stride=rows1), :]
        return _dot(f2, jnp.concatenate([zr, zi], axis=0), HIGHEST)

    @pl.when(pl.program_id(1) == 0)
    def _():
        stage1(lambda idx: hf_ref[idx, :])

        def spec_f(k1, carry):
            h_s[pl.ds(pl.multiple_of(k1 * 2 * n2, 2 * n2), 2 * n2), :] = stage2(k1)
            return carry
        lax.fori_loop(0, n1, spec_f, 0)
        stage1(lambda idx: hb_ref[idx, :])

        def spec_b(k1, carry):
            rows = pl.ds(pl.multiple_of(k1 * 2 * n2, 2 * n2), 2 * n2)
            h_s[rows, :] = h_s[rows, :] + _conj(stage2(k1), n2)
            return carry
        lax.fori_loop(0, n1, spec_b, 0)

    stage1(lambda idx: u_ref[0, idx, :])

    def mid(k1, carry):
        rows = pl.ds(pl.multiple_of(k1 * 2 * n2, 2 * n2), 2 * n2)
        z = _dot(f2t, _cmul(stage2(k1), h_s[rows, :], n2), HIGHEST)
        a_s[pl.ds(k1, n2, stride=rows1), :] = z[:n2]
        a_s[pl.ds(n1 + k1, n2, stride=rows1), :] = z[n2:]
        return carry
    lax.fori_loop(0, n1, mid, 0)

    fb = fb_ref[...]

    def inv1(m2, carry):
        blk = a_s[pl.ds(pl.multiple_of(m2 * rows1, rows1), rows1), :]
        y = _dot(t1t_ref[m2], blk, HIGHEST)
        idx = pl.ds(m2, n1h, stride=n2)
        o_ref[0, idx, :] = (y + u_ref[0, idx, :] * fb) * g0_ref[0, idx, :]
        return carry
    lax.fori_loop(0, n2, inv1, 0)


def _fft_tables(length):
    n = 2 * length
    n2 = FFT_N2
    n1 = n // n2
    n1h = n1 // 2
    k1 = np.arange(n1, dtype=np.float64)[:, None]
    m1 = np.arange(n1h, dtype=np.float64)[None, :]
    t1 = []
    for m2 in range(n2):
        phi = 2.0 * np.pi * (k1 * m1 / n1 + k1 * m2 / n)
        t1.append(np.concatenate([np.cos(phi), -np.sin(phi)], axis=0))
    t1 = np.stack(t1)
    t1t = np.transpose(t1, (0, 2, 1)) / n
    k2 = np.arange(n2, dtype=np.float64)[:, None]
    m2 = np.arange(n2, dtype=np.float64)[None, :]
    th = 2.0 * np.pi * k2 * m2 / n2
    mr, mi = np.cos(th), -np.sin(th)
    f2 = np.block([[mr, -mi], [mi, mr]])
    as32 = lambda a: jnp.asarray(a.astype(np.float32))
    return as32(t1), as32(t1t), as32(f2), as32(f2.T)


def _dft_tables(length):
    n = 2 * length
    k = np.arange(n, dtype=np.float64)[:, None]
    m = np.arange(length, dtype=np.float64)[None, :]
    phi = 2.0 * np.pi * k * m / n
    fd = np.concatenate([np.cos(phi), -np.sin(phi)], axis=0)
    fi = fd.T / n
    return jnp.asarray(fd.astype(np.float32)), jnp.asarray(fi.astype(np.float32))


def _long_conv(u, g0, hf, hb, fbias, *, length, blk_index):
    b, t, e = u.shape
    lg = FFT_LANES
    ng = e // lg
    seq_spec = pl.BlockSpec((1, length, lg), lambda g, bb: (bb, blk_index, g))
    filt_spec = pl.BlockSpec((length, lg), lambda g, bb: (0, g))
    row_spec = pl.BlockSpec((1, lg), lambda g, bb: (0, g))
    out_spec = pl.BlockSpec((1, length, lg), lambda g, bb: (bb, 0, g))
    common = dict(
        grid=(ng, b),
        out_specs=out_spec,
        out_shape=jax.ShapeDtypeStruct((b, length, e), F32),
        compiler_params=_cparams("parallel", "arbitrary"),
    )
    const = lambda a: pl.BlockSpec(a.shape, lambda g, bb: (0,) * a.ndim)
    if length <= TOKEN_BLOCK:
        fd, fi = _dft_tables(length)
        return pl.pallas_call(
            _long_conv_direct_kernel,
            in_specs=[seq_spec, seq_spec, filt_spec, filt_spec, row_spec, const(fd), const(fi)],
            scratch_shapes=[pltpu.VMEM((4 * length, lg), F32)],
            name="hyena_long_conv_ctx", **common,
        )(u, g0, hf, hb, fbias, fd, fi)
    tabs = _fft_tables(length)
    return pl.pallas_call(
        _long_conv_kernel,
        in_specs=[seq_spec, seq_spec, filt_spec, filt_spec, row_spec] + [const(a) for a in tabs],
        scratch_shapes=[pltpu.VMEM((4 * length, lg), F32), pltpu.VMEM((4 * length, lg), F32)],
        name="hyena_long_conv", **common,
    )(u, g0, hf, hb, fbias, *tabs)


def _hyena_layer(x, mod, g_pre, g_post, w_out, p, *, nxb, nblk_out):
    b, t, d = x.shape
    e = p['filter_bias'].shape[0]
    seq = nxb * TOKEN_BLOCK
    ctx_len = t - seq
    h = _pre_norm(x, g_pre, mod, shift=False, nxb=nxb)
    proj = _project(h.reshape(b * t, d), _bf(p['w_in'])).reshape(b, t, 4 * e)
    u, g0 = _hy_pre(proj, p['conv_w'], p['conv_b'], nxb=nxb)
    fbias = p['filter_bias'][None]
    hf, hb = _hy_filters(seq, p, e)
    yx = _long_conv(u, g0, hf, hb, fbias, length=seq, blk_index=0)
    hf, hb = _hy_filters(ctx_len, p, e)
    yc = _long_conv(u, g0, hf, hb, fbias, length=ctx_len, blk_index=seq // ctx_len)
    y = jnp.concatenate([yx, yc], axis=1)
    return _out_residual(y, _bf(w_out), x, g_post, mod, nxb=nxb, nblk_out=nblk_out)


def _modulation(c, c_ctx, ada_w, ada_b):
    b, d = c.shape
    depth = ada_w.shape[0]
    rows = -(-(b + 1) // 8) * 8
    cstack = jnp.zeros((rows, d), F32).at[:b].set(c).at[b].set(c_ctx)
    ada = _ada_all(cstack, ada_w, ada_b).reshape(depth, rows, 3, d)
    lat = ada[:, :b]
    cx = jnp.broadcast_to(ada[:, b:b + 1], lat.shape)
    return jnp.stack([lat, cx], axis=2)


def kernel(x, c, ctx, c_ctx, ada_w, ada_b, norm_pre, norm_post, w_out,
           l0_w_in, l0_mu, l0_w0, l0_w2, l0_a0, l0_a2, l0_k_k, l0_k_a, l0_r_k, l0_ln_w, l0_ln_b,
           l1_w_in, l1_conv_w, l1_conv_b, l1_f_w1, l1_f_b1, l1_f_w2, l1_f_b2, l1_f_w3, l1_f_b3,
           l1_f_w4, l1_sin_freq, l1_filter_bias,
           l2_w_in, l2_g_norm, hgrn_lb_logits,
           l3_w_in, l3_mu, l3_w0, l3_w2, l3_a0, l3_a2, l3_k_k, l3_k_a, l3_r_k, l3_ln_w, l3_ln_b,
           l3_v0, l3_v2):
    rwkv0 = dict(w_in=l0_w_in, mu=l0_mu, w0=l0_w0, w2=l0_w2, a0=l0_a0, a2=l0_a2, k_k=l0_k_k,
                 k_a=l0_k_a, r_k=l0_r_k, ln_w=l0_ln_w, ln_b=l0_ln_b)
    hyena1 = dict(w_in=l1_w_in, conv_w=l1_conv_w, conv_b=l1_conv_b, f_w1=l1_f_w1, f_b1=l1_f_b1,
                  f_w2=l1_f_w2, f_b2=l1_f_b2, f_w3=l1_f_w3, f_b3=l1_f_b3, f_w4=l1_f_w4,
                  sin_freq=l1_sin_freq, filter_bias=l1_filter_bias)
    hgrn2 = dict(w_in=l2_w_in, g_norm=l2_g_norm)
    rwkv3 = dict(w_in=l3_w_in, mu=l3_mu, w0=l3_w0, w2=l3_w2, a0=l3_a0, a2=l3_a2, k_k=l3_k_k,
                 k_a=l3_k_a, r_k=l3_r_k, ln_w=l3_ln_w, ln_b=l3_ln_b, v0=l3_v0, v2=l3_v2)
    b, seq, d = x.shape
    assert ctx.shape[1] == TOKEN_BLOCK and seq % TOKEN_BLOCK == 0
    nxb = seq // TOKEN_BLOCK
    mods = _modulation(c, c_ctx, ada_w, ada_b)
    xa = jnp.concatenate([x, ctx], axis=1)
    xa, proj0 = _rwkv_layer(xa, mods[0], norm_pre[0][None], norm_post[0][None], w_out[0], rwkv0,
                            None, nxb=nxb, nblk_out=nxb + 1)
    xa = _hyena_layer(xa, mods[1], norm_pre[1][None], norm_post[1][None], w_out[1], hyena1,
                      nxb=nxb, nblk_out=nxb + 1)
    xa = _hgrn_layer(xa, mods[2], norm_pre[2][None], norm_post[2][None], w_out[2], hgrn2,
                     hgrn_lb_logits, nxb=nxb, nblk_out=nxb + 1, layer=2)
    xa, _ = _rwkv_layer(xa, mods[3], norm_pre[3][None], norm_post[3][None], w_out[3], rwkv3,
                        proj0, nxb=nxb, nblk_out=nxb)
    return xa
```

```python
import functools
import math

import jax
import jax.numpy as jnp
import numpy as np
from jax import lax
from jax.experimental import pallas as pl
from jax.experimental.pallas import tpu as pltpu

F32 = jnp.float32
BF16 = jnp.bfloat16
HIGHEST = lax.Precision.HIGHEST

NORM_EPS = 1e-6
GRID_W = 64
TOKEN_BLOCK = 256
LANE = 128
VMEM_LIMIT = 56 * 1024 * 1024

RWKV_HEAD = 64
RWKV_CHUNK = 64
RWKV_LANES = 512
RWKV_GN_EPS = 64e-5
TAIL_WIDTH = 384

HGRN_HEAD = 128
HGRN_SUB = 16
HGRN_LANES = 512

HYENA_EMB = 33
HYENA_FILTER_WIDTH = 64
HYENA_FAST_DECAY = 0.3
HYENA_SLOW_DECAY = 1.5
HYENA_TARGET = 1e-2
FFT_N2 = 64
FFT_LANES = 128
FFT_TABLE_PARTS = ("hi",)
FFT_UNROLL = 8


def _cparams(*sem):
    return pltpu.CompilerParams(dimension_semantics=sem, vmem_limit_bytes=VMEM_LIMIT)


def _dot(a, b, precision=None):
    return jnp.dot(a, b, preferred_element_type=F32, precision=precision)


def _dot_nt(a, b, precision=None):
    return lax.dot_general(a, b, (((1,), (1,)), ((), ())),
                           preferred_element_type=F32, precision=precision)


def _dot_tn(a, b):
    return lax.dot_general(a, b, (((0,), (0,)), ((), ())), preferred_element_type=F32)


def _bf(x):
    return x.astype(BF16)


def _split_dot(x, w_bf16):
    hi = x.astype(BF16)
    lo = (x - hi.astype(F32)).astype(BF16)
    return _dot(hi, w_bf16) + _dot(lo, w_bf16)


def _mask_dot(m_bf16, x):
    hi = x.astype(BF16)
    lo = (x - hi.astype(F32)).astype(BF16)
    return _dot(m_bf16, hi) + _dot(m_bf16, lo)


def _tree_sum(terms):
    terms = list(terms)
    while len(terms) > 1:
        terms = [a + b for a, b in zip(terms[0::2], terms[1::2])] + (terms[-1:] if len(terms) % 2 else [])
    return terms[0]


def _sigmoid(x):
    return 1.0 / (1.0 + jnp.exp(-x))


def _silu(x):
    return x * _sigmoid(x)


def _ada_kernel(c_ref, w_ref, b_ref, o_ref):
    o_ref[0] = _dot(_silu(c_ref[...]), w_ref[0], HIGHEST) + b_ref[0]


def _ada_all(cstack, ada_w, ada_b):
    depth, d, d3 = ada_w.shape
    rows = cstack.shape[0]
    nt = d3 // d
    return pl.pallas_call(
        _ada_kernel,
        grid=(depth, nt),
        in_specs=[
            pl.BlockSpec((rows, d), lambda l, j: (0, 0)),
            pl.BlockSpec((1, d, d), lambda l, j: (l, 0, j)),
            pl.BlockSpec((1, 1, d), lambda l, j: (l, 0, j)),
        ],
        out_specs=pl.BlockSpec((1, rows, d), lambda l, j: (l, 0, j)),
        out_shape=jax.ShapeDtypeStruct((depth, rows, d3), F32),
        compiler_params=_cparams("parallel", "parallel"),
        name="adaln",
    )(cstack, ada_w, ada_b.reshape(depth, 1, d3))


def _norm_kernel(*refs, shift, nxb):
    if shift:
        x_ref, xp_ref, xn_ref, g_ref, mod_ref, h_ref, d_ref = refs
    else:
        x_ref, g_ref, mod_ref, h_ref = refs
    i = pl.program_id(1)
    g = g_ref[...]
    shift_v = mod_ref[0, 0, 0:1, :]
    scale1p = 1.0 + mod_ref[0, 0, 1:2, :]

    def nrm(x):
        ms = jnp.mean(x * x, axis=-1, keepdims=True)
        return x * lax.rsqrt(ms + NORM_EPS) * g * scale1p + shift_v

    h = nrm(x_ref[0])
    h_ref[0] = h.astype(BF16)
    if not shift:
        return
    tb, d = h.shape
    q = d // 4
    row = lax.broadcasted_iota(jnp.int32, (tb, 1), 0)

    @pl.when(i < nxb)
    def _():
        col = row % GRID_W
        left = jnp.where(col > 0, pltpu.roll(h[:, 0:q], 1, 0), 0.0)
        right = jnp.where(col < GRID_W - 1, pltpu.roll(h[:, q:2 * q], tb - 1, 0), 0.0)
        hp = nrm(xp_ref[0])[:, 2 * q:3 * q]
        hn = nrm(xn_ref[0])[:, 3 * q:]
        hp = jnp.where(i > 0, hp, 0.0)
        hn = jnp.where(i < nxb - 1, hn, 0.0)
        up = jnp.concatenate([hp, h[:tb - GRID_W, 2 * q:3 * q]], axis=0)
        down = jnp.concatenate([h[GRID_W:, 3 * q:], hn], axis=0)
        hs = jnp.concatenate([left, right, up, down], axis=-1)
        d_ref[0] = (hs - h).astype(BF16)

    @pl.when(i >= nxb)
    def _():
        half = d // 2
        prev = jnp.where(row > 0, pltpu.roll(h[:, :half], 1, 0), 0.0)
        nxt = jnp.where(row < tb - 1, pltpu.roll(h[:, half:], tb - 1, 0), 0.0)
        hs = jnp.concatenate([prev, nxt], axis=-1)
        d_ref[0] = (hs - h).astype(BF16)


def _pre_norm(x, g, mod, *, shift, nxb):
    b, t, d = x.shape
    tb = TOKEN_BLOCK
    nblk = t // tb
    hb = tb // GRID_W
    nhalo = t // GRID_W
    seg = lambda i: jnp.where(i < nxb, 0, 1)
    x_spec = pl.BlockSpec((1, tb, d), lambda bb, i: (bb, i, 0))
    g_spec = pl.BlockSpec((1, d), lambda bb, i: (0, 0))
    mod_spec = pl.BlockSpec((1, 1, 3, d), lambda bb, i: (bb, seg(i), 0, 0))
    out_spec = pl.BlockSpec((1, tb, d), lambda bb, i: (bb, i, 0))
    if shift:
        in_specs = [
            x_spec,
            pl.BlockSpec((1, GRID_W, d), lambda bb, i: (bb, jnp.maximum(i * hb - 1, 0), 0)),
            pl.BlockSpec((1, GRID_W, d), lambda bb, i: (bb, jnp.minimum((i + 1) * hb, nhalo - 1), 0)),
            g_spec, mod_spec,
        ]
        args = (x, x, x, g, mod)
        out_specs = [out_spec, out_spec]
        out_shape = [jax.ShapeDtypeStruct((b, t, d), BF16)] * 2
    else:
        in_specs = [x_spec, g_spec, mod_spec]
        args = (x, g, mod)
        out_specs = out_spec
        out_shape = jax.ShapeDtypeStruct((b, t, d), BF16)
    return pl.pallas_call(
        functools.partial(_norm_kernel, shift=shift, nxb=nxb),
        grid=(b, nblk),
        in_specs=in_specs,
        out_specs=out_specs,
        out_shape=out_shape,
        compiler_params=_cparams("parallel", "parallel"),
        name="pre_norm_shift" if shift else "pre_norm",
    )(*args)


def _row_tile(m):
    for tm in (1024, 512, 256):
        if m % tm == 0:
            return tm
    raise ValueError(f"token count {m} is not a multiple of {TOKEN_BLOCK}")


def _proj_lerp_kernel(h_ref, d_ref, mu_ref, w_ref, o_ref, lhs_ref, *, tiles_per_group):
    j = pl.program_id(1)

    @pl.when(j % tiles_per_group == 0)
    def _():
        lhs_ref[...] = (h_ref[...].astype(F32) + mu_ref[0] * d_ref[...].astype(F32)).astype(BF16)

    o_ref[...] = _dot(lhs_ref[...], w_ref[...])


def _proj_kernel(h_ref, w_ref, o_ref):
    o_ref[...] = _dot(h_ref[...], w_ref[...])


def _project(h2, w_bf16, *, d2=None, mu=None, group_width=None):
    m, d = h2.shape
    n = w_bf16.shape[1]
    tm = _row_tile(m)
    tn = 1024
    lhs_spec = pl.BlockSpec((tm, d), lambda i, j: (i, 0))
    w_spec = pl.BlockSpec((d, tn), lambda i, j: (0, j))
    o_spec = pl.BlockSpec((tm, tn), lambda i, j: (i, j))
    if d2 is None:
        return pl.pallas_call(
            _proj_kernel, grid=(m // tm, n // tn),
            in_specs=[lhs_spec, w_spec], out_specs=o_spec,
            out_shape=jax.ShapeDtypeStruct((m, n), F32),
            compiler_params=_cparams("parallel", "parallel"),
            name="project",
        )(h2, w_bf16)
    tpg = group_width // tn
    return pl.pallas_call(
        functools.partial(_proj_lerp_kernel, tiles_per_group=tpg),
        grid=(m // tm, n // tn),
        in_specs=[lhs_spec, lhs_spec,
                  pl.BlockSpec((1, 1, d), lambda i, j: (j // tpg, 0, 0)),
                  w_spec],
        out_specs=o_spec,
        out_shape=jax.ShapeDtypeStruct((m, n), F32),
        scratch_shapes=[pltpu.VMEM((tm, d), BF16)],
        compiler_params=_cparams("parallel", "arbitrary"),
        name="project_lerp",
    )(h2, d2, mu, w_bf16)


def _tail_kernel(h_ref, d_ref, w_ref, mu_ref, o_ref):
    w = w_ref[...]
    o_ref[...] = _dot(h_ref[...], _bf(w)) + _dot(d_ref[...], _bf(w * mu_ref[...]))


def _project_tail(h2, d2, w_tail, mu_cols):
    m, d = h2.shape
    n = w_tail.shape[1]
    tm = _row_tile(m)
    lhs_spec = pl.BlockSpec((tm, d), lambda i: (i, 0))
    w_spec = pl.BlockSpec((d, n), lambda i: (0, 0))
    return pl.pallas_call(
        _tail_kernel, grid=(m // tm,),
        in_specs=[lhs_spec, lhs_spec, w_spec, w_spec],
        out_specs=pl.BlockSpec((tm, n), lambda i: (i, 0)),
        out_shape=jax.ShapeDtypeStruct((m, n), F32),
        compiler_params=_cparams("parallel"),
        name="project_tail",
    )(h2, d2, w_tail, mu_cols)


def _out_kernel(u_ref, w_ref, x_ref, g_ref, mod_ref, o_ref):
    y = _dot(_bf(u_ref[0]), w_ref[...])
    ms = jnp.mean(y * y, axis=-1, keepdims=True)
    yn = y * lax.rsqrt(ms + NORM_EPS) * g_ref[...]
    o_ref[0] = x_ref[0] + yn * mod_ref[0, 0, 2:3, :]


def _out_residual(u, w_bf16, x, g, mod, *, nxb, nblk_out):
    b, t, e = u.shape
    d = x.shape[-1]
    tb = TOKEN_BLOCK
    seg = lambda i: jnp.where(i < nxb, 0, 1)
    return pl.pallas_call(
        _out_kernel, grid=(b, nblk_out),
        in_specs=[
            pl.BlockSpec((1, tb, e), lambda bb, i: (bb, i, 0)),
            pl.BlockSpec((e, d), lambda bb, i: (0, 0)),
            pl.BlockSpec((1, tb, d), lambda bb, i: (bb, i, 0)),
            pl.BlockSpec((1, d), lambda bb, i: (0, 0)),
            pl.BlockSpec((1, 1, 3, d), lambda bb, i: (bb, seg(i), 0, 0)),
        ],
        out_specs=pl.BlockSpec((1, tb, d), lambda bb, i: (bb, i, 0)),
        out_shape=jax.ShapeDtypeStruct((b, nblk_out * tb, d), F32),
        compiler_params=_cparams("parallel", "parallel"),
        name="out_residual",
    )(u, w_bf16, x, g, mod)


def _rwkv_scan_kernel(*refs, reverse, vres, last):
    it = iter(refs)
    r_ref, k_ref, v_ref, lo_ref, w2_ref, a2_ref = (next(it) for _ in range(6))
    v2_ref = next(it) if vres else None
    vec_ref = next(it)
    vf_ref = next(it) if vres else None
    if last:
        gate_ref, y0_ref, b0_ref, u_ref = (next(it) for _ in range(4))
    else:
        yo_ref, bo_ref = next(it), next(it)
    s_ref, y_s, b_s = it

    i = pl.program_id(2)
    tb, lg = y_s.shape
    npair = lg // LANE
    c = RWKV_CHUNK
    nch = tb // c
    hd = RWKV_HEAD

    @pl.when(i == 0)
    def _():
        s_ref[...] = jnp.zeros_like(s_ref)

    vec = vec_ref[...]
    lo = lo_ref[0]
    r = r_ref[0]
    k = k_ref[0]
    v = v_ref[0]
    zt = vec[0:1] + _dot(_bf(jnp.tanh(lo[:, :LANE])), w2_ref[...])
    logw = -math.exp(-0.5) * _sigmoid(zt)
    a = _sigmoid(vec[1:2] + _dot(_bf(lo[:, LANE:2 * LANE]), a2_ref[...]))
    if vres:
        v = v + (vf_ref[0] - v) * _sigmoid(vec[5:6] + _dot(_bf(lo[:, 2 * LANE:]), v2_ref[...]))
    kdir = k * (1.0 + (a - 1.0) * vec[3:4])
    kkr = k * vec[2:3]
    bd = (lax.broadcasted_iota(jnp.int32, (LANE, LANE), 0) // hd
          == lax.broadcasted_iota(jnp.int32, (LANE, LANE), 1) // hd)
    ones_bd = jnp.where(bd, 1.0, 0.0).astype(BF16)

    def segsum(x):
        return jnp.concatenate(
            [_dot(_bf(x[:, p * LANE:(p + 1) * LANE]), ones_bd) for p in range(npair)], axis=-1)

    kk = kkr / jnp.maximum(jnp.sqrt(segsum(kkr * kkr)), 1e-12)
    b_s[...] = segsum(r * kdir * vec[4:5]) * v
    alpha = -kk
    beta = kk * a

    t1 = lax.broadcasted_iota(jnp.int32, (c, c), 0)
    s1 = lax.broadcasted_iota(jnp.int32, (c, c), 1)
    t2 = lax.broadcasted_iota(jnp.int32, (c, LANE), 0)
    lane2 = lax.broadcasted_iota(jnp.int32, (c, LANE), 1)
    s2 = lane2 % c
    if reverse:
        incl1, incl2, strict2 = s1 >= t1, s2 >= t2, s2 > t2
    else:
        incl1, incl2, strict2 = s1 <= t1, s2 <= t2, s2 < t2
    tri = jnp.where(incl1, 1.0, 0.0).astype(BF16)
    low = lane2 < hd
    high = jnp.logical_not(low)
    low_x = lax.broadcasted_iota(jnp.int32, (2 * c, LANE), 1) < hd
    strict_lo, strict_hi = strict2 & low, strict2 & high
    incl_lo, incl_hi = incl2 & low, incl2 & high
    eye2 = jnp.where(lax.broadcasted_iota(jnp.int32, (LANE, LANE), 0)
                     == lax.broadcasted_iota(jnp.int32, (LANE, LANE), 1), 1.0, 0.0)
    zeros_cv = jnp.zeros((c, LANE), F32)

    def stack(top, bot):
        return jnp.concatenate([top, bot], axis=0)

    def fold(z):
        return z[:c] + z[c:]

    ch = []
    for ci in range(nch):
        rows = slice(ci * c, (ci + 1) * c)
        lw = logw[rows]
        lc = _mask_dot(tri, lw)
        ltot = lc[0:1] if reverse else lc[c - 1:c]
        p_inv = jnp.exp(-lc)
        p_rem = jnp.exp(ltot - lc)
        p_all = jnp.exp(ltot)
        ab = alpha[rows] * jnp.exp(lc - lw)
        rb = r[rows] * jnp.exp(lc)
        bt = beta[rows] * p_inv
        kt = kdir[rows] * p_inv
        bh = beta[rows] * p_rem
        kh = kdir[rows] * p_rem
        for p in range(npair):
            sl = slice(p * LANE, (p + 1) * LANE)
            ch.append(dict(ci=ci, p=p, ab=ab[:, sl], rb=rb[:, sl], bt=bt[:, sl], kt=kt[:, sl],
                           bh=bh[:, sl], kh=kh[:, sl], v=v[rows, sl], p_all=p_all[:, sl]))
    for d in ch:
        y01 = jnp.concatenate([jnp.where(low_x, stack(d['bt'], d['kt']), 0.0),
                               jnp.where(low_x, 0.0, stack(d['kt'], d['bt']))], axis=0)
        d['g'] = _dot_nt(_bf(stack(d['ab'], d['rb'])), _bf(y01))
    for d in ch:
        g0t, g0b = d['g'][:c, :LANE], d['g'][c:, :LANE]
        g1t, g1b = d['g'][:c, LANE:], d['g'][c:, LANE:]
        d['a'] = stack(jnp.where(strict_lo, g0t, 0.0), jnp.where(strict_hi, g1t, 0.0))
        arb = stack(jnp.where(incl_lo, g0b, 0.0), jnp.where(incl_hi, g1b, 0.0))
        ark = stack(jnp.where(incl_hi, g0b, 0.0), jnp.where(incl_lo, g1b, 0.0))
        d['arbk'] = _bf(jnp.concatenate([arb, ark], axis=1))
        ak = stack(jnp.where(strict_hi, g0t, 0.0), jnp.where(strict_lo, g1t, 0.0))
        d['vx'] = stack(jnp.where(high, d['v'], 0.0), jnp.where(low, d['v'], 0.0))
        d['w'] = _dot(_bf(ak), _bf(d['vx']))
        del d['g']
    rounds = int(math.log2(c)) - 1
    for d in ch:
        d['tm'] = eye2 + d['a']
        apb = _bf(d['a'])
        d['a'] = _dot(apb, apb)
    for j in range(rounds):
        for d in ch:
            if j < rounds - 1:
                pt = _dot(_bf(d['a']), _bf(jnp.concatenate([d['a'], d['tm']], axis=1)))
                d['a'] = pt[:, :LANE]
                d['tm'] = d['tm'] + pt[:, LANE:]
            else:
                d['tm'] = d['tm'] + _dot(_bf(d['a']), _bf(d['tm']))
    for d in ch:
        ab_st = stack(jnp.where(low, d['ab'], 0.0), jnp.where(high, d['ab'], 0.0))
        d['tz'] = _dot(_bf(d['tm']), _bf(jnp.concatenate([ab_st, d['w']], axis=1)))
    for d in ch:
        lower = jnp.concatenate([jnp.zeros((2 * c, LANE), F32), d['vx']], axis=1)
        yz = _dot(d['arbk'], _bf(stack(d['tz'], lower)))
        ta, tw = fold(d['tz'][:, :LANE]), fold(d['tz'][:, LANE:])
        d['ra'] = _bf(d['rb'] + fold(yz[:, :LANE]))
        d['yw'] = fold(yz[:, LANE:])
        lhs = stack(jnp.concatenate([ta, tw], axis=1), jnp.concatenate([zeros_cv, d['v']], axis=1))
        mn = _dot_tn(_bf(lhs), _bf(stack(d['bh'], d['kh'])))
        d['m'] = _bf(jnp.where(bd, mn[:LANE], 0.0))
        d['n'] = jnp.where(bd, mn[LANE:], 0.0)

    state = [s_ref[p] for p in range(npair)]
    for step in range(nch):
        ci = (nch - 1 - step) if reverse else step
        for p in range(npair):
            d = ch[ci * npair + p]
            sp = state[p]
            spb = _bf(sp)
            y_s[ci * c:(ci + 1) * c, p * LANE:(p + 1) * LANE] = _dot_nt(d['ra'], spb) + d['yw']
            state[p] = sp * d['p_all'] + _dot(spb, d['m']) + d['n']
    for p in range(npair):
        s_ref[p] = state[p]

    if not last:
        yo_ref[0] = y_s[...]
        bo_ref[0] = b_s[...]
    else:
        yy = y0_ref[0] + y_s[...]
        mean = segsum(yy) * (1.0 / hd)
        yc = yy - mean
        var = segsum(yc * yc) * (1.0 / hd)
        yn = yc * lax.rsqrt(var + RWKV_GN_EPS) * vec[6:7] + vec[7:8]
        u_ref[0] = ((yn + b0_ref[0] + b_s[...]) * _silu(gate_ref[0])).astype(BF16)


def _rwkv_scan(proj, tail, w2p, a2p, v2p, vec, proj0, y0, b0, *, reverse, vres, last, nxb):
    b, t, e4 = proj.shape
    e = e4 // 4
    tb = TOKEN_BLOCK
    lg = RWKV_LANES
    ng = e // lg
    nblk = t // tb
    assert nblk == nxb + 1, "the context prefix must be exactly one token block"
    if reverse:
        blk = lambda i: jnp.where(i == 0, nxb, nxb - i)
    else:
        blk = lambda i: jnp.where(i == 0, nxb, i - 1)

    def col(off):
        return pl.BlockSpec((1, tb, lg), lambda bb, g, i: (bb, blk(i), off * ng + g))

    wspec = pl.BlockSpec((LANE, lg), lambda bb, g, i: (0, g))
    in_specs = [col(0), col(1), col(2),
                pl.BlockSpec((1, tb, TAIL_WIDTH), lambda bb, g, i: (bb, blk(i), 0)),
                wspec, wspec]
    args = [proj, proj, proj, tail, w2p, a2p]
    if vres:
        in_specs.append(wspec)
        args.append(v2p)
    in_specs.append(pl.BlockSpec((8, lg), lambda bb, g, i: (0, g)))
    args.append(vec)
    if vres:
        in_specs.append(col(2))
        args.append(proj0)
    act = pl.BlockSpec((1, tb, lg), lambda bb, g, i: (bb, blk(i), g))
    if last:
        in_specs += [col(3), act, act]
        args += [proj, y0, b0]
        out_specs = act
        out_shape = jax.ShapeDtypeStruct((b, t, e), BF16)
    else:
        out_specs = [act, act]
        out_shape = [jax.ShapeDtypeStruct((b, t, e), F32)] * 2
    scratch = [pltpu.VMEM((lg // LANE, LANE, LANE), F32)] + [pltpu.VMEM((tb, lg), F32)] * 2
    return pl.pallas_call(
        functools.partial(_rwkv_scan_kernel, reverse=reverse, vres=vres, last=last),
        grid=(b, ng, nblk),
        in_specs=in_specs, out_specs=out_specs, out_shape=out_shape,
        scratch_shapes=scratch,
        compiler_params=_cparams("parallel", "parallel", "arbitrary"),
        name="rwkv_scan_bwd" if reverse else "rwkv_scan_fwd",
    )(*args)


def _rwkv_layer(x, mod, g_pre, g_post, w_out, p, proj0, *, nxb, nblk_out):
    b, t, d = x.shape
    e = p['k_k'].shape[0]
    vres = 'v0' in p
    h, dl = _pre_norm(x, g_pre, mod, shift=True, nxb=nxb)
    h2, d2 = h.reshape(b * t, d), dl.reshape(b * t, d)
    w_in = p['w_in']
    proj = _project(h2, _bf(w_in[:, :4 * e]), d2=d2, mu=p['mu'][:4, None, :],
                    group_width=e).reshape(b, t, 4 * e)
    n_lo = w_in.shape[1] - 4 * e
    groups = [4] * 128 + [5] * 128 + [2] * (n_lo - 256)
    pad = TAIL_WIDTH - n_lo
    w_tail = jnp.pad(w_in[:, 4 * e:], ((0, 0), (0, pad)))
    mu_cols = jnp.pad(p['mu'][np.asarray(groups)].T, ((0, 0), (0, pad)))
    tail = _project_tail(h2, d2, w_tail, mu_cols).reshape(b, t, TAIL_WIDTH)

    def lora(w, row0):
        return _bf(jnp.pad(w, ((row0, LANE - row0 - w.shape[0]), (0, 0))))

    zero = jnp.zeros((e,), F32)
    y0 = b0 = None
    for z in range(2):
        vec = jnp.stack([p['w0'][z], p['a0'][z], p['k_k'], p['k_a'], p['r_k'].reshape(e),
                         p['v0'] if vres else zero, p['ln_w'], p['ln_b']])
        out = _rwkv_scan(proj, tail, lora(p['w2'][z], 64 * z), lora(p['a2'][z], 64 * z),
                         lora(p['v2'], 0) if vres else None, vec, proj0, y0, b0,
                         reverse=(z == 1), vres=vres, last=(z == 1), nxb=nxb)
        if z == 0:
            y0, b0 = out
    x_new = _out_residual(out, _bf(w_out), x, g_post, mod, nxb=nxb, nblk_out=nblk_out)
    return x_new, proj


def _hgrn_scan_kernel(*refs, reverse, last, layer):
    it = iter(refs)
    q_ref, f_ref, i_ref, lb_ref = (next(it) for _ in range(4))
    if last:
        gate_ref, o0_ref, gn_ref, u_ref = (next(it) for _ in range(4))
    else:
        oo_ref = next(it)
    s_ref, gc_s, q_s, k_s, v_s, qg_s, kd_s, pt_s, o_s = it

    i = pl.program_id(2)
    tb, lg = q_s.shape
    nh = lg // HGRN_HEAD
    c = HGRN_SUB
    nsub = tb // c

    @pl.when(i == 0)
    def _():
        s_ref[...] = jnp.zeros_like(s_ref)

    logits = lb_ref[...]
    ex = jnp.exp(logits - jnp.max(logits, axis=0, keepdims=True))
    lb = jnp.sum(ex[1:layer + 1], axis=0, keepdims=True) / jnp.sum(ex, axis=0, keepdims=True)
    f = lb + (1.0 - lb) * _sigmoid(f_ref[0])
    g = jnp.log(f)
    r1 = lax.broadcasted_iota(jnp.int32, (tb, tb), 0)
    c1 = lax.broadcasted_iota(jnp.int32, (tb, tb), 1)
    same = (r1 // c) == (c1 // c)
    before = (c1 >= r1) if reverse else (c1 <= r1)
    gc = _mask_dot(jnp.where(same & before, 1.0, 0.0).astype(BF16), g)
    gtot = _mask_dot(jnp.where(same, 1.0, 0.0).astype(BF16), g)
    q = _silu(q_ref[0])
    kk = 1.0 - f
    gc_s[...] = gc
    q_s[...] = q
    k_s[...] = kk
    v_s[...] = i_ref[0]
    qg_s[...] = q * jnp.exp(gc)
    kd_s[...] = kk * jnp.exp(gtot - gc)
    pt_s[...] = jnp.exp(gtot)

    trow = lax.broadcasted_iota(jnp.int32, (c, 1), 0)

    def sub(jj, carry):
        ji = (nsub - 1 - jj) if reverse else jj
        rows = pl.ds(pl.multiple_of(ji * c, c), c)
        for hd in range(nh):
            sl = slice(hd * HGRN_HEAD, (hd + 1) * HGRN_HEAD)
            st = s_ref[hd]
            o = _dot_nt(_bf(qg_s[rows, sl]), _bf(st))
            gcj = gc_s[rows, sl]
            qj = q_s[rows, sl]
            kj = k_s[rows, sl]
            vj = v_s[rows, sl]
            terms = [[], []]
            for s in range(c):
                for half in range(c // 8):
                    t_lo, t_hi = 8 * half, 8 * half + 7
                    if (t_lo > s) if reverse else (t_hi < s):
                        continue
                    hs = slice(t_lo, t_lo + 8)
                    dlt = gcj[hs] - gcj[s:s + 1]
                    if not ((t_hi <= s) if reverse else (t_lo >= s)):
                        th = trow[hs]
                        dlt = jnp.where((th <= s) if reverse else (th >= s), dlt, -1e30)
                    w = qj[hs] * jnp.exp(dlt) * kj[s:s + 1]
                    terms[half].append(jnp.sum(w, axis=-1, keepdims=True) * vj[s:s + 1])
            o_s[rows, sl] = o + jnp.concatenate([_tree_sum(ts) for ts in terms], axis=0)
            upd = _dot_tn(_bf(vj), _bf(kd_s[rows, sl]))
            s_ref[hd] = st * pt_s[rows, sl][0:1] + upd
        return carry

    lax.fori_loop(0, nsub, sub, 0)

    if not last:
        oo_ref[0] = o_s[...]
    else:
        o = o0_ref[0] + o_s[...]
        gate = gate_ref[0]
        gn = gn_ref[...]
        for hd in range(nh):
            sl = slice(hd * HGRN_HEAD, (hd + 1) * HGRN_HEAD)
            oh = o[:, sl]
            ms = jnp.mean(oh * oh, axis=-1, keepdims=True)
            u_ref[0, :, sl] = (oh * lax.rsqrt(ms + NORM_EPS) * gn[:, sl] * _silu(gate[:, sl])).astype(BF16)


def _hgrn_scan(proj, lb_logits, gn, o0, *, reverse, last, nxb, layer):
    b, t, e5 = proj.shape
    e = e5 // 5
    tb = TOKEN_BLOCK
    lg = HGRN_LANES
    ng = e // lg
    nblk = t // tb
    assert nblk == nxb + 1, "the context prefix must be exactly one token block"
    if reverse:
        blk = lambda i: jnp.where(i == 0, nxb, nxb - i)
    else:
        blk = lambda i: jnp.where(i == 0, nxb, i - 1)

    def col(off):
        return pl.BlockSpec((1, tb, lg), lambda bb, g, i: (bb, blk(i), off * ng + g))

    row = pl.BlockSpec((1, lg), lambda bb, g, i: (0, g))
    act = pl.BlockSpec((1, tb, lg), lambda bb, g, i: (bb, blk(i), g))
    in_specs = [col(0), col(2 if reverse else 1), col(3),
                pl.BlockSpec((lb_logits.shape[0], lg), lambda bb, g, i: (0, g))]
    args = [proj, proj, proj, lb_logits]
    if last:
        in_specs += [col(4), act, row]
        args += [proj, o0, gn]
        out_shape = jax.ShapeDtypeStruct((b, t, e), BF16)
    else:
        out_shape = jax.ShapeDtypeStruct((b, t, e), F32)
    scratch = [pltpu.VMEM((lg // HGRN_HEAD, HGRN_HEAD, HGRN_HEAD), F32)] + [pltpu.VMEM((tb, lg), F32)] * 8
    return pl.pallas_call(
        functools.partial(_hgrn_scan_kernel, reverse=reverse, last=last, layer=layer),
        grid=(b, ng, nblk),
        in_specs=in_specs, out_specs=act, out_shape=out_shape,
        scratch_shapes=scratch,
        compiler_params=_cparams("parallel", "parallel", "arbitrary"),
        name="hgrn_scan_bwd" if reverse else "hgrn_scan_fwd",
    )(*args)


def _hgrn_layer(x, mod, g_pre, g_post, w_out, p, lb_logits, *, nxb, nblk_out, layer):
    b, t, d = x.shape
    e = lb_logits.shape[1]
    h = _pre_norm(x, g_pre, mod, shift=False, nxb=nxb)
    proj = _project(h.reshape(b * t, d), _bf(p['w_in'])).reshape(b, t, 5 * e)
    gn = jnp.tile(p['g_norm'], e // HGRN_HEAD)[None]
    o0 = _hgrn_scan(proj, lb_logits, gn, None, reverse=False, last=False, nxb=nxb, layer=layer)
    u = _hgrn_scan(proj, lb_logits, gn, o0, reverse=True, last=True, nxb=nxb, layer=layer)
    return _out_residual(u, _bf(w_out), x, g_post, mod, nxb=nxb, nblk_out=nblk_out)


def _hy_pre_kernel(*refs, nxb, nblk):
    cur = refs[0:4]
    prv = refs[4:7]
    nxt = refs[7:10]
    w = refs[10:13]
    bias = refs[13:16]
    u_ref, g0_ref = refs[16], refs[17]
    i = pl.program_id(1)
    tb = cur[0].shape[1]
    row = lax.broadcasted_iota(jnp.int32, (tb, 1), 0)
    has_prev = jnp.logical_and(i != 0, i != nxb)
    has_next = jnp.logical_and(i != nxb - 1, i != nblk - 1)

    def conv(j):
        x = cur[j][0]
        up = jnp.where(row == 0, jnp.where(has_prev, prv[j][0, 7:8, :], 0.0), pltpu.roll(x, 1, 0))
        dn = jnp.where(row == tb - 1, jnp.where(has_next, nxt[j][0, 0:1, :], 0.0),
                       pltpu.roll(x, tb - 1, 0))
        wj = w[j][...]
        return wj[0:1] * up + wj[1:2] * x + wj[2:3] * dn + bias[j][...]

    u_ref[0] = conv(2) * conv(1)
    g0_ref[0] = conv(0) * _silu(cur[3][0])


def _hy_pre(proj, conv_w, conv_b, *, nxb):
    b, t, e4 = proj.shape
    e = e4 // 4
    tb = TOKEN_BLOCK
    lg = 512
    ng = e // lg
    nblk = t // tb
    hb = tb // 8
    n8 = t // 8

    def col(off):
        return pl.BlockSpec((1, tb, lg), lambda bb, i, g: (bb, i, off * ng + g))

    def halo_prev(off):
        return pl.BlockSpec((1, 8, lg), lambda bb, i, g: (bb, jnp.maximum(i * hb - 1, 0), off * ng + g))

    def halo_next(off):
        return pl.BlockSpec((1, 8, lg), lambda bb, i, g: (bb, jnp.minimum((i + 1) * hb, n8 - 1), off * ng + g))

    def wcol(rows, off):
        return pl.BlockSpec((rows, lg), lambda bb, i, g: (0, off * ng + g))

    in_specs = ([col(o) for o in range(4)] + [halo_prev(o) for o in range(3)]
                + [halo_next(o) for o in range(3)] + [wcol(3, o) for o in range(3)]
                + [wcol(1, o) for o in range(3)])
    args = [proj] * 10 + [conv_w] * 3 + [conv_b[None]] * 3
    act = pl.BlockSpec((1, tb, lg), lambda bb, i, g: (bb, i, g))
    return pl.pallas_call(
        functools.partial(_hy_pre_kernel, nxb=nxb, nblk=nblk),
        grid=(b, nblk, ng),
        in_specs=in_specs, out_specs=[act, act],
        out_shape=[jax.ShapeDtypeStruct((b, t, e), F32)] * 2,
        compiler_params=_cparams("parallel", "parallel", "parallel"),
        name="hyena_short_conv",
    )(*args)


def _hy_filter_kernel(fv_ref, w1_ref, b1_ref, w2_ref, b2_ref, w3_ref, b3_ref, sf_ref,
                      w4f_ref, w4b_ref, dl_ref, hf_ref, hb_ref, *, length):
    tl = hf_ref.shape[0]
    n = (pl.program_id(0) * tl + lax.broadcasted_iota(jnp.int32, (tl, 1), 0)).astype(F32)
    t = n * (1.0 / (length - 1))
    lane = lax.broadcasted_iota(jnp.int32, (tl, LANE), 1)
    nb = (HYENA_EMB - 1) // 2
    ang = (2.0 * math.pi / length) * n * fv_ref[...]
    z = jnp.where(lane == 0, t,
                  jnp.where(lane <= nb, jnp.cos(ang),
                            jnp.where(lane <= 2 * nb, -jnp.sin(ang), 0.0)))
    sf = sf_ref[...]
    hdn = jnp.sin(sf * (_dot(z, w1_ref[...], HIGHEST) + b1_ref[...]))
    hdn = jnp.sin(sf * (_dot(hdn, w2_ref[...], HIGHEST) + b2_ref[...]))
    hdn = jnp.sin(sf * (_dot(hdn, w3_ref[...], HIGHEST) + b3_ref[...]))
    window = jnp.exp(-t * dl_ref[...])
    hf_ref[...] = _dot(hdn, w4f_ref[...], HIGHEST) * window
    hb_ref[...] = _dot(hdn, w4b_ref[...], HIGHEST) * window


def _hy_filters(length, p, e):
    tl = min(length, 1024)
    lg = 512
    ng = e // lg
    nb = (HYENA_EMB - 1) // 2
    freqs = np.linspace(1e-4, nb - 1, nb, dtype=np.float32)
    fv = np.zeros((1, LANE), np.float32)
    fv[0, 1:1 + nb] = freqs
    fv[0, 1 + nb:1 + 2 * nb] = freqs
    deltas = np.abs(np.linspace(math.log(HYENA_TARGET) / HYENA_SLOW_DECAY,
                                math.log(HYENA_TARGET) / HYENA_FAST_DECAY, e, dtype=np.float32))[None]
    fw = HYENA_FILTER_WIDTH
    w1 = jnp.pad(p['f_w1'], ((0, LANE - HYENA_EMB), (0, 0)))
    full = lambda shape: pl.BlockSpec(shape, lambda r, g: (0, 0))
    in_specs = [full((1, LANE)), full((LANE, fw)), full((1, fw)), full((fw, fw)), full((1, fw)),
                full((fw, fw)), full((1, fw)), full((1, fw)),
                pl.BlockSpec((fw, lg), lambda r, g: (0, g)),
                pl.BlockSpec((fw, lg), lambda r, g: (0, ng + g)),
                pl.BlockSpec((1, lg), lambda r, g: (0, g))]
    out = pl.BlockSpec((tl, lg), lambda r, g: (r, g))
    return pl.pallas_call(
        functools.partial(_hy_filter_kernel, length=length),
        grid=(length // tl, ng),
        in_specs=in_specs, out_specs=[out, out],
        out_shape=[jax.ShapeDtypeStruct((length, e), F32)] * 2,
        compiler_params=_cparams("parallel", "parallel"),
        name="hyena_filters",
    )(jnp.asarray(fv), w1, p['f_b1'][None], p['f_w2'], p['f_b2'][None], p['f_w3'], p['f_b3'][None],
      p['sin_freq'][None], p['f_w4'], p['f_w4'], jnp.asarray(deltas))


def _cmul(x, h, half):
    xr, xi = x[:half], x[half:]
    hr, hi = h[:half], h[half:]
    return jnp.concatenate([xr * hr - xi * hi, xr * hi + xi * hr], axis=0)


def _conj(x, half):
    return jnp.concatenate([x[:half], -x[half:]], axis=0)


def _tdot(tab, x, idx=None):
    get = (lambda ref: ref[...]) if idx is None else (lambda ref: ref[idx])
    xh = x.astype(BF16)
    out = _dot(get(tab[0]), xh)
    if len(tab) == 2:
        xl = (x - xh.astype(F32)).astype(BF16)
        out = out + _dot(get(tab[0]), xl) + _dot(get(tab[1]), xh)
    return out


def _split_tables(refs, count):
    per = len(FFT_TABLE_PARTS)
    return [refs[i * per:(i + 1) * per] for i in range(count)], refs[count * per:]


def _long_conv_direct_kernel(u_ref, g0_ref, hf_ref, hb_ref, fb_ref, *rest):
    (fd_t, fi_t), (o_ref, h_s) = _split_tables(rest, 2)
    half = fd_t[0].shape[0] // 2

    @pl.when(pl.program_id(1) == 0)
    def _():
        h_s[...] = _tdot(fd_t, hf_ref[...]) + _conj(_tdot(fd_t, hb_ref[...]), half)

    u = u_ref[0]
    y = _tdot(fi_t, _cmul(_tdot(fd_t, u), h_s[...], half))
    o_ref[0] = (y + u * fb_ref[...]) * g0_ref[0]


def _long_conv_kernel(u_ref, g0_ref, hf_ref, hb_ref, fb_ref, *rest):
    (t1_t, t1t_t, f2_t, f2t_t), (o_ref, a_s, h_s) = _split_tables(rest, 4)
    n2, rows1, n1h = t1_t[0].shape
    n1 = rows1 // 2
    un = FFT_UNROLL

    def stage1(load):
        def body(j, carry):
            m2s = [j * un + q for q in range(un)]
            xs = [load(pl.ds(m2, n1h, stride=n2)) for m2 in m2s]
            outs = [_tdot(t1_t, x, m2) for m2, x in zip(m2s, xs)]
            for m2, o in zip(m2s, outs):
                a_s[pl.ds(pl.multiple_of(m2 * rows1, rows1), rows1), :] = o
            return carry
        lax.fori_loop(0, n2 // un, body, 0)

    def stage2(k1s):
        zs = [jnp.concatenate([a_s[pl.ds(k1, n2, stride=rows1), :],
                               a_s[pl.ds(n1 + k1, n2, stride=rows1), :]], axis=0) for k1 in k1s]
        return [_tdot(f2_t, z) for z in zs]

    def spec_rows(k1):
        return pl.ds(pl.multiple_of(k1 * 2 * n2, 2 * n2), 2 * n2)

    @pl.when(pl.program_id(1) == 0)
    def _():
        stage1(lambda idx: hf_ref[idx, :])

        def spec_f(j, carry):
            k1s = [j * un + q for q in range(un)]
            for k1, x in zip(k1s, stage2(k1s)):
                h_s[spec_rows(k1), :] = x
            return carry
        lax.fori_loop(0, n1 // un, spec_f, 0)
        stage1(lambda idx: hb_ref[idx, :])

        def spec_b(j, carry):
            k1s = [j * un + q for q in range(un)]
            for k1, x in zip(k1s, stage2(k1s)):
                h_s[spec_rows(k1), :] = h_s[spec_rows(k1), :] + _conj(x, n2)
            return carry
        lax.fori_loop(0, n1 // un, spec_b, 0)

    stage1(lambda idx: u_ref[0, idx, :])

    def mid(j, carry):
        k1s = [j * un + q for q in range(un)]
        ys = [_cmul(x, h_s[spec_rows(k1), :], n2) for k1, x in zip(k1s, stage2(k1s))]
        zs = [_tdot(f2t_t, y) for y in ys]
        for k1, z in zip(k1s, zs):
            a_s[pl.ds(k1, n2, stride=rows1), :] = z[:n2]
            a_s[pl.ds(n1 + k1, n2, stride=rows1), :] = z[n2:]
        return carry
    lax.fori_loop(0, n1 // un, mid, 0)

    fb = fb_ref[...]

    def inv1(j, carry):
        m2s = [j * un + q for q in range(un)]
        blks = [a_s[pl.ds(pl.multiple_of(m2 * rows1, rows1), rows1), :] for m2 in m2s]
        ys = [_tdot(t1t_t, blk, m2) for m2, blk in zip(m2s, blks)]
        for m2, y in zip(m2s, ys):
            idx = pl.ds(m2, n1h, stride=n2)
            o_ref[0, idx, :] = (y + u_ref[0, idx, :] * fb) * g0_ref[0, idx, :]
        return carry
    lax.fori_loop(0, n2 // un, inv1, 0)


def _fft_tables(length):
    n = 2 * length
    n2 = FFT_N2
    n1 = n // n2
    n1h = n1 // 2
    k1 = np.arange(n1, dtype=np.float64)[:, None]
    m1 = np.arange(n1h, dtype=np.float64)[None, :]
    t1 = []
    for m2 in range(n2):
        phi = 2.0 * np.pi * (k1 * m1 / n1 + k1 * m2 / n)
        t1.append(np.concatenate([np.cos(phi), -np.sin(phi)], axis=0))
    t1 = np.stack(t1)
    t1t = np.transpose(t1, (0, 2, 1)) / n
    k2 = np.arange(n2, dtype=np.float64)[:, None]
    m2 = np.arange(n2, dtype=np.float64)[None, :]
    th = 2.0 * np.pi * k2 * m2 / n2
    mr, mi = np.cos(th), -np.sin(th)
    f2 = np.block([[mr, -mi], [mi, mr]])
    return _hi_lo(t1) + _hi_lo(t1t) + _hi_lo(f2) + _hi_lo(f2.T)


def _hi_lo(a):
    a32 = jnp.asarray(a.astype(np.float32))
    hi = a32.astype(BF16)
    parts = {"hi": hi, "lo": (a32 - hi.astype(F32)).astype(BF16)}
    return [parts[name] for name in FFT_TABLE_PARTS]


def _dft_tables(length):
    n = 2 * length
    k = np.arange(n, dtype=np.float64)[:, None]
    m = np.arange(length, dtype=np.float64)[None, :]
    phi = 2.0 * np.pi * k * m / n
    fd = np.concatenate([np.cos(phi), -np.sin(phi)], axis=0)
    return _hi_lo(fd) + _hi_lo(fd.T / n)


def _long_conv(u, g0, hf, hb, fbias, *, length, blk_index):
    b, t, e = u.shape
    lg = FFT_LANES
    ng = e // lg
    seq_spec = pl.BlockSpec((1, length, lg), lambda g, bb: (bb, blk_index, g))
    filt_spec = pl.BlockSpec((length, lg), lambda g, bb: (0, g))
    row_spec = pl.BlockSpec((1, lg), lambda g, bb: (0, g))
    out_spec = pl.BlockSpec((1, length, lg), lambda g, bb: (bb, 0, g))
    common = dict(
        grid=(ng, b),
        out_specs=out_spec,
        out_shape=jax.ShapeDtypeStruct((b, length, e), F32),
        compiler_params=_cparams("parallel", "arbitrary"),
    )
    const = lambda a: pl.BlockSpec(a.shape, lambda g, bb: (0,) * a.ndim)
    if length <= TOKEN_BLOCK:
        tabs = _dft_tables(length)
        return pl.pallas_call(
            _long_conv_direct_kernel,
            in_specs=[seq_spec, seq_spec, filt_spec, filt_spec, row_spec] + [const(a) for a in tabs],
            scratch_shapes=[pltpu.VMEM((4 * length, lg), F32)],
            name="hyena_long_conv_ctx", **common,
        )(u, g0, hf, hb, fbias, *tabs)
    tabs = _fft_tables(length)
    return pl.pallas_call(
        _long_conv_kernel,
        in_specs=[seq_spec, seq_spec, filt_spec, filt_spec, row_spec] + [const(a) for a in tabs],
        scratch_shapes=[pltpu.VMEM((4 * length, lg), F32), pltpu.VMEM((4 * length, lg), F32)],
        name="hyena_long_conv", **common,
    )(u, g0, hf, hb, fbias, *tabs)


def _hyena_layer(x, mod, g_pre, g_post, w_out, p, *, nxb, nblk_out):
    b, t, d = x.shape
    e = p['filter_bias'].shape[0]
    seq = nxb * TOKEN_BLOCK
    ctx_len = t - seq
    h = _pre_norm(x, g_pre, mod, shift=False, nxb=nxb)
    proj = _project(h.reshape(b * t, d), _bf(p['w_in'])).reshape(b, t, 4 * e)
    u, g0 = _hy_pre(proj, p['conv_w'], p['conv_b'], nxb=nxb)
    fbias = p['filter_bias'][None]
    hf, hb = _hy_filters(seq, p, e)
    yx = _long_conv(u, g0, hf, hb, fbias, length=seq, blk_index=0)
    hf, hb = _hy_filters(ctx_len, p, e)
    yc = _long_conv(u, g0, hf, hb, fbias, length=ctx_len, blk_index=seq // ctx_len)
    y = jnp.concatenate([yx, yc], axis=1)
    return _out_residual(y, _bf(w_out), x, g_post, mod, nxb=nxb, nblk_out=nblk_out)


def _modulation(c, c_ctx, ada_w, ada_b):
    b, d = c.shape
    depth = ada_w.shape[0]
    rows = -(-(b + 1) // 8) * 8
    cstack = jnp.zeros((rows, d), F32).at[:b].set(c).at[b].set(c_ctx)
    ada = _ada_all(cstack, ada_w, ada_b).reshape(depth, rows, 3, d)
    lat = ada[:, :b]
    cx = jnp.broadcast_to(ada[:, b:b + 1], lat.shape)
    return jnp.stack([lat, cx], axis=2)


def kernel(x, c, ctx, c_ctx, ada_w, ada_b, norm_pre, norm_post, w_out,
           l0_w_in, l0_mu, l0_w0, l0_w2, l0_a0, l0_a2, l0_k_k, l0_k_a, l0_r_k, l0_ln_w, l0_ln_b,
           l1_w_in, l1_conv_w, l1_conv_b, l1_f_w1, l1_f_b1, l1_f_w2, l1_f_b2, l1_f_w3, l1_f_b3,
           l1_f_w4, l1_sin_freq, l1_filter_bias,
           l2_w_in, l2_g_norm, hgrn_lb_logits,
           l3_w_in, l3_mu, l3_w0, l3_w2, l3_a0, l3_a2, l3_k_k, l3_k_a, l3_r_k, l3_ln_w, l3_ln_b,
           l3_v0, l3_v2):
    rwkv0 = dict(w_in=l0_w_in, mu=l0_mu, w0=l0_w0, w2=l0_w2, a0=l0_a0, a2=l0_a2, k_k=l0_k_k,
                 k_a=l0_k_a, r_k=l0_r_k, ln_w=l0_ln_w, ln_b=l0_ln_b)
    hyena1 = dict(w_in=l1_w_in, conv_w=l1_conv_w, conv_b=l1_conv_b, f_w1=l1_f_w1, f_b1=l1_f_b1,
                  f_w2=l1_f_w2, f_b2=l1_f_b2, f_w3=l1_f_w3, f_b3=l1_f_b3, f_w4=l1_f_w4,
                  sin_freq=l1_sin_freq, filter_bias=l1_filter_bias)
    hgrn2 = dict(w_in=l2_w_in, g_norm=l2_g_norm)
    rwkv3 = dict(w_in=l3_w_in, mu=l3_mu, w0=l3_w0, w2=l3_w2, a0=l3_a0, a2=l3_a2, k_k=l3_k_k,
                 k_a=l3_k_a, r_k=l3_r_k, ln_w=l3_ln_w, ln_b=l3_ln_b, v0=l3_v0, v2=l3_v2)
    b, seq, d = x.shape
    assert ctx.shape[1] == TOKEN_BLOCK and seq % TOKEN_BLOCK == 0
    nxb = seq // TOKEN_BLOCK
    mods = _modulation(c, c_ctx, ada_w, ada_b)
    xa = jnp.concatenate([x, ctx], axis=1)
    xa, proj0 = _rwkv_layer(xa, mods[0], norm_pre[0][None], norm_post[0][None], w_out[0], rwkv0,
                            None, nxb=nxb, nblk_out=nxb + 1)
    xa = _hyena_layer(xa, mods[1], norm_pre[1][None], norm_post[1][None], w_out[1], hyena1,
                      nxb=nxb, nblk_out=nxb + 1)
    xa = _hgrn_layer(xa, mods[2], norm_pre[2][None], norm_post[2][None], w_out[2], hgrn2,
                     hgrn_lb_logits, nxb=nxb, nblk_out=nxb + 1, layer=2)
    xa, _ = _rwkv_layer(xa, mods[3], norm_pre[3][None], norm_post[3][None], w_out[3], rwkv3,
                        proj0, nxb=nxb, nblk_out=nxb)
    return xa
```

```python
import functools
import math

import jax
import jax.numpy as jnp
import numpy as np
from jax import lax
from jax.experimental import pallas as pl
from jax.experimental.pallas import tpu as pltpu

F32 = jnp.float32
BF16 = jnp.bfloat16
HIGHEST = lax.Precision.HIGHEST

NORM_EPS = 1e-6
GRID_W = 64
TOKEN_BLOCK = 256
LANE = 128
VMEM_LIMIT = 56 * 1024 * 1024

RWKV_HEAD = 64
RWKV_CHUNK = 64
RWKV_LANES = 512
RWKV_GN_EPS = 64e-5
TAIL_WIDTH = 384

HGRN_HEAD = 128
HGRN_SUB = 16
HGRN_LANES = 512
HGRN_UNROLL = 4

HYENA_EMB = 33
HYENA_FILTER_WIDTH = 64
HYENA_FAST_DECAY = 0.3
HYENA_SLOW_DECAY = 1.5
HYENA_TARGET = 1e-2
FFT_N2 = 64
FFT_LANES = 128
FFT_TABLE_PARTS = ("hi",)
FFT_UNROLL = 8


def _cparams(*sem):
    return pltpu.CompilerParams(dimension_semantics=sem, vmem_limit_bytes=VMEM_LIMIT)


def _dot(a, b, precision=None):
    return jnp.dot(a, b, preferred_element_type=F32, precision=precision)


def _dot_nt(a, b, precision=None):
    return lax.dot_general(a, b, (((1,), (1,)), ((), ())),
                           preferred_element_type=F32, precision=precision)


def _dot_tn(a, b):
    return lax.dot_general(a, b, (((0,), (0,)), ((), ())), preferred_element_type=F32)


def _bf(x):
    return x.astype(BF16)


def _split_dot(x, w_bf16):
    hi = x.astype(BF16)
    lo = (x - hi.astype(F32)).astype(BF16)
    return _dot(hi, w_bf16) + _dot(lo, w_bf16)


def _mask_dot(m_bf16, x):
    hi = x.astype(BF16)
    lo = (x - hi.astype(F32)).astype(BF16)
    return _dot(m_bf16, hi) + _dot(m_bf16, lo)


def _tree_sum(terms):
    terms = list(terms)
    while len(terms) > 1:
        terms = [a + b for a, b in zip(terms[0::2], terms[1::2])] + (terms[-1:] if len(terms) % 2 else [])
    return terms[0]


def _sigmoid(x):
    return 1.0 / (1.0 + jnp.exp(-x))


def _silu(x):
    return x * _sigmoid(x)


def _ada_kernel(c_ref, w_ref, b_ref, o_ref):
    o_ref[0] = _dot(_silu(c_ref[...]), w_ref[0], HIGHEST) + b_ref[0]


def _ada_all(cstack, ada_w, ada_b):
    depth, d, d3 = ada_w.shape
    rows = cstack.shape[0]
    nt = d3 // d
    return pl.pallas_call(
        _ada_kernel,
        grid=(depth, nt),
        in_specs=[
            pl.BlockSpec((rows, d), lambda l, j: (0, 0)),
            pl.BlockSpec((1, d, d), lambda l, j: (l, 0, j)),
            pl.BlockSpec((1, 1, d), lambda l, j: (l, 0, j)),
        ],
        out_specs=pl.BlockSpec((1, rows, d), lambda l, j: (l, 0, j)),
        out_shape=jax.ShapeDtypeStruct((depth, rows, d3), F32),
        compiler_params=_cparams("parallel", "parallel"),
        name="adaln",
    )(cstack, ada_w, ada_b.reshape(depth, 1, d3))


def _norm_kernel(*refs, shift, nxb):
    if shift:
        x_ref, xp_ref, xn_ref, g_ref, mod_ref, h_ref, d_ref = refs
    else:
        x_ref, g_ref, mod_ref, h_ref = refs
    i = pl.program_id(1)
    g = g_ref[...]
    shift_v = mod_ref[0, 0, 0:1, :]
    scale1p = 1.0 + mod_ref[0, 0, 1:2, :]

    def nrm(x):
        ms = jnp.mean(x * x, axis=-1, keepdims=True)
        return x * lax.rsqrt(ms + NORM_EPS) * g * scale1p + shift_v

    h = nrm(x_ref[0])
    h_ref[0] = h.astype(BF16)
    if not shift:
        return
    tb, d = h.shape
    q = d // 4
    row = lax.broadcasted_iota(jnp.int32, (tb, 1), 0)

    @pl.when(i < nxb)
    def _():
        col = row % GRID_W
        left = jnp.where(col > 0, pltpu.roll(h[:, 0:q], 1, 0), 0.0)
        right = jnp.where(col < GRID_W - 1, pltpu.roll(h[:, q:2 * q], tb - 1, 0), 0.0)
        hp = nrm(xp_ref[0])[:, 2 * q:3 * q]
        hn = nrm(xn_ref[0])[:, 3 * q:]
        hp = jnp.where(i > 0, hp, 0.0)
        hn = jnp.where(i < nxb - 1, hn, 0.0)
        up = jnp.concatenate([hp, h[:tb - GRID_W, 2 * q:3 * q]], axis=0)
        down = jnp.concatenate([h[GRID_W:, 3 * q:], hn], axis=0)
        hs = jnp.concatenate([left, right, up, down], axis=-1)
        d_ref[0] = (hs - h).astype(BF16)

    @pl.when(i >= nxb)
    def _():
        half = d // 2
        prev = jnp.where(row > 0, pltpu.roll(h[:, :half], 1, 0), 0.0)
        nxt = jnp.where(row < tb - 1, pltpu.roll(h[:, half:], tb - 1, 0), 0.0)
        hs = jnp.concatenate([prev, nxt], axis=-1)
        d_ref[0] = (hs - h).astype(BF16)


def _pre_norm(x, g, mod, *, shift, nxb):
    b, t, d = x.shape
    tb = TOKEN_BLOCK
    nblk = t // tb
    hb = tb // GRID_W
    nhalo = t // GRID_W
    seg = lambda i: jnp.where(i < nxb, 0, 1)
    x_spec = pl.BlockSpec((1, tb, d), lambda bb, i: (bb, i, 0))
    g_spec = pl.BlockSpec((1, d), lambda bb, i: (0, 0))
    mod_spec = pl.BlockSpec((1, 1, 3, d), lambda bb, i: (bb, seg(i), 0, 0))
    out_spec = pl.BlockSpec((1, tb, d), lambda bb, i: (bb, i, 0))
    if shift:
        in_specs = [
            x_spec,
            pl.BlockSpec((1, GRID_W, d), lambda bb, i: (bb, jnp.maximum(i * hb - 1, 0), 0)),
            pl.BlockSpec((1, GRID_W, d), lambda bb, i: (bb, jnp.minimum((i + 1) * hb, nhalo - 1), 0)),
            g_spec, mod_spec,
        ]
        args = (x, x, x, g, mod)
        out_specs = [out_spec, out_spec]
        out_shape = [jax.ShapeDtypeStruct((b, t, d), BF16)] * 2
    else:
        in_specs = [x_spec, g_spec, mod_spec]
        args = (x, g, mod)
        out_specs = out_spec
        out_shape = jax.ShapeDtypeStruct((b, t, d), BF16)
    return pl.pallas_call(
        functools.partial(_norm_kernel, shift=shift, nxb=nxb),
        grid=(b, nblk),
        in_specs=in_specs,
        out_specs=out_specs,
        out_shape=out_shape,
        compiler_params=_cparams("parallel", "parallel"),
        name="pre_norm_shift" if shift else "pre_norm",
    )(*args)


def _row_tile(m):
    for tm in (1024, 512, 256):
        if m % tm == 0:
            return tm
    raise ValueError(f"token count {m} is not a multiple of {TOKEN_BLOCK}")


def _proj_lerp_kernel(h_ref, d_ref, mu_ref, w_ref, o_ref, lhs_ref, *, tiles_per_group):
    j = pl.program_id(1)

    @pl.when(j % tiles_per_group == 0)
    def _():
        lhs_ref[...] = (h_ref[...].astype(F32) + mu_ref[0] * d_ref[...].astype(F32)).astype(BF16)

    o_ref[...] = _dot(lhs_ref[...], w_ref[...]).astype(o_ref.dtype)


def _proj_kernel(h_ref, w_ref, o_ref):
    o_ref[...] = _dot(h_ref[...], w_ref[...]).astype(o_ref.dtype)


def _project(h2, w_bf16, *, out_dtype, d2=None, mu=None, group_width=None):
    m, d = h2.shape
    n = w_bf16.shape[1]
    tm = _row_tile(m)
    tn = 1024
    lhs_spec = pl.BlockSpec((tm, d), lambda i, j: (i, 0))
    w_spec = pl.BlockSpec((d, tn), lambda i, j: (0, j))
    o_spec = pl.BlockSpec((tm, tn), lambda i, j: (i, j))
    if d2 is None:
        return pl.pallas_call(
            _proj_kernel, grid=(m // tm, n // tn),
            in_specs=[lhs_spec, w_spec], out_specs=o_spec,
            out_shape=jax.ShapeDtypeStruct((m, n), out_dtype),
            compiler_params=_cparams("parallel", "parallel"),
            name="project",
        )(h2, w_bf16)
    tpg = group_width // tn
    return pl.pallas_call(
        functools.partial(_proj_lerp_kernel, tiles_per_group=tpg),
        grid=(m // tm, n // tn),
        in_specs=[lhs_spec, lhs_spec,
                  pl.BlockSpec((1, 1, d), lambda i, j: (j // tpg, 0, 0)),
                  w_spec],
        out_specs=o_spec,
        out_shape=jax.ShapeDtypeStruct((m, n), out_dtype),
        scratch_shapes=[pltpu.VMEM((tm, d), BF16)],
        compiler_params=_cparams("parallel", "arbitrary"),
        name="project_lerp",
    )(h2, d2, mu, w_bf16)


def _tail_kernel(h_ref, d_ref, w_ref, mu_ref, o_ref):
    w = w_ref[...]
    o_ref[...] = _dot(h_ref[...], _bf(w)) + _dot(d_ref[...], _bf(w * mu_ref[...]))


def _project_tail(h2, d2, w_tail, mu_cols):
    m, d = h2.shape
    n = w_tail.shape[1]
    tm = _row_tile(m)
    lhs_spec = pl.BlockSpec((tm, d), lambda i: (i, 0))
    w_spec = pl.BlockSpec((d, n), lambda i: (0, 0))
    return pl.pallas_call(
        _tail_kernel, grid=(m // tm,),
        in_specs=[lhs_spec, lhs_spec, w_spec, w_spec],
        out_specs=pl.BlockSpec((tm, n), lambda i: (i, 0)),
        out_shape=jax.ShapeDtypeStruct((m, n), F32),
        compiler_params=_cparams("parallel"),
        name="project_tail",
    )(h2, d2, w_tail, mu_cols)


def _out_kernel(u_ref, w_ref, x_ref, g_ref, mod_ref, o_ref):
    y = _dot(_bf(u_ref[0]), w_ref[...])
    ms = jnp.mean(y * y, axis=-1, keepdims=True)
    yn = y * lax.rsqrt(ms + NORM_EPS) * g_ref[...]
    o_ref[0] = x_ref[0] + yn * mod_ref[0, 0, 2:3, :]


def _out_residual(u, w_bf16, x, g, mod, *, nxb, nblk_out):
    b, t, e = u.shape
    d = x.shape[-1]
    tb = TOKEN_BLOCK
    seg = lambda i: jnp.where(i < nxb, 0, 1)
    return pl.pallas_call(
        _out_kernel, grid=(b, nblk_out),
        in_specs=[
            pl.BlockSpec((1, tb, e), lambda bb, i: (bb, i, 0)),
            pl.BlockSpec((e, d), lambda bb, i: (0, 0)),
            pl.BlockSpec((1, tb, d), lambda bb, i: (bb, i, 0)),
            pl.BlockSpec((1, d), lambda bb, i: (0, 0)),
            pl.BlockSpec((1, 1, 3, d), lambda bb, i: (bb, seg(i), 0, 0)),
        ],
        out_specs=pl.BlockSpec((1, tb, d), lambda bb, i: (bb, i, 0)),
        out_shape=jax.ShapeDtypeStruct((b, nblk_out * tb, d), F32),
        compiler_params=_cparams("parallel", "parallel"),
        name="out_residual",
    )(u, w_bf16, x, g, mod)


def _rwkv_scan_kernel(*refs, reverse, vres, last):
    it = iter(refs)
    r_ref, k_ref, v_ref, lo_ref, w2_ref, a2_ref = (next(it) for _ in range(6))
    v2_ref = next(it) if vres else None
    vec_ref = next(it)
    vf_ref = next(it) if vres else None
    if last:
        gate_ref, y0_ref, b0_ref, u_ref = (next(it) for _ in range(4))
    else:
        yo_ref, bo_ref = next(it), next(it)
    s_ref, y_s, b_s = it

    i = pl.program_id(2)
    tb, lg = y_s.shape
    npair = lg // LANE
    c = RWKV_CHUNK
    nch = tb // c
    hd = RWKV_HEAD

    @pl.when(i == 0)
    def _():
        s_ref[...] = jnp.zeros_like(s_ref)

    vec = vec_ref[...]
    lo = lo_ref[0]
    r = r_ref[0].astype(F32)
    k = k_ref[0].astype(F32)
    v = v_ref[0].astype(F32)
    zt = vec[0:1] + _dot(_bf(jnp.tanh(lo[:, :LANE])), w2_ref[...])
    logw = -math.exp(-0.5) * _sigmoid(zt)
    a = _sigmoid(vec[1:2] + _dot(_bf(lo[:, LANE:2 * LANE]), a2_ref[...]))
    if vres:
        v = v + (vf_ref[0].astype(F32) - v) * _sigmoid(vec[5:6] + _dot(_bf(lo[:, 2 * LANE:]), v2_ref[...]))
    kdir = k * (1.0 + (a - 1.0) * vec[3:4])
    kkr = k * vec[2:3]
    bd = (lax.broadcasted_iota(jnp.int32, (LANE, LANE), 0) // hd
          == lax.broadcasted_iota(jnp.int32, (LANE, LANE), 1) // hd)
    ones_bd = jnp.where(bd, 1.0, 0.0).astype(BF16)

    def segsum(x):
        return jnp.concatenate(
            [_dot(_bf(x[:, p * LANE:(p + 1) * LANE]), ones_bd) for p in range(npair)], axis=-1)

    kk = kkr / jnp.maximum(jnp.sqrt(segsum(kkr * kkr)), 1e-12)
    b_s[...] = segsum(r * kdir * vec[4:5]) * v
    alpha = -kk
    beta = kk * a

    t1 = lax.broadcasted_iota(jnp.int32, (c, c), 0)
    s1 = lax.broadcasted_iota(jnp.int32, (c, c), 1)
    t2 = lax.broadcasted_iota(jnp.int32, (c, LANE), 0)
    lane2 = lax.broadcasted_iota(jnp.int32, (c, LANE), 1)
    s2 = lane2 % c
    if reverse:
        incl1, incl2, strict2 = s1 >= t1, s2 >= t2, s2 > t2
    else:
        incl1, incl2, strict2 = s1 <= t1, s2 <= t2, s2 < t2
    tri = jnp.where(incl1, 1.0, 0.0).astype(BF16)
    low = lane2 < hd
    high = jnp.logical_not(low)
    low_x = lax.broadcasted_iota(jnp.int32, (2 * c, LANE), 1) < hd
    strict_lo, strict_hi = strict2 & low, strict2 & high
    incl_lo, incl_hi = incl2 & low, incl2 & high
    eye2 = jnp.where(lax.broadcasted_iota(jnp.int32, (LANE, LANE), 0)
                     == lax.broadcasted_iota(jnp.int32, (LANE, LANE), 1), 1.0, 0.0)
    zeros_cv = jnp.zeros((c, LANE), F32)

    def stack(top, bot):
        return jnp.concatenate([top, bot], axis=0)

    def fold(z):
        return z[:c] + z[c:]

    ch = []
    for ci in range(nch):
        rows = slice(ci * c, (ci + 1) * c)
        lw = logw[rows]
        lc = _mask_dot(tri, lw)
        ltot = lc[0:1] if reverse else lc[c - 1:c]
        p_inv = jnp.exp(-lc)
        p_all = jnp.exp(ltot)
        ab = alpha[rows] * jnp.exp(lc - lw)
        rb = r[rows] * jnp.exp(lc)
        bt = beta[rows] * p_inv
        kt = kdir[rows] * p_inv
        bh = bt * p_all
        kh = kt * p_all
        for p in range(npair):
            sl = slice(p * LANE, (p + 1) * LANE)
            ch.append(dict(ci=ci, p=p, ab=ab[:, sl], rb=rb[:, sl], bt=bt[:, sl], kt=kt[:, sl],
                           bh=bh[:, sl], kh=kh[:, sl], v=v[rows, sl], p_all=p_all[:, sl]))
    for d in ch:
        y01 = jnp.concatenate([jnp.where(low_x, stack(d['bt'], d['kt']), 0.0),
                               jnp.where(low_x, 0.0, stack(d['kt'], d['bt']))], axis=0)
        d['g'] = _dot_nt(_bf(stack(d['ab'], d['rb'])), _bf(y01))
    for d in ch:
        g0t, g0b = d['g'][:c, :LANE], d['g'][c:, :LANE]
        g1t, g1b = d['g'][:c, LANE:], d['g'][c:, LANE:]
        d['a'] = stack(jnp.where(strict_lo, g0t, 0.0), jnp.where(strict_hi, g1t, 0.0))
        arb = stack(jnp.where(incl_lo, g0b, 0.0), jnp.where(incl_hi, g1b, 0.0))
        ark = stack(jnp.where(incl_hi, g0b, 0.0), jnp.where(incl_lo, g1b, 0.0))
        d['arbk'] = _bf(jnp.concatenate([arb, ark], axis=1))
        ak = stack(jnp.where(strict_hi, g0t, 0.0), jnp.where(strict_lo, g1t, 0.0))
        d['vx'] = stack(jnp.where(high, d['v'], 0.0), jnp.where(low, d['v'], 0.0))
        d['w'] = _dot(_bf(ak), _bf(d['vx']))
        del d['g']
    rounds = int(math.log2(c)) - 1
    for d in ch:
        d['tm'] = eye2 + d['a']
        apb = _bf(d['a'])
        d['a'] = _dot(apb, apb)
    for j in range(rounds):
        for d in ch:
            if j < rounds - 1:
                pt = _dot(_bf(d['a']), _bf(jnp.concatenate([d['a'], d['tm']], axis=1)))
                d['a'] = pt[:, :LANE]
                d['tm'] = d['tm'] + pt[:, LANE:]
            else:
                d['tm'] = d['tm'] + _dot(_bf(d['a']), _bf(d['tm']))
    for d in ch:
        ab_st = stack(jnp.where(low, d['ab'], 0.0), jnp.where(high, d['ab'], 0.0))
        d['tz'] = _dot(_bf(d['tm']), _bf(jnp.concatenate([ab_st, d['w']], axis=1)))
    for d in ch:
        lower = jnp.concatenate([jnp.zeros((2 * c, LANE), F32), d['vx']], axis=1)
        yz = _dot(d['arbk'], _bf(stack(d['tz'], lower)))
        ta, tw = fold(d['tz'][:, :LANE]), fold(d['tz'][:, LANE:])
        d['ra'] = _bf(d['rb'] + fold(yz[:, :LANE]))
        d['yw'] = fold(yz[:, LANE:])
        lhs = stack(jnp.concatenate([ta, tw], axis=1), jnp.concatenate([zeros_cv, d['v']], axis=1))
        mn = _dot_tn(_bf(lhs), _bf(stack(d['bh'], d['kh'])))
        d['m'] = _bf(jnp.where(bd, mn[:LANE], 0.0))
        d['n'] = jnp.where(bd, mn[LANE:], 0.0)

    state = [s_ref[p] for p in range(npair)]
    for step in range(nch):
        ci = (nch - 1 - step) if reverse else step
        for p in range(npair):
            d = ch[ci * npair + p]
            sp = state[p]
            spb = _bf(sp)
            y_s[ci * c:(ci + 1) * c, p * LANE:(p + 1) * LANE] = _dot_nt(d['ra'], spb) + d['yw']
            state[p] = sp * d['p_all'] + _dot(spb, d['m']) + d['n']
    for p in range(npair):
        s_ref[p] = state[p]

    if not last:
        yo_ref[0] = y_s[...]
        bo_ref[0] = b_s[...]
    else:
        yy = y0_ref[0] + y_s[...]
        mean = segsum(yy) * (1.0 / hd)
        yc = yy - mean
        var = segsum(yc * yc) * (1.0 / hd)
        yn = yc * lax.rsqrt(var + RWKV_GN_EPS) * vec[6:7] + vec[7:8]
        u_ref[0] = ((yn + b0_ref[0] + b_s[...]) * _silu(gate_ref[0].astype(F32))).astype(BF16)


def _rwkv_scan(proj, tail, w2p, a2p, v2p, vec, proj0, y0, b0, *, reverse, vres, last, nxb):
    b, t, e4 = proj.shape
    e = e4 // 4
    tb = TOKEN_BLOCK
    lg = RWKV_LANES
    ng = e // lg
    nblk = t // tb
    assert nblk == nxb + 1, "the context prefix must be exactly one token block"
    if reverse:
        blk = lambda i: jnp.where(i == 0, nxb, nxb - i)
    else:
        blk = lambda i: jnp.where(i == 0, nxb, i - 1)

    def col(off):
        return pl.BlockSpec((1, tb, lg), lambda bb, g, i: (bb, blk(i), off * ng + g))

    wspec = pl.BlockSpec((LANE, lg), lambda bb, g, i: (0, g))
    in_specs = [col(0), col(1), col(2),
                pl.BlockSpec((1, tb, TAIL_WIDTH), lambda bb, g, i: (bb, blk(i), 0)),
                wspec, wspec]
    args = [proj, proj, proj, tail, w2p, a2p]
    if vres:
        in_specs.append(wspec)
        args.append(v2p)
    in_specs.append(pl.BlockSpec((8, lg), lambda bb, g, i: (0, g)))
    args.append(vec)
    if vres:
        in_specs.append(col(2))
        args.append(proj0)
    act = pl.BlockSpec((1, tb, lg), lambda bb, g, i: (bb, blk(i), g))
    if last:
        in_specs += [col(3), act, act]
        args += [proj, y0, b0]
        out_specs = act
        out_shape = jax.ShapeDtypeStruct((b, t, e), BF16)
    else:
        out_specs = [act, act]
        out_shape = [jax.ShapeDtypeStruct((b, t, e), F32)] * 2
    scratch = [pltpu.VMEM((lg // LANE, LANE, LANE), F32)] + [pltpu.VMEM((tb, lg), F32)] * 2
    return pl.pallas_call(
        functools.partial(_rwkv_scan_kernel, reverse=reverse, vres=vres, last=last),
        grid=(b, ng, nblk),
        in_specs=in_specs, out_specs=out_specs, out_shape=out_shape,
        scratch_shapes=scratch,
        compiler_params=_cparams("parallel", "parallel", "arbitrary"),
        name="rwkv_scan_bwd" if reverse else "rwkv_scan_fwd",
    )(*args)


def _rwkv_layer(x, mod, g_pre, g_post, w_out, p, proj0, *, nxb, nblk_out):
    b, t, d = x.shape
    e = p['k_k'].shape[0]
    vres = 'v0' in p
    h, dl = _pre_norm(x, g_pre, mod, shift=True, nxb=nxb)
    h2, d2 = h.reshape(b * t, d), dl.reshape(b * t, d)
    w_in = p['w_in']
    proj = _project(h2, _bf(w_in[:, :4 * e]), out_dtype=BF16, d2=d2, mu=p['mu'][:4, None, :],
                    group_width=e).reshape(b, t, 4 * e)
    n_lo = w_in.shape[1] - 4 * e
    groups = [4] * 128 + [5] * 128 + [2] * (n_lo - 256)
    pad = TAIL_WIDTH - n_lo
    w_tail = jnp.pad(w_in[:, 4 * e:], ((0, 0), (0, pad)))
    mu_cols = jnp.pad(p['mu'][np.asarray(groups)].T, ((0, 0), (0, pad)))
    tail = _project_tail(h2, d2, w_tail, mu_cols).reshape(b, t, TAIL_WIDTH)

    def lora(w, row0):
        return _bf(jnp.pad(w, ((row0, LANE - row0 - w.shape[0]), (0, 0))))

    zero = jnp.zeros((e,), F32)
    y0 = b0 = None
    for z in range(2):
        vec = jnp.stack([p['w0'][z], p['a0'][z], p['k_k'], p['k_a'], p['r_k'].reshape(e),
                         p['v0'] if vres else zero, p['ln_w'], p['ln_b']])
        out = _rwkv_scan(proj, tail, lora(p['w2'][z], 64 * z), lora(p['a2'][z], 64 * z),
                         lora(p['v2'], 0) if vres else None, vec, proj0, y0, b0,
                         reverse=(z == 1), vres=vres, last=(z == 1), nxb=nxb)
        if z == 0:
            y0, b0 = out
    x_new = _out_residual(out, _bf(w_out), x, g_post, mod, nxb=nxb, nblk_out=nblk_out)
    return x_new, proj


def _hgrn_scan_kernel(*refs, reverse, last, layer):
    it = iter(refs)
    q_ref, f_ref, i_ref, lb_ref = (next(it) for _ in range(4))
    if last:
        gate_ref, o0_ref, gn_ref, u_ref = (next(it) for _ in range(4))
    else:
        oo_ref = next(it)
    s_ref, gc_s, q_s, k_s, v_s, qg_s, kd_s, pt_s, o_s = it

    i = pl.program_id(2)
    tb, lg = q_s.shape
    nh = lg // HGRN_HEAD
    c = HGRN_SUB
    nsub = tb // c

    @pl.when(i == 0)
    def _():
        s_ref[...] = jnp.zeros_like(s_ref)

    logits = lb_ref[...]
    ex = jnp.exp(logits - jnp.max(logits, axis=0, keepdims=True))
    lb = jnp.sum(ex[1:layer + 1], axis=0, keepdims=True) / jnp.sum(ex, axis=0, keepdims=True)
    f = lb + (1.0 - lb) * _sigmoid(f_ref[0])
    g = jnp.log(f)
    r1 = lax.broadcasted_iota(jnp.int32, (tb, tb), 0)
    c1 = lax.broadcasted_iota(jnp.int32, (tb, tb), 1)
    same = (r1 // c) == (c1 // c)
    before = (c1 >= r1) if reverse else (c1 <= r1)
    gc = _mask_dot(jnp.where(same & before, 1.0, 0.0).astype(BF16), g)
    gtot = _mask_dot(jnp.where(same, 1.0, 0.0).astype(BF16), g)
    q = _silu(q_ref[0].astype(F32))
    kk = 1.0 - f
    gc_s[...] = gc
    q_s[...] = q
    k_s[...] = kk
    v_s[...] = i_ref[0].astype(F32)
    qg_s[...] = q * jnp.exp(gc)
    kd_s[...] = kk * jnp.exp(gtot - gc)
    pt_s[...] = jnp.exp(gtot)

    trow = lax.broadcasted_iota(jnp.int32, (c, 1), 0)

    sls = [slice(hd * HGRN_HEAD, (hd + 1) * HGRN_HEAD) for hd in range(nh)]

    def sub_chunk(ji, sts):
        rows = pl.ds(pl.multiple_of(ji * c, c), c)
        gcs = [gc_s[rows, sl] for sl in sls]
        qs = [q_s[rows, sl] for sl in sls]
        ks = [k_s[rows, sl] for sl in sls]
        vs = [v_s[rows, sl] for sl in sls]
        first, second = (slice(8, 16), slice(0, 8)) if reverse else (slice(0, 8), slice(8, 16))
        atts = []
        for gcj, qj, kj in zip(gcs, qs, ks):
            gb = gcj[8:9] if reverse else gcj[7:8]
            qx = qj[second] * jnp.exp(gcj[second] - gb)
            kx = kj[first] * jnp.exp(gb - gcj[first])
            z8 = jnp.zeros_like(qx)
            q16 = jnp.concatenate([qx, z8] if reverse else [z8, qx], axis=0)
            k16 = jnp.concatenate([z8, kx] if reverse else [kx, z8], axis=0)
            atts.append(_dot_nt(_bf(q16), _bf(k16)))
        o_state = [_dot_nt(_bf(qg_s[rows, sl]), _bf(st)) for sl, st in zip(sls, sts)]
        upds = [_dot_tn(_bf(vj), _bf(kd_s[rows, sl])) for sl, vj in zip(sls, vs)]
        o_cross = [_dot(_bf(att), _bf(vj)) for att, vj in zip(atts, vs)]
        new_sts = [sts[hd] * pt_s[rows, sls[hd]][0:1] + upds[hd] for hd in range(nh)]
        for hd in range(nh):
            gcj, qj, kj, vj = gcs[hd], qs[hd], ks[hd], vs[hd]
            terms = [[], []]
            for s in range(c):
                half = s // 8
                hs = slice(8 * half, 8 * half + 8)
                dlt = gcj[hs] - gcj[s:s + 1]
                if s != (8 * half + 7 if reverse else 8 * half):
                    th = trow[hs]
                    dlt = jnp.where((th <= s) if reverse else (th >= s), dlt, -1e30)
                w = qj[hs] * jnp.exp(dlt) * kj[s:s + 1]
                terms[half].append(jnp.sum(w, axis=-1, keepdims=True) * vj[s:s + 1])
            pair = jnp.concatenate([_tree_sum(ts) for ts in terms], axis=0)
            o_s[rows, sls[hd]] = (o_state[hd] + o_cross[hd]) + pair
        return new_sts

    def sub(jj, carry):
        sts = [s_ref[hd] for hd in range(nh)]
        for q in range(HGRN_UNROLL):
            step = jj * HGRN_UNROLL + q
            sts = sub_chunk((nsub - 1 - step) if reverse else step, sts)
        for hd in range(nh):
            s_ref[hd] = sts[hd]
        return carry

    lax.fori_loop(0, nsub // HGRN_UNROLL, sub, 0)

    if not last:
        oo_ref[0] = o_s[...]
    else:
        o = o0_ref[0] + o_s[...]
        gate = gate_ref[0].astype(F32)
        gn = gn_ref[...]
        for hd in range(nh):
            sl = slice(hd * HGRN_HEAD, (hd + 1) * HGRN_HEAD)
            oh = o[:, sl]
            ms = jnp.mean(oh * oh, axis=-1, keepdims=True)
            u_ref[0, :, sl] = (oh * lax.rsqrt(ms + NORM_EPS) * gn[:, sl] * _silu(gate[:, sl])).astype(BF16)


def _hgrn_scan(proj_qig, proj_f, lb_logits, gn, o0, *, reverse, last, nxb, layer):
    b, t, e3 = proj_qig.shape
    e = e3 // 3
    tb = TOKEN_BLOCK
    lg = HGRN_LANES
    ng = e // lg
    nblk = t // tb
    assert nblk == nxb + 1, "the context prefix must be exactly one token block"
    if reverse:
        blk = lambda i: jnp.where(i == 0, nxb, nxb - i)
    else:
        blk = lambda i: jnp.where(i == 0, nxb, i - 1)

    def col(off):
        return pl.BlockSpec((1, tb, lg), lambda bb, g, i: (bb, blk(i), off * ng + g))

    row = pl.BlockSpec((1, lg), lambda bb, g, i: (0, g))
    act = pl.BlockSpec((1, tb, lg), lambda bb, g, i: (bb, blk(i), g))
    in_specs = [col(0), col(1 if reverse else 0), col(1),
                pl.BlockSpec((lb_logits.shape[0], lg), lambda bb, g, i: (0, g))]
    args = [proj_qig, proj_f, proj_qig, lb_logits]
    if last:
        in_specs += [col(2), act, row]
        args += [proj_qig, o0, gn]
        out_shape = jax.ShapeDtypeStruct((b, t, e), BF16)
    else:
        out_shape = jax.ShapeDtypeStruct((b, t, e), F32)
    scratch = [pltpu.VMEM((lg // HGRN_HEAD, HGRN_HEAD, HGRN_HEAD), F32)] + [pltpu.VMEM((tb, lg), F32)] * 8
    return pl.pallas_call(
        functools.partial(_hgrn_scan_kernel, reverse=reverse, last=last, layer=layer),
        grid=(b, ng, nblk),
        in_specs=in_specs, out_specs=act, out_shape=out_shape,
        scratch_shapes=scratch,
        compiler_params=_cparams("parallel", "parallel", "arbitrary"),
        name="hgrn_scan_bwd" if reverse else "hgrn_scan_fwd",
    )(*args)


def _hgrn_layer(x, mod, g_pre, g_post, w_out, p, lb_logits, *, nxb, nblk_out, layer):
    b, t, d = x.shape
    e = lb_logits.shape[1]
    h = _pre_norm(x, g_pre, mod, shift=False, nxb=nxb)
    h2 = h.reshape(b * t, d)
    w_in = p['w_in']
    w_qig = _bf(jnp.concatenate([w_in[:, :e], w_in[:, 3 * e:]], axis=1))
    proj_f = _project(h2, _bf(w_in[:, e:3 * e]), out_dtype=F32).reshape(b, t, 2 * e)
    proj_qig = _project(h2, w_qig, out_dtype=BF16).reshape(b, t, 3 * e)
    gn = jnp.tile(p['g_norm'], e // HGRN_HEAD)[None]
    o0 = _hgrn_scan(proj_qig, proj_f, lb_logits, gn, None, reverse=False, last=False, nxb=nxb,
                    layer=layer)
    u = _hgrn_scan(proj_qig, proj_f, lb_logits, gn, o0, reverse=True, last=True, nxb=nxb,
                   layer=layer)
    return _out_residual(u, _bf(w_out), x, g_post, mod, nxb=nxb, nblk_out=nblk_out)


def _hy_pre_kernel(*refs, nxb, nblk):
    cur = refs[0:4]
    prv = refs[4:7]
    nxt = refs[7:10]
    w = refs[10:13]
    bias = refs[13:16]
    u_ref, g0_ref = refs[16], refs[17]
    i = pl.program_id(1)
    tb = cur[0].shape[1]
    row = lax.broadcasted_iota(jnp.int32, (tb, 1), 0)
    has_prev = jnp.logical_and(i != 0, i != nxb)
    has_next = jnp.logical_and(i != nxb - 1, i != nblk - 1)

    hr = prv[0].shape[1]

    def conv(j):
        x = cur[j][0].astype(F32)
        before = prv[j][0].astype(F32)[hr - 1:hr]
        after = nxt[j][0].astype(F32)[0:1]
        up = jnp.where(row == 0, jnp.where(has_prev, before, 0.0), pltpu.roll(x, 1, 0))
        dn = jnp.where(row == tb - 1, jnp.where(has_next, after, 0.0), pltpu.roll(x, tb - 1, 0))
        wj = w[j][...]
        return wj[0:1] * up + wj[1:2] * x + wj[2:3] * dn + bias[j][...]

    u_ref[0] = conv(2) * conv(1)
    g0_ref[0] = conv(0) * _silu(cur[3][0].astype(F32))


def _hy_pre(proj, conv_w, conv_b, *, nxb):
    b, t, e4 = proj.shape
    e = e4 // 4
    tb = TOKEN_BLOCK
    lg = 512
    ng = e // lg
    nblk = t // tb
    hr = 16
    hb = tb // hr
    nh = t // hr

    def col(off):
        return pl.BlockSpec((1, tb, lg), lambda bb, i, g: (bb, i, off * ng + g))

    def halo_prev(off):
        return pl.BlockSpec((1, hr, lg), lambda bb, i, g: (bb, jnp.maximum(i * hb - 1, 0), off * ng + g))

    def halo_next(off):
        return pl.BlockSpec((1, hr, lg), lambda bb, i, g: (bb, jnp.minimum((i + 1) * hb, nh - 1), off * ng + g))

    def wcol(rows, off):
        return pl.BlockSpec((rows, lg), lambda bb, i, g: (0, off * ng + g))

    in_specs = ([col(o) for o in range(4)] + [halo_prev(o) for o in range(3)]
                + [halo_next(o) for o in range(3)] + [wcol(3, o) for o in range(3)]
                + [wcol(1, o) for o in range(3)])
    args = [proj] * 10 + [conv_w] * 3 + [conv_b[None]] * 3
    act = pl.BlockSpec((1, tb, lg), lambda bb, i, g: (bb, i, g))
    return pl.pallas_call(
        functools.partial(_hy_pre_kernel, nxb=nxb, nblk=nblk),
        grid=(b, nblk, ng),
        in_specs=in_specs, out_specs=[act, act],
        out_shape=[jax.ShapeDtypeStruct((b, t, e), F32)] * 2,
        compiler_params=_cparams("parallel", "parallel", "parallel"),
        name="hyena_short_conv",
    )(*args)


def _hy_filter_kernel(fv_ref, w1_ref, b1_ref, w2_ref, b2_ref, w3_ref, b3_ref, sf_ref,
                      w4f_ref, w4b_ref, dl_ref, hf_ref, hb_ref, *, length):
    tl = hf_ref.shape[0]
    n = (pl.program_id(0) * tl + lax.broadcasted_iota(jnp.int32, (tl, 1), 0)).astype(F32)
    t = n * (1.0 / (length - 1))
    lane = lax.broadcasted_iota(jnp.int32, (tl, LANE), 1)
    nb = (HYENA_EMB - 1) // 2
    ang = (2.0 * math.pi / length) * n * fv_ref[...]
    z = jnp.where(lane == 0, t,
                  jnp.where(lane <= nb, jnp.cos(ang),
                            jnp.where(lane <= 2 * nb, -jnp.sin(ang), 0.0)))
    sf = sf_ref[...]
    hdn = jnp.sin(sf * (_dot(z, w1_ref[...], HIGHEST) + b1_ref[...]))
    hdn = jnp.sin(sf * (_dot(hdn, w2_ref[...], HIGHEST) + b2_ref[...]))
    hdn = jnp.sin(sf * (_dot(hdn, w3_ref[...], HIGHEST) + b3_ref[...]))
    window = jnp.exp(-t * dl_ref[...])
    hf_ref[...] = _dot(hdn, w4f_ref[...], HIGHEST) * window
    hb_ref[...] = _dot(hdn, w4b_ref[...], HIGHEST) * window


def _hy_filters(length, p, e):
    tl = min(length, 1024)
    lg = 512
    ng = e // lg
    nb = (HYENA_EMB - 1) // 2
    freqs = np.linspace(1e-4, nb - 1, nb, dtype=np.float32)
    fv = np.zeros((1, LANE), np.float32)
    fv[0, 1:1 + nb] = freqs
    fv[0, 1 + nb:1 + 2 * nb] = freqs
    deltas = np.abs(np.linspace(math.log(HYENA_TARGET) / HYENA_SLOW_DECAY,
                                math.log(HYENA_TARGET) / HYENA_FAST_DECAY, e, dtype=np.float32))[None]
    fw = HYENA_FILTER_WIDTH
    w1 = jnp.pad(p['f_w1'], ((0, LANE - HYENA_EMB), (0, 0)))
    full = lambda shape: pl.BlockSpec(shape, lambda r, g: (0, 0))
    in_specs = [full((1, LANE)), full((LANE, fw)), full((1, fw)), full((fw, fw)), full((1, fw)),
                full((fw, fw)), full((1, fw)), full((1, fw)),
                pl.BlockSpec((fw, lg), lambda r, g: (0, g)),
                pl.BlockSpec((fw, lg), lambda r, g: (0, ng + g)),
                pl.BlockSpec((1, lg), lambda r, g: (0, g))]
    out = pl.BlockSpec((tl, lg), lambda r, g: (r, g))
    return pl.pallas_call(
        functools.partial(_hy_filter_kernel, length=length),
        grid=(length // tl, ng),
        in_specs=in_specs, out_specs=[out, out],
        out_shape=[jax.ShapeDtypeStruct((length, e), F32)] * 2,
        compiler_params=_cparams("parallel", "parallel"),
        name="hyena_filters",
    )(jnp.asarray(fv), w1, p['f_b1'][None], p['f_w2'], p['f_b2'][None], p['f_w3'], p['f_b3'][None],
      p['sin_freq'][None], p['f_w4'], p['f_w4'], jnp.asarray(deltas))


def _cmul(x, h, half):
    xr, xi = x[:half], x[half:]
    hr, hi = h[:half], h[half:]
    return jnp.concatenate([xr * hr - xi * hi, xr * hi + xi * hr], axis=0)


def _conj(x, half):
    return jnp.concatenate([x[:half], -x[half:]], axis=0)


def _tdot(tab, x, idx=None):
    get = (lambda ref: ref[...]) if idx is None else (lambda ref: ref[idx])
    xh = x.astype(BF16)
    out = _dot(get(tab[0]), xh)
    if len(tab) == 2:
        xl = (x - xh.astype(F32)).astype(BF16)
        out = out + _dot(get(tab[0]), xl) + _dot(get(tab[1]), xh)
    return out


def _split_tables(refs, count):
    per = len(FFT_TABLE_PARTS)
    return [refs[i * per:(i + 1) * per] for i in range(count)], refs[count * per:]


def _long_conv_direct_kernel(u_ref, g0_ref, hf_ref, hb_ref, fb_ref, *rest):
    (fd_t, fi_t), (o_ref, h_s) = _split_tables(rest, 2)
    half = fd_t[0].shape[0] // 2

    @pl.when(pl.program_id(1) == 0)
    def _():
        h_s[...] = _tdot(fd_t, hf_ref[...]) + _conj(_tdot(fd_t, hb_ref[...]), half)

    u = u_ref[0]
    y = _tdot(fi_t, _cmul(_tdot(fd_t, u), h_s[...], half))
    o_ref[0] = (y + u * fb_ref[...]) * g0_ref[0]


def _long_conv_kernel(u_ref, g0_ref, hf_ref, hb_ref, fb_ref, *rest):
    (t1_t, t1t_t, f2_t, f2t_t), (o_ref, x_s, a_s, h_s) = _split_tables(rest, 4)
    n2, rows1, n1h = t1_t[0].shape
    n1 = rows1 // 2
    un = FFT_UNROLL
    xp = _odd_pitch(n2)
    ap = _odd_pitch(rows1)

    def a_block(m2):
        return pl.ds(pl.multiple_of(m2 * ap, 8), rows1)

    def stage1(src):
        def copy(m1, carry):
            x_s[pl.ds(pl.multiple_of(m1 * xp, 8), n2), :] = src(pl.ds(pl.multiple_of(m1 * n2, n2), n2))
            return carry
        lax.fori_loop(0, n1h, copy, 0, unroll=4)

        def body(j, carry):
            m2s = [j * un + q for q in range(un)]
            xs = [x_s[pl.ds(m2, n1h, stride=xp), :] for m2 in m2s]
            outs = [_tdot(t1_t, x, m2) for m2, x in zip(m2s, xs)]
            for m2, o in zip(m2s, outs):
                a_s[a_block(m2), :] = o
            return carry
        lax.fori_loop(0, n2 // un, body, 0)

    def stage2(k1s):
        zs = [jnp.concatenate([a_s[pl.ds(k1, n2, stride=ap), :],
                               a_s[pl.ds(n1 + k1, n2, stride=ap), :]], axis=0) for k1 in k1s]
        return [_tdot(f2_t, z) for z in zs]

    def spec_rows(k1):
        return pl.ds(pl.multiple_of(k1 * 2 * n2, 2 * n2), 2 * n2)

    @pl.when(pl.program_id(1) == 0)
    def _():
        stage1(lambda rows: hf_ref[rows, :])

        def spec_f(j, carry):
            k1s = [j * un + q for q in range(un)]
            for k1, x in zip(k1s, stage2(k1s)):
                h_s[spec_rows(k1), :] = x
            return carry
        lax.fori_loop(0, n1 // un, spec_f, 0)
        stage1(lambda rows: hb_ref[rows, :])

        def spec_b(j, carry):
            k1s = [j * un + q for q in range(un)]
            for k1, x in zip(k1s, stage2(k1s)):
                h_s[spec_rows(k1), :] = h_s[spec_rows(k1), :] + _conj(x, n2)
            return carry
        lax.fori_loop(0, n1 // un, spec_b, 0)

    stage1(lambda rows: u_ref[0, rows, :])

    def mid(j, carry):
        k1s = [j * un + q for q in range(un)]
        ys = [_cmul(x, h_s[spec_rows(k1), :], n2) for k1, x in zip(k1s, stage2(k1s))]
        zs = [_tdot(f2t_t, y) for y in ys]
        for k1, z in zip(k1s, zs):
            a_s[pl.ds(k1, n2, stride=ap), :] = z[:n2]
            a_s[pl.ds(n1 + k1, n2, stride=ap), :] = z[n2:]
        return carry
    lax.fori_loop(0, n1 // un, mid, 0)

    def inv1(j, carry):
        m2s = [j * un + q for q in range(un)]
        ys = [_tdot(t1t_t, a_s[a_block(m2), :], m2) for m2 in m2s]
        for m2, y in zip(m2s, ys):
            x_s[pl.ds(m2, n1h, stride=xp), :] = y
        return carry
    lax.fori_loop(0, n2 // un, inv1, 0)

    fb = fb_ref[...]

    def finish(m1, carry):
        rows = pl.ds(pl.multiple_of(m1 * n2, n2), n2)
        y = x_s[pl.ds(pl.multiple_of(m1 * xp, 8), n2), :]
        o_ref[0, rows, :] = (y + u_ref[0, rows, :] * fb) * g0_ref[0, rows, :]
        return carry
    lax.fori_loop(0, n1h, finish, 0, unroll=4)


def _odd_pitch(rows):
    return rows if (rows // 8) % 2 else rows + 8


def _fft_tables(length):
    n = 2 * length
    n2 = FFT_N2
    n1 = n // n2
    n1h = n1 // 2
    k1 = np.arange(n1, dtype=np.float64)[:, None]
    m1 = np.arange(n1h, dtype=np.float64)[None, :]
    t1 = []
    for m2 in range(n2):
        phi = 2.0 * np.pi * (k1 * m1 / n1 + k1 * m2 / n)
        t1.append(np.concatenate([np.cos(phi), -np.sin(phi)], axis=0))
    t1 = np.stack(t1)
    t1t = np.transpose(t1, (0, 2, 1)) / n
    k2 = np.arange(n2, dtype=np.float64)[:, None]
    m2 = np.arange(n2, dtype=np.float64)[None, :]
    th = 2.0 * np.pi * k2 * m2 / n2
    mr, mi = np.cos(th), -np.sin(th)
    f2 = np.block([[mr, -mi], [mi, mr]])
    return _hi_lo(t1) + _hi_lo(t1t) + _hi_lo(f2) + _hi_lo(f2.T)


def _hi_lo(a):
    a32 = jnp.asarray(a.astype(np.float32))
    hi = a32.astype(BF16)
    parts = {"hi": hi, "lo": (a32 - hi.astype(F32)).astype(BF16)}
    return [parts[name] for name in FFT_TABLE_PARTS]


def _dft_tables(length):
    n = 2 * length
    k = np.arange(n, dtype=np.float64)[:, None]
    m = np.arange(length, dtype=np.float64)[None, :]
    phi = 2.0 * np.pi * k * m / n
    fd = np.concatenate([np.cos(phi), -np.sin(phi)], axis=0)
    return _hi_lo(fd) + _hi_lo(fd.T / n)


def _long_conv(u, g0, hf, hb, fbias, *, length, blk_index):
    b, t, e = u.shape
    lg = FFT_LANES
    ng = e // lg
    seq_spec = pl.BlockSpec((1, length, lg), lambda g, bb: (bb, blk_index, g))
    filt_spec = pl.BlockSpec((length, lg), lambda g, bb: (0, g))
    row_spec = pl.BlockSpec((1, lg), lambda g, bb: (0, g))
    out_spec = pl.BlockSpec((1, length, lg), lambda g, bb: (bb, 0, g))
    common = dict(
        grid=(ng, b),
        out_specs=out_spec,
        out_shape=jax.ShapeDtypeStruct((b, length, e), F32),
        compiler_params=_cparams("parallel", "arbitrary"),
    )
    const = lambda a: pl.BlockSpec(a.shape, lambda g, bb: (0,) * a.ndim)
    if length <= TOKEN_BLOCK:
        tabs = _dft_tables(length)
        return pl.pallas_call(
            _long_conv_direct_kernel,
            in_specs=[seq_spec, seq_spec, filt_spec, filt_spec, row_spec] + [const(a) for a in tabs],
            scratch_shapes=[pltpu.VMEM((4 * length, lg), F32)],
            name="hyena_long_conv_ctx", **common,
        )(u, g0, hf, hb, fbias, *tabs)
    tabs = _fft_tables(length)
    n2 = FFT_N2
    n1 = 2 * length // n2
    return pl.pallas_call(
        _long_conv_kernel,
        in_specs=[seq_spec, seq_spec, filt_spec, filt_spec, row_spec] + [const(a) for a in tabs],
        scratch_shapes=[pltpu.VMEM((n1 // 2 * _odd_pitch(n2), lg), F32),
                        pltpu.VMEM((n2 * _odd_pitch(2 * n1), lg), F32),
                        pltpu.VMEM((4 * length, lg), F32)],
        name="hyena_long_conv", **common,
    )(u, g0, hf, hb, fbias, *tabs)


def _hyena_layer(x, mod, g_pre, g_post, w_out, p, *, nxb, nblk_out):
    b, t, d = x.shape
    e = p['filter_bias'].shape[0]
    seq = nxb * TOKEN_BLOCK
    ctx_len = t - seq
    h = _pre_norm(x, g_pre, mod, shift=False, nxb=nxb)
    proj = _project(h.reshape(b * t, d), _bf(p['w_in']), out_dtype=BF16).reshape(b, t, 4 * e)
    u, g0 = _hy_pre(proj, p['conv_w'], p['conv_b'], nxb=nxb)
    fbias = p['filter_bias'][None]
    hf, hb = _hy_filters(seq, p, e)
    yx = _long_conv(u, g0, hf, hb, fbias, length=seq, blk_index=0)
    hf, hb = _hy_filters(ctx_len, p, e)
    yc = _long_conv(u, g0, hf, hb, fbias, length=ctx_len, blk_index=seq // ctx_len)
    y = jnp.concatenate([yx, yc], axis=1)
    return _out_residual(y, _bf(w_out), x, g_post, mod, nxb=nxb, nblk_out=nblk_out)


def _modulation(c, c_ctx, ada_w, ada_b):
    b, d = c.shape
    depth = ada_w.shape[0]
    rows = -(-(b + 1) // 8) * 8
    cstack = jnp.zeros((rows, d), F32).at[:b].set(c).at[b].set(c_ctx)
    ada = _ada_all(cstack, ada_w, ada_b).reshape(depth, rows, 3, d)
    lat = ada[:, :b]
    cx = jnp.broadcast_to(ada[:, b:b + 1], lat.shape)
    return jnp.stack([lat, cx], axis=2)


def kernel(x, c, ctx, c_ctx, ada_w, ada_b, norm_pre, norm_post, w_out,
           l0_w_in, l0_mu, l0_w0, l0_w2, l0_a0, l0_a2, l0_k_k, l0_k_a, l0_r_k, l0_ln_w, l0_ln_b,
           l1_w_in, l1_conv_w, l1_conv_b, l1_f_w1, l1_f_b1, l1_f_w2, l1_f_b2, l1_f_w3, l1_f_b3,
           l1_f_w4, l1_sin_freq, l1_filter_bias,
           l2_w_in, l2_g_norm, hgrn_lb_logits,
           l3_w_in, l3_mu, l3_w0, l3_w2, l3_a0, l3_a2, l3_k_k, l3_k_a, l3_r_k, l3_ln_w, l3_ln_b,
           l3_v0, l3_v2):
    rwkv0 = dict(w_in=l0_w_in, mu=l0_mu, w0=l0_w0, w2=l0_w2, a0=l0_a0, a2=l0_a2, k_k=l0_k_k,
                 k_a=l0_k_a, r_k=l0_r_k, ln_w=l0_ln_w, ln_b=l0_ln_b)
    hyena1 = dict(w_in=l1_w_in, conv_w=l1_conv_w, conv_b=l1_conv_b, f_w1=l1_f_w1, f_b1=l1_f_b1,
                  f_w2=l1_f_w2, f_b2=l1_f_b2, f_w3=l1_f_w3, f_b3=l1_f_b3, f_w4=l1_f_w4,
                  sin_freq=l1_sin_freq, filter_bias=l1_filter_bias)
    hgrn2 = dict(w_in=l2_w_in, g_norm=l2_g_norm)
    rwkv3 = dict(w_in=l3_w_in, mu=l3_mu, w0=l3_w0, w2=l3_w2, a0=l3_a0, a2=l3_a2, k_k=l3_k_k,
                 k_a=l3_k_a, r_k=l3_r_k, ln_w=l3_ln_w, ln_b=l3_ln_b, v0=l3_v0, v2=l3_v2)
    b, seq, d = x.shape
    assert ctx.shape[1] == TOKEN_BLOCK and seq % TOKEN_BLOCK == 0
    nxb = seq // TOKEN_BLOCK
    mods = _modulation(c, c_ctx, ada_w, ada_b)
    xa = jnp.concatenate([x, ctx], axis=1)
    xa, proj0 = _rwkv_layer(xa, mods[0], norm_pre[0][None], norm_post[0][None], w_out[0], rwkv0,
                            None, nxb=nxb, nblk_out=nxb + 1)
    xa = _hyena_layer(xa, mods[1], norm_pre[1][None], norm_post[1][None], w_out[1], hyena1,
                      nxb=nxb, nblk_out=nxb + 1)
    xa = _hgrn_layer(xa, mods[2], norm_pre[2][None], norm_post[2][None], w_out[2], hgrn2,
                     hgrn_lb_logits, nxb=nxb, nblk_out=nxb + 1, layer=2)
    xa, _ = _rwkv_layer(xa, mods[3], norm_pre[3][None], norm_post[3][None], w_out[3], rwkv3,
                        proj0, nxb=nxb, nblk_out=nxb)
    return xa
```

```python
import functools
import math

import jax
import jax.numpy as jnp
import numpy as np
from jax import lax
from jax.experimental import pallas as pl
from jax.experimental.pallas import tpu as pltpu

F32 = jnp.float32
BF16 = jnp.bfloat16
HIGHEST = lax.Precision.HIGHEST

NORM_EPS = 1e-6
GRID_W = 64
TOKEN_BLOCK = 256
LANE = 128
VMEM_LIMIT = 56 * 1024 * 1024

RWKV_HEAD = 64
RWKV_CHUNK = 64
RWKV_LANES = 512
RWKV_GN_EPS = 64e-5
TAIL_WIDTH = 384

HGRN_HEAD = 128
HGRN_SUB = 16
HGRN_LANES = 512
HGRN_UNROLL = 16

HYENA_EMB = 33
HYENA_FILTER_WIDTH = 64
HYENA_FAST_DECAY = 0.3
HYENA_SLOW_DECAY = 1.5
HYENA_TARGET = 1e-2
FFT_N2 = 64
FFT_LANES = 128
FFT_TABLE_PARTS = ("hi",)
FFT_UNROLL = 8


def _cparams(*sem):
    return pltpu.CompilerParams(dimension_semantics=sem, vmem_limit_bytes=VMEM_LIMIT)


def _dot(a, b, precision=None):
    return jnp.dot(a, b, preferred_element_type=F32, precision=precision)


def _dot_nt(a, b, precision=None):
    return lax.dot_general(a, b, (((1,), (1,)), ((), ())),
                           preferred_element_type=F32, precision=precision)


def _dot_tn(a, b):
    return lax.dot_general(a, b, (((0,), (0,)), ((), ())), preferred_element_type=F32)


def _bf(x):
    return x.astype(BF16)


def _split_dot(x, w_bf16):
    hi = x.astype(BF16)
    lo = (x - hi.astype(F32)).astype(BF16)
    return _dot(hi, w_bf16) + _dot(lo, w_bf16)


def _mask_dot(m_bf16, x):
    hi = x.astype(BF16)
    lo = (x - hi.astype(F32)).astype(BF16)
    return _dot(m_bf16, hi) + _dot(m_bf16, lo)


def _tree_sum(terms):
    terms = list(terms)
    while len(terms) > 1:
        terms = [a + b for a, b in zip(terms[0::2], terms[1::2])] + (terms[-1:] if len(terms) % 2 else [])
    return terms[0]


def _sigmoid(x):
    return 1.0 / (1.0 + jnp.exp(-x))


def _silu(x):
    return x * _sigmoid(x)


def _ada_kernel(c_ref, w_ref, b_ref, o_ref):
    o_ref[0] = _dot(_silu(c_ref[...]), w_ref[0], HIGHEST) + b_ref[0]


def _ada_all(cstack, ada_w, ada_b):
    depth, d, d3 = ada_w.shape
    rows = cstack.shape[0]
    nt = d3 // d
    return pl.pallas_call(
        _ada_kernel,
        grid=(depth, nt),
        in_specs=[
            pl.BlockSpec((rows, d), lambda l, j: (0, 0)),
            pl.BlockSpec((1, d, d), lambda l, j: (l, 0, j)),
            pl.BlockSpec((1, 1, d), lambda l, j: (l, 0, j)),
        ],
        out_specs=pl.BlockSpec((1, rows, d), lambda l, j: (l, 0, j)),
        out_shape=jax.ShapeDtypeStruct((depth, rows, d3), F32),
        compiler_params=_cparams("parallel", "parallel"),
        name="adaln",
    )(cstack, ada_w, ada_b.reshape(depth, 1, d3))


def _norm_kernel(*refs, shift, nxb):
    if shift:
        x_ref, xp_ref, xn_ref, g_ref, mod_ref, h_ref, d_ref = refs
    else:
        x_ref, g_ref, mod_ref, h_ref = refs
    i = pl.program_id(1)
    g = g_ref[...]
    shift_v = mod_ref[0, 0, 0:1, :]
    scale1p = 1.0 + mod_ref[0, 0, 1:2, :]

    def nrm(x):
        ms = jnp.mean(x * x, axis=-1, keepdims=True)
        return x * lax.rsqrt(ms + NORM_EPS) * g * scale1p + shift_v

    h = nrm(x_ref[0])
    h_ref[0] = h.astype(BF16)
    if not shift:
        return
    tb, d = h.shape
    q = d // 4
    row = lax.broadcasted_iota(jnp.int32, (tb, 1), 0)

    @pl.when(i < nxb)
    def _():
        col = row % GRID_W
        left = jnp.where(col > 0, pltpu.roll(h[:, 0:q], 1, 0), 0.0)
        right = jnp.where(col < GRID_W - 1, pltpu.roll(h[:, q:2 * q], tb - 1, 0), 0.0)
        hp = nrm(xp_ref[0])[:, 2 * q:3 * q]
        hn = nrm(xn_ref[0])[:, 3 * q:]
        hp = jnp.where(i > 0, hp, 0.0)
        hn = jnp.where(i < nxb - 1, hn, 0.0)
        up = jnp.concatenate([hp, h[:tb - GRID_W, 2 * q:3 * q]], axis=0)
        down = jnp.concatenate([h[GRID_W:, 3 * q:], hn], axis=0)
        hs = jnp.concatenate([left, right, up, down], axis=-1)
        d_ref[0] = (hs - h).astype(BF16)

    @pl.when(i >= nxb)
    def _():
        half = d // 2
        prev = jnp.where(row > 0, pltpu.roll(h[:, :half], 1, 0), 0.0)
        nxt = jnp.where(row < tb - 1, pltpu.roll(h[:, half:], tb - 1, 0), 0.0)
        hs = jnp.concatenate([prev, nxt], axis=-1)
        d_ref[0] = (hs - h).astype(BF16)


def _pre_norm(x, g, mod, *, shift, nxb):
    b, t, d = x.shape
    tb = TOKEN_BLOCK
    nblk = t // tb
    hb = tb // GRID_W
    nhalo = t // GRID_W
    seg = lambda i: jnp.where(i < nxb, 0, 1)
    x_spec = pl.BlockSpec((1, tb, d), lambda bb, i: (bb, i, 0))
    g_spec = pl.BlockSpec((1, d), lambda bb, i: (0, 0))
    mod_spec = pl.BlockSpec((1, 1, 3, d), lambda bb, i: (bb, seg(i), 0, 0))
    out_spec = pl.BlockSpec((1, tb, d), lambda bb, i: (bb, i, 0))
    if shift:
        in_specs = [
            x_spec,
            pl.BlockSpec((1, GRID_W, d), lambda bb, i: (bb, jnp.maximum(i * hb - 1, 0), 0)),
            pl.BlockSpec((1, GRID_W, d), lambda bb, i: (bb, jnp.minimum((i + 1) * hb, nhalo - 1), 0)),
            g_spec, mod_spec,
        ]
        args = (x, x, x, g, mod)
        out_specs = [out_spec, out_spec]
        out_shape = [jax.ShapeDtypeStruct((b, t, d), BF16)] * 2
    else:
        in_specs = [x_spec, g_spec, mod_spec]
        args = (x, g, mod)
        out_specs = out_spec
        out_shape = jax.ShapeDtypeStruct((b, t, d), BF16)
    return pl.pallas_call(
        functools.partial(_norm_kernel, shift=shift, nxb=nxb),
        grid=(b, nblk),
        in_specs=in_specs,
        out_specs=out_specs,
        out_shape=out_shape,
        compiler_params=_cparams("parallel", "parallel"),
        name="pre_norm_shift" if shift else "pre_norm",
    )(*args)


def _row_tile(m):
    for tm in (1024, 512, 256):
        if m % tm == 0:
            return tm
    raise ValueError(f"token count {m} is not a multiple of {TOKEN_BLOCK}")


def _proj_lerp_kernel(h_ref, d_ref, mu_ref, w_ref, o_ref, lhs_ref, *, tiles_per_group):
    j = pl.program_id(1)

    @pl.when(j % tiles_per_group == 0)
    def _():
        lhs_ref[...] = (h_ref[...].astype(F32) + mu_ref[0] * d_ref[...].astype(F32)).astype(BF16)

    o_ref[...] = _dot(lhs_ref[...], w_ref[...]).astype(o_ref.dtype)


def _proj_kernel(h_ref, w_ref, o_ref):
    o_ref[...] = _dot(h_ref[...], w_ref[...]).astype(o_ref.dtype)


def _project(h2, w_bf16, *, out_dtype, d2=None, mu=None, group_width=None):
    m, d = h2.shape
    n = w_bf16.shape[1]
    tm = _row_tile(m)
    tn = 1024
    lhs_spec = pl.BlockSpec((tm, d), lambda i, j: (i, 0))
    w_spec = pl.BlockSpec((d, tn), lambda i, j: (0, j))
    o_spec = pl.BlockSpec((tm, tn), lambda i, j: (i, j))
    if d2 is None:
        return pl.pallas_call(
            _proj_kernel, grid=(m // tm, n // tn),
            in_specs=[lhs_spec, w_spec], out_specs=o_spec,
            out_shape=jax.ShapeDtypeStruct((m, n), out_dtype),
            compiler_params=_cparams("parallel", "parallel"),
            name="project",
        )(h2, w_bf16)
    tpg = group_width // tn
    return pl.pallas_call(
        functools.partial(_proj_lerp_kernel, tiles_per_group=tpg),
        grid=(m // tm, n // tn),
        in_specs=[lhs_spec, lhs_spec,
                  pl.BlockSpec((1, 1, d), lambda i, j: (j // tpg, 0, 0)),
                  w_spec],
        out_specs=o_spec,
        out_shape=jax.ShapeDtypeStruct((m, n), out_dtype),
        scratch_shapes=[pltpu.VMEM((tm, d), BF16)],
        compiler_params=_cparams("parallel", "arbitrary"),
        name="project_lerp",
    )(h2, d2, mu, w_bf16)


def _tail_kernel(h_ref, d_ref, w_ref, mu_ref, o_ref):
    w = w_ref[...]
    o_ref[...] = _dot(h_ref[...], _bf(w)) + _dot(d_ref[...], _bf(w * mu_ref[...]))


def _project_tail(h2, d2, w_tail, mu_cols):
    m, d = h2.shape
    n = w_tail.shape[1]
    tm = _row_tile(m)
    lhs_spec = pl.BlockSpec((tm, d), lambda i: (i, 0))
    w_spec = pl.BlockSpec((d, n), lambda i: (0, 0))
    return pl.pallas_call(
        _tail_kernel, grid=(m // tm,),
        in_specs=[lhs_spec, lhs_spec, w_spec, w_spec],
        out_specs=pl.BlockSpec((tm, n), lambda i: (i, 0)),
        out_shape=jax.ShapeDtypeStruct((m, n), F32),
        compiler_params=_cparams("parallel"),
        name="project_tail",
    )(h2, d2, w_tail, mu_cols)


def _out_kernel(u_ref, w_ref, x_ref, g_ref, mod_ref, o_ref):
    y = _dot(_bf(u_ref[0]), w_ref[...])
    ms = jnp.mean(y * y, axis=-1, keepdims=True)
    yn = y * lax.rsqrt(ms + NORM_EPS) * g_ref[...]
    o_ref[0] = x_ref[0] + yn * mod_ref[0, 0, 2:3, :]


def _out_residual(u, w_bf16, x, g, mod, *, nxb, nblk_out):
    b, t, e = u.shape
    d = x.shape[-1]
    tb = TOKEN_BLOCK
    seg = lambda i: jnp.where(i < nxb, 0, 1)
    return pl.pallas_call(
        _out_kernel, grid=(b, nblk_out),
        in_specs=[
            pl.BlockSpec((1, tb, e), lambda bb, i: (bb, i, 0)),
            pl.BlockSpec((e, d), lambda bb, i: (0, 0)),
            pl.BlockSpec((1, tb, d), lambda bb, i: (bb, i, 0)),
            pl.BlockSpec((1, d), lambda bb, i: (0, 0)),
            pl.BlockSpec((1, 1, 3, d), lambda bb, i: (bb, seg(i), 0, 0)),
        ],
        out_specs=pl.BlockSpec((1, tb, d), lambda bb, i: (bb, i, 0)),
        out_shape=jax.ShapeDtypeStruct((b, nblk_out * tb, d), F32),
        compiler_params=_cparams("parallel", "parallel"),
        name="out_residual",
    )(u, w_bf16, x, g, mod)


def _rwkv_scan_kernel(*refs, reverse, vres, last):
    it = iter(refs)
    r_ref, k_ref, v_ref, lo_ref, w2_ref, a2_ref = (next(it) for _ in range(6))
    v2_ref = next(it) if vres else None
    vec_ref = next(it)
    vf_ref = next(it) if vres else None
    if last:
        gate_ref, y0_ref, b0_ref, u_ref = (next(it) for _ in range(4))
    else:
        yo_ref, bo_ref = next(it), next(it)
    s_ref, y_s, b_s = it

    i = pl.program_id(2)
    tb, lg = y_s.shape
    npair = lg // LANE
    c = RWKV_CHUNK
    nch = tb // c
    hd = RWKV_HEAD

    @pl.when(i == 0)
    def _():
        s_ref[...] = jnp.zeros_like(s_ref)

    vec = vec_ref[...]
    lo = lo_ref[0]
    r = r_ref[0].astype(F32)
    k = k_ref[0].astype(F32)
    v = v_ref[0].astype(F32)
    zt = vec[0:1] + _dot(_bf(jnp.tanh(lo[:, :LANE])), w2_ref[...])
    logw = -(math.exp(-0.5) * math.log2(math.e)) * _sigmoid(zt)
    a = _sigmoid(vec[1:2] + _dot(_bf(lo[:, LANE:2 * LANE]), a2_ref[...]))
    if vres:
        v = v + (vf_ref[0].astype(F32) - v) * _sigmoid(vec[5:6] + _dot(_bf(lo[:, 2 * LANE:]), v2_ref[...]))
    kdir = k * (1.0 + (a - 1.0) * vec[3:4])
    kkr = k * vec[2:3]
    bd = (lax.broadcasted_iota(jnp.int32, (LANE, LANE), 0) // hd
          == lax.broadcasted_iota(jnp.int32, (LANE, LANE), 1) // hd)
    ones_bd = jnp.where(bd, 1.0, 0.0).astype(BF16)

    def segsum(x):
        return jnp.concatenate(
            [_dot(_bf(x[:, p * LANE:(p + 1) * LANE]), ones_bd) for p in range(npair)], axis=-1)

    kk = kkr / jnp.maximum(jnp.sqrt(segsum(kkr * kkr)), 1e-12)
    b_s[...] = segsum(r * kdir * vec[4:5]) * v
    alpha = -kk
    beta = kk * a

    t1 = lax.broadcasted_iota(jnp.int32, (c, c), 0)
    s1 = lax.broadcasted_iota(jnp.int32, (c, c), 1)
    t2 = lax.broadcasted_iota(jnp.int32, (c, LANE), 0)
    lane2 = lax.broadcasted_iota(jnp.int32, (c, LANE), 1)
    s2 = lane2 % c
    if reverse:
        incl1, incl2, strict2 = s1 >= t1, s2 >= t2, s2 > t2
    else:
        incl1, incl2, strict2 = s1 <= t1, s2 <= t2, s2 < t2
    tri = jnp.where(incl1, 1.0, 0.0).astype(BF16)
    low = lane2 < hd
    high = jnp.logical_not(low)
    low_x = lax.broadcasted_iota(jnp.int32, (2 * c, LANE), 1) < hd
    strict_lo, strict_hi = strict2 & low, strict2 & high
    incl_lo, incl_hi = incl2 & low, incl2 & high
    eye2 = jnp.where(lax.broadcasted_iota(jnp.int32, (LANE, LANE), 0)
                     == lax.broadcasted_iota(jnp.int32, (LANE, LANE), 1), 1.0, 0.0)
    zeros_cv = jnp.zeros((c, LANE), F32)

    def stack(top, bot):
        return jnp.concatenate([top, bot], axis=0)

    def fold(z):
        return z[:c] + z[c:]

    chains = {}
    rounds = int(math.log2(c)) - 1

    def st_decay(ci):
        rows = slice(ci * c, (ci + 1) * c)
        lw = logw[rows]
        lc = _mask_dot(tri, lw)
        ltot = lc[0:1] if reverse else lc[c - 1:c]
        p_inv = jnp.exp2(-lc)
        p_all = jnp.exp2(ltot)
        ab = alpha[rows] * jnp.exp2(lc - lw)
        rb = r[rows] * jnp.exp2(lc)
        bt = beta[rows] * p_inv
        kt = kdir[rows] * p_inv
        bh = bt * p_all
        kh = kt * p_all
        chains[ci] = []
        for p in range(npair):
            sl = slice(p * LANE, (p + 1) * LANE)
            chains[ci].append(dict(p=p, ab=ab[:, sl], rb=rb[:, sl], bt=bt[:, sl], kt=kt[:, sl],
                                   bh=bh[:, sl], kh=kh[:, sl], v=v[rows, sl], p_all=p_all[:, sl]))

    def st_gram(ci):
        for d in chains[ci]:
            y01 = jnp.concatenate([jnp.where(low_x, stack(d['bt'], d['kt']), 0.0),
                                   jnp.where(low_x, 0.0, stack(d['kt'], d['bt']))], axis=0)
            d['g'] = _dot_nt(_bf(stack(d['ab'], d['rb'])), _bf(y01))

    def st_blocks(ci):
        for d in chains[ci]:
            g0t, g0b = d['g'][:c, :LANE], d['g'][c:, :LANE]
            g1t, g1b = d['g'][:c, LANE:], d['g'][c:, LANE:]
            d['a'] = stack(jnp.where(strict_lo, g0t, 0.0), jnp.where(strict_hi, g1t, 0.0))
            arb = stack(jnp.where(incl_lo, g0b, 0.0), jnp.where(incl_hi, g1b, 0.0))
            ark = stack(jnp.where(incl_hi, g0b, 0.0), jnp.where(incl_lo, g1b, 0.0))
            d['arbk'] = _bf(jnp.concatenate([arb, ark], axis=1))
            ak = stack(jnp.where(strict_hi, g0t, 0.0), jnp.where(strict_lo, g1t, 0.0))
            d['vx'] = stack(jnp.where(high, d['v'], 0.0), jnp.where(low, d['v'], 0.0))
            d['w'] = _dot(_bf(ak), _bf(d['vx']))
            del d['g']

    def st_square(ci):
        for d in chains[ci]:
            d['tm'] = eye2 + d['a']
            apb = _bf(d['a'])
            d['a'] = _dot(apb, apb)

    def st_round(ci):
        for d in chains[ci]:
            pt = _dot(_bf(d['a']), _bf(jnp.concatenate([d['a'], d['tm']], axis=1)))
            d['a'] = pt[:, :LANE]
            d['tm'] = d['tm'] + pt[:, LANE:]

    def st_last_round(ci):
        for d in chains[ci]:
            d['tm'] = d['tm'] + _dot(_bf(d['a']), _bf(d['tm']))

    def st_apply(ci):
        for d in chains[ci]:
            ab_st = stack(jnp.where(low, d['ab'], 0.0), jnp.where(high, d['ab'], 0.0))
            d['tz'] = _dot(_bf(d['tm']), _bf(jnp.concatenate([ab_st, d['w']], axis=1)))

    def st_out(ci):
        for d in chains[ci]:
            lower = jnp.concatenate([jnp.zeros((2 * c, LANE), F32), d['vx']], axis=1)
            yz = _dot(d['arbk'], _bf(stack(d['tz'], lower)))
            ta, tw = fold(d['tz'][:, :LANE]), fold(d['tz'][:, LANE:])
            d['ra'] = _bf(d['rb'] + fold(yz[:, :LANE]))
            d['yw'] = fold(yz[:, LANE:])
            lhs = stack(jnp.concatenate([ta, tw], axis=1), jnp.concatenate([zeros_cv, d['v']], axis=1))
            mn = _dot_tn(_bf(lhs), _bf(stack(d['bh'], d['kh'])))
            d['m'] = _bf(jnp.where(bd, mn[:LANE], 0.0))
            d['n'] = jnp.where(bd, mn[LANE:], 0.0)

    state = [s_ref[p] for p in range(npair)]

    def st_recur(ci):
        for d in chains[ci]:
            p = d['p']
            sp = state[p]
            spb = _bf(sp)
            y_s[ci * c:(ci + 1) * c, p * LANE:(p + 1) * LANE] = _dot_nt(d['ra'], spb) + d['yw']
            state[p] = sp * d['p_all'] + _dot(spb, d['m']) + d['n']

    stages = ([st_decay, st_gram, st_blocks, st_square] + [st_round] * (rounds - 1)
              + [st_last_round, st_apply, st_out])
    for stage in stages:
        for ci in range(nch):
            stage(ci)
    for ci in (reversed(range(nch)) if reverse else range(nch)):
        st_recur(ci)
    for p in range(npair):
        s_ref[p] = state[p]

    bonus = b_s[...]
    if not last:
        yo_ref[0] = y_s[...]
        bo_ref[0] = bonus
    else:
        yy = y0_ref[0] + y_s[...]
        mean = segsum(yy) * (1.0 / hd)
        yc = yy - mean
        var = segsum(yc * yc) * (1.0 / hd)
        yn = yc * lax.rsqrt(var + RWKV_GN_EPS) * vec[6:7] + vec[7:8]
        u_ref[0] = ((yn + b0_ref[0] + bonus) * _silu(gate_ref[0].astype(F32))).astype(BF16)


def _rwkv_scan(proj, tail, w2p, a2p, v2p, vec, proj0, y0, b0, *, reverse, vres, last, nxb):
    b, t, e4 = proj.shape
    e = e4 // 4
    tb = TOKEN_BLOCK
    lg = RWKV_LANES
    ng = e // lg
    nblk = t // tb
    assert nblk == nxb + 1, "the context prefix must be exactly one token block"
    if reverse:
        blk = lambda i: jnp.where(i == 0, nxb, nxb - i)
    else:
        blk = lambda i: jnp.where(i == 0, nxb, i - 1)

    def col(off):
        return pl.BlockSpec((1, tb, lg), lambda bb, g, i: (bb, blk(i), off * ng + g))

    wspec = pl.BlockSpec((LANE, lg), lambda bb, g, i: (0, g))
    in_specs = [col(0), col(1), col(2),
                pl.BlockSpec((1, tb, TAIL_WIDTH), lambda bb, g, i: (bb, blk(i), 0)),
                wspec, wspec]
    args = [proj, proj, proj, tail, w2p, a2p]
    if vres:
        in_specs.append(wspec)
        args.append(v2p)
    in_specs.append(pl.BlockSpec((8, lg), lambda bb, g, i: (0, g)))
    args.append(vec)
    if vres:
        in_specs.append(col(2))
        args.append(proj0)
    act = pl.BlockSpec((1, tb, lg), lambda bb, g, i: (bb, blk(i), g))
    if last:
        in_specs += [col(3), act, act]
        args += [proj, y0, b0]
        out_specs = act
        out_shape = jax.ShapeDtypeStruct((b, t, e), BF16)
    else:
        out_specs = [act, act]
        out_shape = [jax.ShapeDtypeStruct((b, t, e), F32)] * 2
    scratch = [pltpu.VMEM((lg // LANE, LANE, LANE), F32)] + [pltpu.VMEM((tb, lg), F32)] * 2
    return pl.pallas_call(
        functools.partial(_rwkv_scan_kernel, reverse=reverse, vres=vres, last=last),
        grid=(b, ng, nblk),
        in_specs=in_specs, out_specs=out_specs, out_shape=out_shape,
        scratch_shapes=scratch,
        compiler_params=_cparams("parallel", "parallel", "arbitrary"),
        name="rwkv_scan_bwd" if reverse else "rwkv_scan_fwd",
    )(*args)


def _rwkv_layer(x, mod, g_pre, g_post, w_out, p, proj0, *, nxb, nblk_out):
    b, t, d = x.shape
    e = p['k_k'].shape[0]
    vres = 'v0' in p
    h, dl = _pre_norm(x, g_pre, mod, shift=True, nxb=nxb)
    h2, d2 = h.reshape(b * t, d), dl.reshape(b * t, d)
    w_in = p['w_in']
    proj = _project(h2, _bf(w_in[:, :4 * e]), out_dtype=BF16, d2=d2, mu=p['mu'][:4, None, :],
                    group_width=e).reshape(b, t, 4 * e)
    n_lo = w_in.shape[1] - 4 * e
    groups = [4] * 128 + [5] * 128 + [2] * (n_lo - 256)
    pad = TAIL_WIDTH - n_lo
    w_tail = jnp.pad(w_in[:, 4 * e:], ((0, 0), (0, pad)))
    mu_cols = jnp.pad(p['mu'][np.asarray(groups)].T, ((0, 0), (0, pad)))
    tail = _project_tail(h2, d2, w_tail, mu_cols).reshape(b, t, TAIL_WIDTH)

    def lora(w, row0):
        return _bf(jnp.pad(w, ((row0, LANE - row0 - w.shape[0]), (0, 0))))

    zero = jnp.zeros((e,), F32)
    y0 = b0 = None
    for z in range(2):
        vec = jnp.stack([p['w0'][z], p['a0'][z], p['k_k'], p['k_a'], p['r_k'].reshape(e),
                         p['v0'] if vres else zero, p['ln_w'], p['ln_b']])
        out = _rwkv_scan(proj, tail, lora(p['w2'][z], 64 * z), lora(p['a2'][z], 64 * z),
                         lora(p['v2'], 0) if vres else None, vec, proj0, y0, b0,
                         reverse=(z == 1), vres=vres, last=(z == 1), nxb=nxb)
        if z == 0:
            y0, b0 = out
    x_new = _out_residual(out, _bf(w_out), x, g_post, mod, nxb=nxb, nblk_out=nblk_out)
    return x_new, proj


def _hgrn_scan_kernel(*refs, reverse, last, layer):
    it = iter(refs)
    q_ref, f_ref, i_ref, lb_ref = (next(it) for _ in range(4))
    if last:
        gate_ref, o0_ref, gn_ref, u_ref = (next(it) for _ in range(4))
    else:
        oo_ref = next(it)
    s_ref, gc_s, q_s, k_s, v_s, qg_s, kd_s, pt_s, o_s = it

    i = pl.program_id(2)
    tb, lg = q_s.shape
    nh = lg // HGRN_HEAD
    c = HGRN_SUB
    nsub = tb // c

    @pl.when(i == 0)
    def _():
        s_ref[...] = jnp.zeros_like(s_ref)

    logits = lb_ref[...]
    ex = jnp.exp(logits - jnp.max(logits, axis=0, keepdims=True))
    lb = jnp.sum(ex[1:layer + 1], axis=0, keepdims=True) / jnp.sum(ex, axis=0, keepdims=True)
    f = lb + (1.0 - lb) * _sigmoid(f_ref[0])
    g = jnp.log(f) * math.log2(math.e)
    r1 = lax.broadcasted_iota(jnp.int32, (tb, tb), 0)
    c1 = lax.broadcasted_iota(jnp.int32, (tb, tb), 1)
    same = (r1 // c) == (c1 // c)
    before = (c1 >= r1) if reverse else (c1 <= r1)
    gc = _mask_dot(jnp.where(same & before, 1.0, 0.0).astype(BF16), g)
    gtot = _mask_dot(jnp.where(same, 1.0, 0.0).astype(BF16), g)
    q = _silu(q_ref[0].astype(F32))
    kk = 1.0 - f
    gc_s[...] = gc
    q_s[...] = q
    k_s[...] = kk
    v_s[...] = i_ref[0].astype(F32)
    qg_s[...] = q * jnp.exp2(gc)
    kd_s[...] = kk * jnp.exp2(gtot - gc)
    pt_s[...] = jnp.exp2(gtot)

    trow = lax.broadcasted_iota(jnp.int32, (c, 1), 0)

    sls = [slice(hd * HGRN_HEAD, (hd + 1) * HGRN_HEAD) for hd in range(nh)]
    first, second = (slice(8, 16), slice(0, 8)) if reverse else (slice(0, 8), slice(8, 16))

    def pairwise(gcj, qj, kj, vj):
        terms = [[], []]
        for s in range(c):
            half = s // 8
            hs = slice(8 * half, 8 * half + 8)
            dlt = gcj[hs] - gcj[s:s + 1]
            if s != (8 * half + 7 if reverse else 8 * half):
                th = trow[hs]
                dlt = jnp.where((th <= s) if reverse else (th >= s), dlt, -1e30)
            w = qj[hs] * jnp.exp2(dlt) * kj[s:s + 1]
            terms[half].append(jnp.sum(w, axis=-1, keepdims=True) * vj[s:s + 1])
        return jnp.concatenate([_tree_sum(ts) for ts in terms], axis=0)

    def sub(jj, carry):
        subs = []
        for q in range(HGRN_UNROLL):
            step = jj * HGRN_UNROLL + q
            ji = (nsub - 1 - step) if reverse else step
            rows = pl.ds(pl.multiple_of(ji * c, c), c)
            subs.append(dict(rows=rows, gcs=[gc_s[rows, sl] for sl in sls],
                             qs=[q_s[rows, sl] for sl in sls], ks=[k_s[rows, sl] for sl in sls],
                             vs=[v_s[rows, sl] for sl in sls]))
        for d in subs:
            d['atts'] = []
            for gcj, qj, kj in zip(d['gcs'], d['qs'], d['ks']):
                gb = gcj[8:9] if reverse else gcj[7:8]
                qx = qj[second] * jnp.exp2(gcj[second] - gb)
                kx = kj[first] * jnp.exp2(gb - gcj[first])
                z8 = jnp.zeros_like(qx)
                q16 = jnp.concatenate([qx, z8] if reverse else [z8, qx], axis=0)
                k16 = jnp.concatenate([z8, kx] if reverse else [kx, z8], axis=0)
                d['atts'].append(_dot_nt(_bf(q16), _bf(k16)))
            d['upds'] = [_dot_tn(_bf(vj), _bf(kd_s[d['rows'], sl])) for sl, vj in zip(sls, d['vs'])]
        sts = [s_ref[hd] for hd in range(nh)]
        for d in subs:
            d['o_state'] = [_dot_nt(_bf(qg_s[d['rows'], sl]), _bf(st)) for sl, st in zip(sls, sts)]
            d['o_cross'] = [_dot(_bf(att), _bf(vj)) for att, vj in zip(d['atts'], d['vs'])]
            sts = [sts[hd] * pt_s[d['rows'], sls[hd]][0:1] + d['upds'][hd] for hd in range(nh)]
        for hd in range(nh):
            s_ref[hd] = sts[hd]
        for d in subs:
            for hd in range(nh):
                pair = pairwise(d['gcs'][hd], d['qs'][hd], d['ks'][hd], d['vs'][hd])
                o_s[d['rows'], sls[hd]] = (d['o_state'][hd] + d['o_cross'][hd]) + pair
        return carry

    lax.fori_loop(0, nsub // HGRN_UNROLL, sub, 0)

    if not last:
        oo_ref[0] = o_s[...]
    else:
        o = o0_ref[0] + o_s[...]
        gate = gate_ref[0].astype(F32)
        gn = gn_ref[...]
        for hd in range(nh):
            sl = slice(hd * HGRN_HEAD, (hd + 1) * HGRN_HEAD)
            oh = o[:, sl]
            ms = jnp.mean(oh * oh, axis=-1, keepdims=True)
            u_ref[0, :, sl] = (oh * lax.rsqrt(ms + NORM_EPS) * gn[:, sl] * _silu(gate[:, sl])).astype(BF16)


def _hgrn_scan(proj_qig, proj_f, lb_logits, gn, o0, *, reverse, last, nxb, layer):
    b, t, e3 = proj_qig.shape
    e = e3 // 3
    tb = TOKEN_BLOCK
    lg = HGRN_LANES
    ng = e // lg
    nblk = t // tb
    assert nblk == nxb + 1, "the context prefix must be exactly one token block"
    if reverse:
        blk = lambda i: jnp.where(i == 0, nxb, nxb - i)
    else:
        blk = lambda i: jnp.where(i == 0, nxb, i - 1)

    def col(off):
        return pl.BlockSpec((1, tb, lg), lambda bb, g, i: (bb, blk(i), off * ng + g))

    row = pl.BlockSpec((1, lg), lambda bb, g, i: (0, g))
    act = pl.BlockSpec((1, tb, lg), lambda bb, g, i: (bb, blk(i), g))
    in_specs = [col(0), col(1 if reverse else 0), col(1),
                pl.BlockSpec((lb_logits.shape[0], lg), lambda bb, g, i: (0, g))]
    args = [proj_qig, proj_f, proj_qig, lb_logits]
    if last:
        in_specs += [col(2), act, row]
        args += [proj_qig, o0, gn]
        out_shape = jax.ShapeDtypeStruct((b, t, e), BF16)
    else:
        out_shape = jax.ShapeDtypeStruct((b, t, e), F32)
    scratch = [pltpu.VMEM((lg // HGRN_HEAD, HGRN_HEAD, HGRN_HEAD), F32)] + [pltpu.VMEM((tb, lg), F32)] * 8
    return pl.pallas_call(
        functools.partial(_hgrn_scan_kernel, reverse=reverse, last=last, layer=layer),
        grid=(b, ng, nblk),
        in_specs=in_specs, out_specs=act, out_shape=out_shape,
        scratch_shapes=scratch,
        compiler_params=_cparams("parallel", "parallel", "arbitrary"),
        name="hgrn_scan_bwd" if reverse else "hgrn_scan_fwd",
    )(*args)


def _hgrn_layer(x, mod, g_pre, g_post, w_out, p, lb_logits, *, nxb, nblk_out, layer):
    b, t, d = x.shape
    e = lb_logits.shape[1]
    h = _pre_norm(x, g_pre, mod, shift=False, nxb=nxb)
    h2 = h.reshape(b * t, d)
    w_in = p['w_in']
    w_qig = _bf(jnp.concatenate([w_in[:, :e], w_in[:, 3 * e:]], axis=1))
    proj_f = _project(h2, _bf(w_in[:, e:3 * e]), out_dtype=F32).reshape(b, t, 2 * e)
    proj_qig = _project(h2, w_qig, out_dtype=BF16).reshape(b, t, 3 * e)
    gn = jnp.tile(p['g_norm'], e // HGRN_HEAD)[None]
    o0 = _hgrn_scan(proj_qig, proj_f, lb_logits, gn, None, reverse=False, last=False, nxb=nxb,
                    layer=layer)
    u = _hgrn_scan(proj_qig, proj_f, lb_logits, gn, o0, reverse=True, last=True, nxb=nxb,
                   layer=layer)
    return _out_residual(u, _bf(w_out), x, g_post, mod, nxb=nxb, nblk_out=nblk_out)


def _hy_pre_kernel(*refs, nxb, nblk):
    cur = refs[0:4]
    prv = refs[4:7]
    nxt = refs[7:10]
    w = refs[10:13]
    bias = refs[13:16]
    u_ref, g0_ref = refs[16], refs[17]
    i = pl.program_id(1)
    tb = cur[0].shape[1]
    row = lax.broadcasted_iota(jnp.int32, (tb, 1), 0)
    has_prev = jnp.logical_and(i != 0, i != nxb)
    has_next = jnp.logical_and(i != nxb - 1, i != nblk - 1)

    hr = prv[0].shape[1]

    def conv(j):
        x = cur[j][0].astype(F32)
        before = prv[j][0].astype(F32)[hr - 1:hr]
        after = nxt[j][0].astype(F32)[0:1]
        up = jnp.where(row == 0, jnp.where(has_prev, before, 0.0), pltpu.roll(x, 1, 0))
        dn = jnp.where(row == tb - 1, jnp.where(has_next, after, 0.0), pltpu.roll(x, tb - 1, 0))
        wj = w[j][...]
        return wj[0:1] * up + wj[1:2] * x + wj[2:3] * dn + bias[j][...]

    u_ref[0] = conv(2) * conv(1)
    g0_ref[0] = conv(0) * _silu(cur[3][0].astype(F32))


def _hy_pre(proj, conv_w, conv_b, *, nxb):
    b, t, e4 = proj.shape
    e = e4 // 4
    tb = TOKEN_BLOCK
    lg = 512
    ng = e // lg
    nblk = t // tb
    hr = 16
    hb = tb // hr
    nh = t // hr

    def col(off):
        return pl.BlockSpec((1, tb, lg), lambda bb, i, g: (bb, i, off * ng + g))

    def halo_prev(off):
        return pl.BlockSpec((1, hr, lg), lambda bb, i, g: (bb, jnp.maximum(i * hb - 1, 0), off * ng + g))

    def halo_next(off):
        return pl.BlockSpec((1, hr, lg), lambda bb, i, g: (bb, jnp.minimum((i + 1) * hb, nh - 1), off * ng + g))

    def wcol(rows, off):
        return pl.BlockSpec((rows, lg), lambda bb, i, g: (0, off * ng + g))

    in_specs = ([col(o) for o in range(4)] + [halo_prev(o) for o in range(3)]
                + [halo_next(o) for o in range(3)] + [wcol(3, o) for o in range(3)]
                + [wcol(1, o) for o in range(3)])
    args = [proj] * 10 + [conv_w] * 3 + [conv_b[None]] * 3
    act = pl.BlockSpec((1, tb, lg), lambda bb, i, g: (bb, i, g))
    return pl.pallas_call(
        functools.partial(_hy_pre_kernel, nxb=nxb, nblk=nblk),
        grid=(b, nblk, ng),
        in_specs=in_specs, out_specs=[act, act],
        out_shape=[jax.ShapeDtypeStruct((b, t, e), F32)] * 2,
        compiler_params=_cparams("parallel", "parallel", "parallel"),
        name="hyena_short_conv",
    )(*args)


def _hy_filter_kernel(fv_ref, w1_ref, b1_ref, w2_ref, b2_ref, w3_ref, b3_ref, sf_ref,
                      w4f_ref, w4b_ref, dl_ref, hf_ref, hb_ref, *, length):
    tl = hf_ref.shape[0]
    n = (pl.program_id(0) * tl + lax.broadcasted_iota(jnp.int32, (tl, 1), 0)).astype(F32)
    t = n * (1.0 / (length - 1))
    lane = lax.broadcasted_iota(jnp.int32, (tl, LANE), 1)
    nb = (HYENA_EMB - 1) // 2
    ang = (2.0 * math.pi / length) * n * fv_ref[...]
    z = jnp.where(lane == 0, t,
                  jnp.where(lane <= nb, jnp.cos(ang),
                            jnp.where(lane <= 2 * nb, -jnp.sin(ang), 0.0)))
    sf = sf_ref[...]
    hdn = jnp.sin(sf * (_dot(z, w1_ref[...], HIGHEST) + b1_ref[...]))
    hdn = jnp.sin(sf * (_dot(hdn, w2_ref[...], HIGHEST) + b2_ref[...]))
    hdn = jnp.sin(sf * (_dot(hdn, w3_ref[...], HIGHEST) + b3_ref[...]))
    window = jnp.exp(-t * dl_ref[...])
    hf_ref[...] = _dot(hdn, w4f_ref[...], HIGHEST) * window
    hb_ref[...] = _dot(hdn, w4b_ref[...], HIGHEST) * window


def _hy_filters(length, p, e):
    tl = min(length, 1024)
    lg = 512
    ng = e // lg
    nb = (HYENA_EMB - 1) // 2
    freqs = np.linspace(1e-4, nb - 1, nb, dtype=np.float32)
    fv = np.zeros((1, LANE), np.float32)
    fv[0, 1:1 + nb] = freqs
    fv[0, 1 + nb:1 + 2 * nb] = freqs
    deltas = np.abs(np.linspace(math.log(HYENA_TARGET) / HYENA_SLOW_DECAY,
                                math.log(HYENA_TARGET) / HYENA_FAST_DECAY, e, dtype=np.float32))[None]
    fw = HYENA_FILTER_WIDTH
    w1 = jnp.pad(p['f_w1'], ((0, LANE - HYENA_EMB), (0, 0)))
    full = lambda shape: pl.BlockSpec(shape, lambda r, g: (0, 0))
    in_specs = [full((1, LANE)), full((LANE, fw)), full((1, fw)), full((fw, fw)), full((1, fw)),
                full((fw, fw)), full((1, fw)), full((1, fw)),
                pl.BlockSpec((fw, lg), lambda r, g: (0, g)),
                pl.BlockSpec((fw, lg), lambda r, g: (0, ng + g)),
                pl.BlockSpec((1, lg), lambda r, g: (0, g))]
    out = pl.BlockSpec((tl, lg), lambda r, g: (r, g))
    return pl.pallas_call(
        functools.partial(_hy_filter_kernel, length=length),
        grid=(length // tl, ng),
        in_specs=in_specs, out_specs=[out, out],
        out_shape=[jax.ShapeDtypeStruct((length, e), F32)] * 2,
        compiler_params=_cparams("parallel", "parallel"),
        name="hyena_filters",
    )(jnp.asarray(fv), w1, p['f_b1'][None], p['f_w2'], p['f_b2'][None], p['f_w3'], p['f_b3'][None],
      p['sin_freq'][None], p['f_w4'], p['f_w4'], jnp.asarray(deltas))


def _cmul(x, h, half):
    xr, xi = x[:half], x[half:]
    hr, hi = h[:half], h[half:]
    return jnp.concatenate([xr * hr - xi * hi, xr * hi + xi * hr], axis=0)


def _conj(x, half):
    return jnp.concatenate([x[:half], -x[half:]], axis=0)


def _tdot(tab, x, idx=None):
    get = (lambda ref: ref[...]) if idx is None else (lambda ref: ref[idx])
    xh = x.astype(BF16)
    out = _dot(get(tab[0]), xh)
    if len(tab) == 2:
        xl = (x - xh.astype(F32)).astype(BF16)
        out = out + _dot(get(tab[0]), xl) + _dot(get(tab[1]), xh)
    return out


def _split_tables(refs, count):
    per = len(FFT_TABLE_PARTS)
    return [refs[i * per:(i + 1) * per] for i in range(count)], refs[count * per:]


def _long_conv_direct_kernel(u_ref, g0_ref, hf_ref, hb_ref, fb_ref, *rest):
    (fd_t, fi_t), (o_ref, h_s) = _split_tables(rest, 2)
    half = fd_t[0].shape[0] // 2

    @pl.when(pl.program_id(1) == 0)
    def _():
        h_s[...] = _tdot(fd_t, hf_ref[...]) + _conj(_tdot(fd_t, hb_ref[...]), half)

    u = u_ref[0]
    y = _tdot(fi_t, _cmul(_tdot(fd_t, u), h_s[...], half))
    o_ref[0] = (y + u * fb_ref[...]) * g0_ref[0]


def _long_conv_kernel(u_ref, g0_ref, hf_ref, hb_ref, fb_ref, *rest):
    (t1_t, t1t_t, f2_t, f2t_t), (o_ref, x_s, a_s, h_s) = _split_tables(rest, 4)
    n2, rows1, n1h = t1_t[0].shape
    n1 = rows1 // 2
    un = FFT_UNROLL
    xp = _odd_pitch(n2)
    ap = _odd_pitch(rows1)

    def a_block(m2):
        return pl.ds(pl.multiple_of(m2 * ap, 8), rows1)

    def stage1(src):
        def copy(m1, carry):
            x_s[pl.ds(pl.multiple_of(m1 * xp, 8), n2), :] = src(pl.ds(pl.multiple_of(m1 * n2, n2), n2))
            return carry
        lax.fori_loop(0, n1h, copy, 0, unroll=4)

        def body(j, carry):
            m2s = [j * un + q for q in range(un)]
            xs = [x_s[pl.ds(m2, n1h, stride=xp), :] for m2 in m2s]
            outs = [_tdot(t1_t, x, m2) for m2, x in zip(m2s, xs)]
            for m2, o in zip(m2s, outs):
                a_s[a_block(m2), :] = o
            return carry
        lax.fori_loop(0, n2 // un, body, 0)

    def stage2(k1s):
        zs = [jnp.concatenate([a_s[pl.ds(k1, n2, stride=ap), :],
                               a_s[pl.ds(n1 + k1, n2, stride=ap), :]], axis=0) for k1 in k1s]
        return [_tdot(f2_t, z) for z in zs]

    def spec_rows(k1):
        return pl.ds(pl.multiple_of(k1 * 2 * n2, 2 * n2), 2 * n2)

    @pl.when(pl.program_id(1) == 0)
    def _():
        stage1(lambda rows: hf_ref[rows, :])

        def spec_f(j, carry):
            k1s = [j * un + q for q in range(un)]
            for k1, x in zip(k1s, stage2(k1s)):
                h_s[spec_rows(k1), :] = x
            return carry
        lax.fori_loop(0, n1 // un, spec_f, 0)
        stage1(lambda rows: hb_ref[rows, :])

        def spec_b(j, carry):
            k1s = [j * un + q for q in range(un)]
            for k1, x in zip(k1s, stage2(k1s)):
                h_s[spec_rows(k1), :] = h_s[spec_rows(k1), :] + _conj(x, n2)
            return carry
        lax.fori_loop(0, n1 // un, spec_b, 0)

    stage1(lambda rows: u_ref[0, rows, :])

    def mid(j, carry):
        k1s = [j * un + q for q in range(un)]
        ys = [_cmul(x, h_s[spec_rows(k1), :], n2) for k1, x in zip(k1s, stage2(k1s))]
        zs = [_tdot(f2t_t, y) for y in ys]
        for k1, z in zip(k1s, zs):
            a_s[pl.ds(k1, n2, stride=ap), :] = z[:n2]
            a_s[pl.ds(n1 + k1, n2, stride=ap), :] = z[n2:]
        return carry
    lax.fori_loop(0, n1 // un, mid, 0)

    def inv1(j, carry):
        m2s = [j * un + q for q in range(un)]
        ys = [_tdot(t1t_t, a_s[a_block(m2), :], m2) for m2 in m2s]
        for m2, y in zip(m2s, ys):
            x_s[pl.ds(m2, n1h, stride=xp), :] = y
        return carry
    lax.fori_loop(0, n2 // un, inv1, 0)

    fb = fb_ref[...]

    def finish(m1, carry):
        rows = pl.ds(pl.multiple_of(m1 * n2, n2), n2)
        y = x_s[pl.ds(pl.multiple_of(m1 * xp, 8), n2), :]
        o_ref[0, rows, :] = (y + u_ref[0, rows, :] * fb) * g0_ref[0, rows, :]
        return carry
    lax.fori_loop(0, n1h, finish, 0, unroll=4)


def _odd_pitch(rows):
    return rows if (rows // 8) % 2 else rows + 8


def _fft_tables(length):
    n = 2 * length
    n2 = FFT_N2
    n1 = n // n2
    n1h = n1 // 2
    k1 = np.arange(n1, dtype=np.float64)[:, None]
    m1 = np.arange(n1h, dtype=np.float64)[None, :]
    t1 = []
    for m2 in range(n2):
        phi = 2.0 * np.pi * (k1 * m1 / n1 + k1 * m2 / n)
        t1.append(np.concatenate([np.cos(phi), -np.sin(phi)], axis=0))
    t1 = np.stack(t1)
    t1t = np.transpose(t1, (0, 2, 1)) / n
    k2 = np.arange(n2, dtype=np.float64)[:, None]
    m2 = np.arange(n2, dtype=np.float64)[None, :]
    th = 2.0 * np.pi * k2 * m2 / n2
    mr, mi = np.cos(th), -np.sin(th)
    f2 = np.block([[mr, -mi], [mi, mr]])
    return _hi_lo(t1) + _hi_lo(t1t) + _hi_lo(f2) + _hi_lo(f2.T)


def _hi_lo(a):
    a32 = jnp.asarray(a.astype(np.float32))
    hi = a32.astype(BF16)
    parts = {"hi": hi, "lo": (a32 - hi.astype(F32)).astype(BF16)}
    return [parts[name] for name in FFT_TABLE_PARTS]


def _dft_tables(length):
    n = 2 * length
    k = np.arange(n, dtype=np.float64)[:, None]
    m = np.arange(length, dtype=np.float64)[None, :]
    phi = 2.0 * np.pi * k * m / n
    fd = np.concatenate([np.cos(phi), -np.sin(phi)], axis=0)
    return _hi_lo(fd) + _hi_lo(fd.T / n)


def _long_conv(u, g0, hf, hb, fbias, *, length, blk_index):
    b, t, e = u.shape
    lg = FFT_LANES
    ng = e // lg
    seq_spec = pl.BlockSpec((1, length, lg), lambda g, bb: (bb, blk_index, g))
    filt_spec = pl.BlockSpec((length, lg), lambda g, bb: (0, g))
    row_spec = pl.BlockSpec((1, lg), lambda g, bb: (0, g))
    out_spec = pl.BlockSpec((1, length, lg), lambda g, bb: (bb, 0, g))
    common = dict(
        grid=(ng, b),
        out_specs=out_spec,
        out_shape=jax.ShapeDtypeStruct((b, length, e), F32),
        compiler_params=_cparams("parallel", "arbitrary"),
    )
    const = lambda a: pl.BlockSpec(a.shape, lambda g, bb: (0,) * a.ndim)
    if length <= TOKEN_BLOCK:
        tabs = _dft_tables(length)
        return pl.pallas_call(
            _long_conv_direct_kernel,
            in_specs=[seq_spec, seq_spec, filt_spec, filt_spec, row_spec] + [const(a) for a in tabs],
            scratch_shapes=[pltpu.VMEM((4 * length, lg), F32)],
            name="hyena_long_conv_ctx", **common,
        )(u, g0, hf, hb, fbias, *tabs)
    tabs = _fft_tables(length)
    n2 = FFT_N2
    n1 = 2 * length // n2
    return pl.pallas_call(
        _long_conv_kernel,
        in_specs=[seq_spec, seq_spec, filt_spec, filt_spec, row_spec] + [const(a) for a in tabs],
        scratch_shapes=[pltpu.VMEM((n1 // 2 * _odd_pitch(n2), lg), F32),
                        pltpu.VMEM((n2 * _odd_pitch(2 * n1), lg), F32),
                        pltpu.VMEM((4 * length, lg), F32)],
        name="hyena_long_conv", **common,
    )(u, g0, hf, hb, fbias, *tabs)


def _hyena_layer(x, mod, g_pre, g_post, w_out, p, *, nxb, nblk_out):
    b, t, d = x.shape
    e = p['filter_bias'].shape[0]
    seq = nxb * TOKEN_BLOCK
    ctx_len = t - seq
    h = _pre_norm(x, g_pre, mod, shift=False, nxb=nxb)
    proj = _project(h.reshape(b * t, d), _bf(p['w_in']), out_dtype=BF16).reshape(b, t, 4 * e)
    u, g0 = _hy_pre(proj, p['conv_w'], p['conv_b'], nxb=nxb)
    fbias = p['filter_bias'][None]
    hf, hb = _hy_filters(seq, p, e)
    yx = _long_conv(u, g0, hf, hb, fbias, length=seq, blk_index=0)
    hf, hb = _hy_filters(ctx_len, p, e)
    yc = _long_conv(u, g0, hf, hb, fbias, length=ctx_len, blk_index=seq // ctx_len)
    y = jnp.concatenate([yx, yc], axis=1)
    return _out_residual(y, _bf(w_out), x, g_post, mod, nxb=nxb, nblk_out=nblk_out)


def _modulation(c, c_ctx, ada_w, ada_b):
    b, d = c.shape
    depth = ada_w.shape[0]
    rows = -(-(b + 1) // 8) * 8
    cstack = jnp.zeros((rows, d), F32).at[:b].set(c).at[b].set(c_ctx)
    ada = _ada_all(cstack, ada_w, ada_b).reshape(depth, rows, 3, d)
    lat = ada[:, :b]
    cx = jnp.broadcast_to(ada[:, b:b + 1], lat.shape)
    return jnp.stack([lat, cx], axis=2)


def kernel(x, c, ctx, c_ctx, ada_w, ada_b, norm_pre, norm_post, w_out,
           l0_w_in, l0_mu, l0_w0, l0_w2, l0_a0, l0_a2, l0_k_k, l0_k_a, l0_r_k, l0_ln_w, l0_ln_b,
           l1_w_in, l1_conv_w, l1_conv_b, l1_f_w1, l1_f_b1, l1_f_w2, l1_f_b2, l1_f_w3, l1_f_b3,
           l1_f_w4, l1_sin_freq, l1_filter_bias,
           l2_w_in, l2_g_norm, hgrn_lb_logits,
           l3_w_in, l3_mu, l3_w0, l3_w2, l3_a0, l3_a2, l3_k_k, l3_k_a, l3_r_k, l3_ln_w, l3_ln_b,
           l3_v0, l3_v2):
    rwkv0 = dict(w_in=l0_w_in, mu=l0_mu, w0=l0_w0, w2=l0_w2, a0=l0_a0, a2=l0_a2, k_k=l0_k_k,
                 k_a=l0_k_a, r_k=l0_r_k, ln_w=l0_ln_w, ln_b=l0_ln_b)
    hyena1 = dict(w_in=l1_w_in, conv_w=l1_conv_w, conv_b=l1_conv_b, f_w1=l1_f_w1, f_b1=l1_f_b1,
                  f_w2=l1_f_w2, f_b2=l1_f_b2, f_w3=l1_f_w3, f_b3=l1_f_b3, f_w4=l1_f_w4,
                  sin_freq=l1_sin_freq, filter_bias=l1_filter_bias)
    hgrn2 = dict(w_in=l2_w_in, g_norm=l2_g_norm)
    rwkv3 = dict(w_in=l3_w_in, mu=l3_mu, w0=l3_w0, w2=l3_w2, a0=l3_a0, a2=l3_a2, k_k=l3_k_k,
                 k_a=l3_k_a, r_k=l3_r_k, ln_w=l3_ln_w, ln_b=l3_ln_b, v0=l3_v0, v2=l3_v2)
    b, seq, d = x.shape
    assert ctx.shape[1] == TOKEN_BLOCK and seq % TOKEN_BLOCK == 0
    nxb = seq // TOKEN_BLOCK
    mods = _modulation(c, c_ctx, ada_w, ada_b)
    xa = jnp.concatenate([x, ctx], axis=1)
    xa, proj0 = _rwkv_layer(xa, mods[0], norm_pre[0][None], norm_post[0][None], w_out[0], rwkv0,
                            None, nxb=nxb, nblk_out=nxb + 1)
    xa = _hyena_layer(xa, mods[1], norm_pre[1][None], norm_post[1][None], w_out[1], hyena1,
                      nxb=nxb, nblk_out=nxb + 1)
    xa = _hgrn_layer(xa, mods[2], norm_pre[2][None], norm_post[2][None], w_out[2], hgrn2,
                     hgrn_lb_logits, nxb=nxb, nblk_out=nxb + 1, layer=2)
    xa, _ = _rwkv_layer(xa, mods[3], norm_pre[3][None], norm_post[3][None], w_out[3], rwkv3,
                        proj0, nxb=nxb, nblk_out=nxb)
    return xa
```

```python
import functools
import math

import jax
import jax.numpy as jnp
import numpy as np
from jax import lax
from jax.experimental import pallas as pl
from jax.experimental.pallas import tpu as pltpu

F32 = jnp.float32
BF16 = jnp.bfloat16
HIGHEST = lax.Precision.HIGHEST

NORM_EPS = 1e-6
GRID_W = 64
TOKEN_BLOCK = 256
LANE = 128
VMEM_LIMIT = 56 * 1024 * 1024

RWKV_HEAD = 64
RWKV_CHUNK = 64
RWKV_LANES = 512
RWKV_GN_EPS = 64e-5
TAIL_WIDTH = 384

HGRN_HEAD = 128
HGRN_SUB = 16
HGRN_LANES = 512
HGRN_UNROLL = 16

HYENA_EMB = 33
HYENA_FILTER_WIDTH = 64
HYENA_FAST_DECAY = 0.3
HYENA_SLOW_DECAY = 1.5
HYENA_TARGET = 1e-2
FFT_N2 = 64
FFT_LANES = 128
FFT_TABLE_PARTS = ("hi",)
FFT_UNROLL = 16


def _cparams(*sem):
    return pltpu.CompilerParams(dimension_semantics=sem, vmem_limit_bytes=VMEM_LIMIT)


def _dot(a, b, precision=None):
    return jnp.dot(a, b, preferred_element_type=F32, precision=precision)


def _dot_nt(a, b, precision=None):
    return lax.dot_general(a, b, (((1,), (1,)), ((), ())),
                           preferred_element_type=F32, precision=precision)


def _dot_tn(a, b):
    return lax.dot_general(a, b, (((0,), (0,)), ((), ())), preferred_element_type=F32)


def _bf(x):
    return x.astype(BF16)


def _split_dot(x, w_bf16):
    hi = x.astype(BF16)
    lo = (x - hi.astype(F32)).astype(BF16)
    return _dot(hi, w_bf16) + _dot(lo, w_bf16)


def _mask_dot(m_bf16, x):
    hi = x.astype(BF16)
    lo = (x - hi.astype(F32)).astype(BF16)
    return _dot(m_bf16, hi) + _dot(m_bf16, lo)


def _tree_sum(terms):
    terms = list(terms)
    while len(terms) > 1:
        terms = [a + b for a, b in zip(terms[0::2], terms[1::2])] + (terms[-1:] if len(terms) % 2 else [])
    return terms[0]


def _sigmoid(x):
    return 1.0 / (1.0 + jnp.exp(-x))


def _silu(x):
    return x * _sigmoid(x)


def _ada_kernel(c_ref, w_ref, b_ref, o_ref):
    o_ref[0] = _dot(_silu(c_ref[...]), w_ref[0], HIGHEST) + b_ref[0]


def _ada_all(cstack, ada_w, ada_b):
    depth, d, d3 = ada_w.shape
    rows = cstack.shape[0]
    nt = d3 // d
    return pl.pallas_call(
        _ada_kernel,
        grid=(depth, nt),
        in_specs=[
            pl.BlockSpec((rows, d), lambda l, j: (0, 0)),
            pl.BlockSpec((1, d, d), lambda l, j: (l, 0, j)),
            pl.BlockSpec((1, 1, d), lambda l, j: (l, 0, j)),
        ],
        out_specs=pl.BlockSpec((1, rows, d), lambda l, j: (l, 0, j)),
        out_shape=jax.ShapeDtypeStruct((depth, rows, d3), F32),
        compiler_params=_cparams("parallel", "parallel"),
        name="adaln",
    )(cstack, ada_w, ada_b.reshape(depth, 1, d3))


def _norm_kernel(*refs, shift, nxb):
    if shift:
        x_ref, xp_ref, xn_ref, g_ref, mod_ref, h_ref, d_ref = refs
    else:
        x_ref, g_ref, mod_ref, h_ref = refs
    i = pl.program_id(1)
    g = g_ref[...]
    shift_v = mod_ref[0, 0, 0:1, :]
    scale1p = 1.0 + mod_ref[0, 0, 1:2, :]

    def nrm(x):
        ms = jnp.mean(x * x, axis=-1, keepdims=True)
        return x * lax.rsqrt(ms + NORM_EPS) * g * scale1p + shift_v

    h = nrm(x_ref[0])
    h_ref[0] = h.astype(BF16)
    if not shift:
        return
    tb, d = h.shape
    q = d // 4
    row = lax.broadcasted_iota(jnp.int32, (tb, 1), 0)

    @pl.when(i < nxb)
    def _():
        col = row % GRID_W
        left = jnp.where(col > 0, pltpu.roll(h[:, 0:q], 1, 0), 0.0)
        right = jnp.where(col < GRID_W - 1, pltpu.roll(h[:, q:2 * q], tb - 1, 0), 0.0)
        hp = nrm(xp_ref[0])[:, 2 * q:3 * q]
        hn = nrm(xn_ref[0])[:, 3 * q:]
        hp = jnp.where(i > 0, hp, 0.0)
        hn = jnp.where(i < nxb - 1, hn, 0.0)
        up = jnp.concatenate([hp, h[:tb - GRID_W, 2 * q:3 * q]], axis=0)
        down = jnp.concatenate([h[GRID_W:, 3 * q:], hn], axis=0)
        hs = jnp.concatenate([left, right, up, down], axis=-1)
        d_ref[0] = (hs - h).astype(BF16)

    @pl.when(i >= nxb)
    def _():
        half = d // 2
        prev = jnp.where(row > 0, pltpu.roll(h[:, :half], 1, 0), 0.0)
        nxt = jnp.where(row < tb - 1, pltpu.roll(h[:, half:], tb - 1, 0), 0.0)
        hs = jnp.concatenate([prev, nxt], axis=-1)
        d_ref[0] = (hs - h).astype(BF16)


def _pre_norm(x, g, mod, *, shift, nxb):
    b, t, d = x.shape
    tb = TOKEN_BLOCK
    nblk = t // tb
    hb = tb // GRID_W
    nhalo = t // GRID_W
    seg = lambda i: jnp.where(i < nxb, 0, 1)
    x_spec = pl.BlockSpec((1, tb, d), lambda bb, i: (bb, i, 0))
    g_spec = pl.BlockSpec((1, d), lambda bb, i: (0, 0))
    mod_spec = pl.BlockSpec((1, 1, 3, d), lambda bb, i: (bb, seg(i), 0, 0))
    out_spec = pl.BlockSpec((1, tb, d), lambda bb, i: (bb, i, 0))
    if shift:
        in_specs = [
            x_spec,
            pl.BlockSpec((1, GRID_W, d), lambda bb, i: (bb, jnp.maximum(i * hb - 1, 0), 0)),
            pl.BlockSpec((1, GRID_W, d), lambda bb, i: (bb, jnp.minimum((i + 1) * hb, nhalo - 1), 0)),
            g_spec, mod_spec,
        ]
        args = (x, x, x, g, mod)
        out_specs = [out_spec, out_spec]
        out_shape = [jax.ShapeDtypeStruct((b, t, d), BF16)] * 2
    else:
        in_specs = [x_spec, g_spec, mod_spec]
        args = (x, g, mod)
        out_specs = out_spec
        out_shape = jax.ShapeDtypeStruct((b, t, d), BF16)
    return pl.pallas_call(
        functools.partial(_norm_kernel, shift=shift, nxb=nxb),
        grid=(b, nblk),
        in_specs=in_specs,
        out_specs=out_specs,
        out_shape=out_shape,
        compiler_params=_cparams("parallel", "parallel"),
        name="pre_norm_shift" if shift else "pre_norm",
    )(*args)


def _row_tile(m):
    for tm in (1024, 512, 256):
        if m % tm == 0:
            return tm
    raise ValueError(f"token count {m} is not a multiple of {TOKEN_BLOCK}")


def _proj_lerp_kernel(h_ref, d_ref, mu_ref, w_ref, o_ref, lhs_ref, *, tiles_per_group):
    j = pl.program_id(1)

    @pl.when(j % tiles_per_group == 0)
    def _():
        lhs_ref[...] = (h_ref[...].astype(F32) + mu_ref[0] * d_ref[...].astype(F32)).astype(BF16)

    o_ref[...] = _dot(lhs_ref[...], w_ref[...]).astype(o_ref.dtype)


def _proj_kernel(h_ref, w_ref, o_ref):
    o_ref[...] = _dot(h_ref[...], w_ref[...]).astype(o_ref.dtype)


def _project(h2, w_bf16, *, out_dtype, d2=None, mu=None, group_width=None):
    m, d = h2.shape
    n = w_bf16.shape[1]
    tm = _row_tile(m)
    tn = 1024
    lhs_spec = pl.BlockSpec((tm, d), lambda i, j: (i, 0))
    w_spec = pl.BlockSpec((d, tn), lambda i, j: (0, j))
    o_spec = pl.BlockSpec((tm, tn), lambda i, j: (i, j))
    if d2 is None:
        return pl.pallas_call(
            _proj_kernel, grid=(m // tm, n // tn),
            in_specs=[lhs_spec, w_spec], out_specs=o_spec,
            out_shape=jax.ShapeDtypeStruct((m, n), out_dtype),
            compiler_params=_cparams("parallel", "parallel"),
            name="project",
        )(h2, w_bf16)
    tpg = group_width // tn
    return pl.pallas_call(
        functools.partial(_proj_lerp_kernel, tiles_per_group=tpg),
        grid=(m // tm, n // tn),
        in_specs=[lhs_spec, lhs_spec,
                  pl.BlockSpec((1, 1, d), lambda i, j: (j // tpg, 0, 0)),
                  w_spec],
        out_specs=o_spec,
        out_shape=jax.ShapeDtypeStruct((m, n), out_dtype),
        scratch_shapes=[pltpu.VMEM((tm, d), BF16)],
        compiler_params=_cparams("parallel", "arbitrary"),
        name="project_lerp",
    )(h2, d2, mu, w_bf16)


def _tail_kernel(h_ref, d_ref, w_ref, mu_ref, o_ref):
    w = w_ref[...]
    o_ref[...] = _dot(h_ref[...], _bf(w)) + _dot(d_ref[...], _bf(w * mu_ref[...]))


def _project_tail(h2, d2, w_tail, mu_cols):
    m, d = h2.shape
    n = w_tail.shape[1]
    tm = _row_tile(m)
    lhs_spec = pl.BlockSpec((tm, d), lambda i: (i, 0))
    w_spec = pl.BlockSpec((d, n), lambda i: (0, 0))
    return pl.pallas_call(
        _tail_kernel, grid=(m // tm,),
        in_specs=[lhs_spec, lhs_spec, w_spec, w_spec],
        out_specs=pl.BlockSpec((tm, n), lambda i: (i, 0)),
        out_shape=jax.ShapeDtypeStruct((m, n), F32),
        compiler_params=_cparams("parallel"),
        name="project_tail",
    )(h2, d2, w_tail, mu_cols)


def _out_kernel(u_ref, w_ref, x_ref, g_ref, mod_ref, o_ref):
    y = _dot(_bf(u_ref[0]), w_ref[...])
    ms = jnp.mean(y * y, axis=-1, keepdims=True)
    yn = y * lax.rsqrt(ms + NORM_EPS) * g_ref[...]
    o_ref[0] = x_ref[0] + yn * mod_ref[0, 0, 2:3, :]


def _out_residual(u, w_bf16, x, g, mod, *, nxb, nblk_out):
    b, t, e = u.shape
    d = x.shape[-1]
    tb = TOKEN_BLOCK
    seg = lambda i: jnp.where(i < nxb, 0, 1)
    return pl.pallas_call(
        _out_kernel, grid=(b, nblk_out),
        in_specs=[
            pl.BlockSpec((1, tb, e), lambda bb, i: (bb, i, 0)),
            pl.BlockSpec((e, d), lambda bb, i: (0, 0)),
            pl.BlockSpec((1, tb, d), lambda bb, i: (bb, i, 0)),
            pl.BlockSpec((1, d), lambda bb, i: (0, 0)),
            pl.BlockSpec((1, 1, 3, d), lambda bb, i: (bb, seg(i), 0, 0)),
        ],
        out_specs=pl.BlockSpec((1, tb, d), lambda bb, i: (bb, i, 0)),
        out_shape=jax.ShapeDtypeStruct((b, nblk_out * tb, d), F32),
        compiler_params=_cparams("parallel", "parallel"),
        name="out_residual",
    )(u, w_bf16, x, g, mod)


def _rwkv_scan_kernel(*refs, reverse, vres, last):
    it = iter(refs)
    r_ref, k_ref, v_ref, lo_ref, w2_ref, a2_ref = (next(it) for _ in range(6))
    v2_ref = next(it) if vres else None
    vec_ref = next(it)
    vf_ref = next(it) if vres else None
    if last:
        gate_ref, y0_ref, b0_ref, u_ref = (next(it) for _ in range(4))
    else:
        yo_ref, bo_ref = next(it), next(it)
    s_ref, y_s, b_s = it

    i = pl.program_id(2)
    tb, lg = y_s.shape
    npair = lg // LANE
    c = RWKV_CHUNK
    nch = tb // c
    hd = RWKV_HEAD

    @pl.when(i == 0)
    def _():
        s_ref[...] = jnp.zeros_like(s_ref)

    vec = vec_ref[...]
    lo = lo_ref[0]
    r = r_ref[0].astype(F32)
    k = k_ref[0].astype(F32)
    v = v_ref[0].astype(F32)
    zt = vec[0:1] + _dot(_bf(jnp.tanh(lo[:, :LANE])), w2_ref[...])
    logw = -(math.exp(-0.5) * math.log2(math.e)) * _sigmoid(zt)
    a = _sigmoid(vec[1:2] + _dot(_bf(lo[:, LANE:2 * LANE]), a2_ref[...]))
    if vres:
        v = v + (vf_ref[0].astype(F32) - v) * _sigmoid(vec[5:6] + _dot(_bf(lo[:, 2 * LANE:]), v2_ref[...]))
    kdir = k * (1.0 + (a - 1.0) * vec[3:4])
    kkr = k * vec[2:3]
    bd = (lax.broadcasted_iota(jnp.int32, (LANE, LANE), 0) // hd
          == lax.broadcasted_iota(jnp.int32, (LANE, LANE), 1) // hd)
    ones_bd = jnp.where(bd, 1.0, 0.0).astype(BF16)

    def segsum(x):
        return jnp.concatenate(
            [_dot(_bf(x[:, p * LANE:(p + 1) * LANE]), ones_bd) for p in range(npair)], axis=-1)

    kk = kkr / jnp.maximum(jnp.sqrt(segsum(kkr * kkr)), 1e-12)
    b_s[...] = segsum(r * kdir * vec[4:5]) * v
    alpha = -kk
    beta = kk * a

    t1 = lax.broadcasted_iota(jnp.int32, (c, c), 0)
    s1 = lax.broadcasted_iota(jnp.int32, (c, c), 1)
    t2 = lax.broadcasted_iota(jnp.int32, (c, LANE), 0)
    lane2 = lax.broadcasted_iota(jnp.int32, (c, LANE), 1)
    s2 = lane2 % c
    if reverse:
        incl1, incl2, strict2 = s1 >= t1, s2 >= t2, s2 > t2
    else:
        incl1, incl2, strict2 = s1 <= t1, s2 <= t2, s2 < t2
    tri = jnp.where(incl1, 1.0, 0.0).astype(BF16)
    low = lane2 < hd
    high = jnp.logical_not(low)
    low_x = lax.broadcasted_iota(jnp.int32, (2 * c, LANE), 1) < hd
    strict_lo, strict_hi = strict2 & low, strict2 & high
    incl_lo, incl_hi = incl2 & low, incl2 & high
    eye2 = jnp.where(lax.broadcasted_iota(jnp.int32, (LANE, LANE), 0)
                     == lax.broadcasted_iota(jnp.int32, (LANE, LANE), 1), 1.0, 0.0)
    zeros_cv = jnp.zeros((c, LANE), F32)

    def stack(top, bot):
        return jnp.concatenate([top, bot], axis=0)

    def fold(z):
        return z[:c] + z[c:]

    chains = {}
    rounds = int(math.log2(c)) - 1

    def st_decay(ci):
        rows = slice(ci * c, (ci + 1) * c)
        lw = logw[rows]
        lc = _mask_dot(tri, lw)
        ltot = lc[0:1] if reverse else lc[c - 1:c]
        p_inv = jnp.exp2(-lc)
        p_all = jnp.exp2(ltot)
        ab = alpha[rows] * jnp.exp2(lc - lw)
        rb = r[rows] * jnp.exp2(lc)
        bt = beta[rows] * p_inv
        kt = kdir[rows] * p_inv
        bh = bt * p_all
        kh = kt * p_all
        chains[ci] = []
        for p in range(npair):
            sl = slice(p * LANE, (p + 1) * LANE)
            chains[ci].append(dict(p=p, ab=ab[:, sl], rb=rb[:, sl], bt=bt[:, sl], kt=kt[:, sl],
                                   bh=bh[:, sl], kh=kh[:, sl], v=v[rows, sl], p_all=p_all[:, sl]))

    def st_gram(ci):
        for d in chains[ci]:
            y01 = jnp.concatenate([jnp.where(low_x, stack(d['bt'], d['kt']), 0.0),
                                   jnp.where(low_x, 0.0, stack(d['kt'], d['bt']))], axis=0)
            d['g'] = _dot_nt(_bf(stack(d['ab'], d['rb'])), _bf(y01))

    def st_blocks(ci):
        for d in chains[ci]:
            g0t, g0b = d['g'][:c, :LANE], d['g'][c:, :LANE]
            g1t, g1b = d['g'][:c, LANE:], d['g'][c:, LANE:]
            d['a'] = stack(jnp.where(strict_lo, g0t, 0.0), jnp.where(strict_hi, g1t, 0.0))
            arb = stack(jnp.where(incl_lo, g0b, 0.0), jnp.where(incl_hi, g1b, 0.0))
            ark = stack(jnp.where(incl_hi, g0b, 0.0), jnp.where(incl_lo, g1b, 0.0))
            d['arbk'] = _bf(jnp.concatenate([arb, ark], axis=1))
            ak = stack(jnp.where(strict_hi, g0t, 0.0), jnp.where(strict_lo, g1t, 0.0))
            d['vx'] = stack(jnp.where(high, d['v'], 0.0), jnp.where(low, d['v'], 0.0))
            d['w'] = _dot(_bf(ak), _bf(d['vx']))
            del d['g']

    def st_square(ci):
        for d in chains[ci]:
            d['tm'] = eye2 + d['a']
            apb = _bf(d['a'])
            d['a'] = _dot(apb, apb)

    def st_round(ci):
        for d in chains[ci]:
            pt = _dot(_bf(d['a']), _bf(jnp.concatenate([d['a'], d['tm']], axis=1)))
            d['a'] = pt[:, :LANE]
            d['tm'] = d['tm'] + pt[:, LANE:]

    def st_last_round(ci):
        for d in chains[ci]:
            d['tm'] = d['tm'] + _dot(_bf(d['a']), _bf(d['tm']))

    def st_apply(ci):
        for d in chains[ci]:
            ab_st = stack(jnp.where(low, d['ab'], 0.0), jnp.where(high, d['ab'], 0.0))
            d['tz'] = _dot(_bf(d['tm']), _bf(jnp.concatenate([ab_st, d['w']], axis=1)))

    def st_out(ci):
        for d in chains[ci]:
            lower = jnp.concatenate([jnp.zeros((2 * c, LANE), F32), d['vx']], axis=1)
            yz = _dot(d['arbk'], _bf(stack(d['tz'], lower)))
            ta, tw = fold(d['tz'][:, :LANE]), fold(d['tz'][:, LANE:])
            d['ra'] = _bf(d['rb'] + fold(yz[:, :LANE]))
            d['yw'] = fold(yz[:, LANE:])
            lhs = stack(jnp.concatenate([ta, tw], axis=1), jnp.concatenate([zeros_cv, d['v']], axis=1))
            mn = _dot_tn(_bf(lhs), _bf(stack(d['bh'], d['kh'])))
            d['m'] = _bf(jnp.where(bd, mn[:LANE], 0.0))
            d['n'] = jnp.where(bd, mn[LANE:], 0.0)

    state = [s_ref[p] for p in range(npair)]

    def st_recur(ci):
        for d in chains[ci]:
            p = d['p']
            sp = state[p]
            spb = _bf(sp)
            y_s[ci * c:(ci + 1) * c, p * LANE:(p + 1) * LANE] = _dot_nt(d['ra'], spb) + d['yw']
            state[p] = sp * d['p_all'] + _dot(spb, d['m']) + d['n']

    stages = ([st_decay, st_gram, st_blocks, st_square] + [st_round] * (rounds - 1)
              + [st_last_round, st_apply, st_out])
    for stage in stages:
        for ci in range(nch):
            stage(ci)
    for ci in (reversed(range(nch)) if reverse else range(nch)):
        st_recur(ci)
    for p in range(npair):
        s_ref[p] = state[p]

    bonus = b_s[...]
    if not last:
        yo_ref[0] = y_s[...]
        bo_ref[0] = bonus
    else:
        yy = y0_ref[0] + y_s[...]
        mean = segsum(yy) * (1.0 / hd)
        yc = yy - mean
        var = segsum(yc * yc) * (1.0 / hd)
        yn = yc * lax.rsqrt(var + RWKV_GN_EPS) * vec[6:7] + vec[7:8]
        u_ref[0] = ((yn + b0_ref[0] + bonus) * _silu(gate_ref[0].astype(F32))).astype(BF16)


def _rwkv_scan(proj, tail, w2p, a2p, v2p, vec, proj0, y0, b0, *, reverse, vres, last, nxb):
    b, t, e4 = proj.shape
    e = e4 // 4
    tb = TOKEN_BLOCK
    lg = RWKV_LANES
    ng = e // lg
    nblk = t // tb
    assert nblk == nxb + 1, "the context prefix must be exactly one token block"
    if reverse:
        blk = lambda i: jnp.where(i == 0, nxb, nxb - i)
    else:
        blk = lambda i: jnp.where(i == 0, nxb, i - 1)

    def col(off):
        return pl.BlockSpec((1, tb, lg), lambda bb, g, i: (bb, blk(i), off * ng + g))

    wspec = pl.BlockSpec((LANE, lg), lambda bb, g, i: (0, g))
    in_specs = [col(0), col(1), col(2),
                pl.BlockSpec((1, tb, TAIL_WIDTH), lambda bb, g, i: (bb, blk(i), 0)),
                wspec, wspec]
    args = [proj, proj, proj, tail, w2p, a2p]
    if vres:
        in_specs.append(wspec)
        args.append(v2p)
    in_specs.append(pl.BlockSpec((8, lg), lambda bb, g, i: (0, g)))
    args.append(vec)
    if vres:
        in_specs.append(col(2))
        args.append(proj0)
    act = pl.BlockSpec((1, tb, lg), lambda bb, g, i: (bb, blk(i), g))
    if last:
        in_specs += [col(3), act, act]
        args += [proj, y0, b0]
        out_specs = act
        out_shape = jax.ShapeDtypeStruct((b, t, e), BF16)
    else:
        out_specs = [act, act]
        out_shape = [jax.ShapeDtypeStruct((b, t, e), F32)] * 2
    scratch = [pltpu.VMEM((lg // LANE, LANE, LANE), F32)] + [pltpu.VMEM((tb, lg), F32)] * 2
    return pl.pallas_call(
        functools.partial(_rwkv_scan_kernel, reverse=reverse, vres=vres, last=last),
        grid=(b, ng, nblk),
        in_specs=in_specs, out_specs=out_specs, out_shape=out_shape,
        scratch_shapes=scratch,
        compiler_params=_cparams("parallel", "parallel", "arbitrary"),
        name="rwkv_scan_bwd" if reverse else "rwkv_scan_fwd",
    )(*args)


def _rwkv_layer(x, mod, g_pre, g_post, w_out, p, proj0, *, nxb, nblk_out):
    b, t, d = x.shape
    e = p['k_k'].shape[0]
    vres = 'v0' in p
    h, dl = _pre_norm(x, g_pre, mod, shift=True, nxb=nxb)
    h2, d2 = h.reshape(b * t, d), dl.reshape(b * t, d)
    w_in = p['w_in']
    proj = _project(h2, _bf(w_in[:, :4 * e]), out_dtype=BF16, d2=d2, mu=p['mu'][:4, None, :],
                    group_width=e).reshape(b, t, 4 * e)
    n_lo = w_in.shape[1] - 4 * e
    groups = [4] * 128 + [5] * 128 + [2] * (n_lo - 256)
    pad = TAIL_WIDTH - n_lo
    w_tail = jnp.pad(w_in[:, 4 * e:], ((0, 0), (0, pad)))
    mu_cols = jnp.pad(p['mu'][np.asarray(groups)].T, ((0, 0), (0, pad)))
    tail = _project_tail(h2, d2, w_tail, mu_cols).reshape(b, t, TAIL_WIDTH)

    def lora(w, row0):
        return _bf(jnp.pad(w, ((row0, LANE - row0 - w.shape[0]), (0, 0))))

    zero = jnp.zeros((e,), F32)
    y0 = b0 = None
    for z in range(2):
        vec = jnp.stack([p['w0'][z], p['a0'][z], p['k_k'], p['k_a'], p['r_k'].reshape(e),
                         p['v0'] if vres else zero, p['ln_w'], p['ln_b']])
        out = _rwkv_scan(proj, tail, lora(p['w2'][z], 64 * z), lora(p['a2'][z], 64 * z),
                         lora(p['v2'], 0) if vres else None, vec, proj0, y0, b0,
                         reverse=(z == 1), vres=vres, last=(z == 1), nxb=nxb)
        if z == 0:
            y0, b0 = out
    x_new = _out_residual(out, _bf(w_out), x, g_post, mod, nxb=nxb, nblk_out=nblk_out)
    return x_new, proj


def _hgrn_scan_kernel(*refs, reverse, last, layer):
    it = iter(refs)
    q_ref, f_ref, i_ref, lb_ref = (next(it) for _ in range(4))
    if last:
        gate_ref, o0_ref, gn_ref, u_ref = (next(it) for _ in range(4))
    else:
        oo_ref = next(it)
    s_ref, gc_s, q_s, k_s, v_s, qg_s, kd_s, pt_s, o_s = it

    i = pl.program_id(2)
    tb, lg = q_s.shape
    nh = lg // HGRN_HEAD
    c = HGRN_SUB
    nsub = tb // c

    @pl.when(i == 0)
    def _():
        s_ref[...] = jnp.zeros_like(s_ref)

    logits = lb_ref[...]
    ex = jnp.exp(logits - jnp.max(logits, axis=0, keepdims=True))
    lb = jnp.sum(ex[1:layer + 1], axis=0, keepdims=True) / jnp.sum(ex, axis=0, keepdims=True)
    f = lb + (1.0 - lb) * _sigmoid(f_ref[0])
    g = jnp.log2(f)
    r1 = lax.broadcasted_iota(jnp.int32, (tb, tb), 0)
    c1 = lax.broadcasted_iota(jnp.int32, (tb, tb), 1)
    same = (r1 // c) == (c1 // c)
    before = (c1 >= r1) if reverse else (c1 <= r1)
    masks = jnp.concatenate([jnp.where(same & before, 1.0, 0.0), jnp.where(same, 1.0, 0.0)],
                            axis=0).astype(BF16)
    sums = _mask_dot(masks, g)
    gc, gtot = sums[:tb], sums[tb:]
    q = _silu(q_ref[0].astype(F32))
    kk = 1.0 - f
    gc_s[...] = gc
    q_s[...] = q
    k_s[...] = kk
    v_s[...] = i_ref[0].astype(F32)
    qg_s[...] = q * jnp.exp2(gc)
    kd_s[...] = kk * jnp.exp2(gtot - gc)
    pt_s[...] = jnp.exp2(gtot)

    trow = lax.broadcasted_iota(jnp.int32, (c, 1), 0)

    sls = [slice(hd * HGRN_HEAD, (hd + 1) * HGRN_HEAD) for hd in range(nh)]
    first, second = (slice(8, 16), slice(0, 8)) if reverse else (slice(0, 8), slice(8, 16))

    lane8 = lax.broadcasted_iota(jnp.int32, (8, HGRN_HEAD), 1)

    def pairwise(gcj, qj, kj):
        halves = [jnp.zeros((8, HGRN_HEAD), F32), jnp.zeros((8, HGRN_HEAD), F32)]
        for s in range(c):
            half = s // 8
            hs = slice(8 * half, 8 * half + 8)
            dlt = gcj[hs] - gcj[s:s + 1]
            if s != (8 * half + 7 if reverse else 8 * half):
                th = trow[hs]
                dlt = jnp.where((th <= s) if reverse else (th >= s), dlt, -1e30)
            w = qj[hs] * jnp.exp2(dlt) * kj[s:s + 1]
            halves[half] = jnp.where(lane8 == s, jnp.sum(w, axis=-1, keepdims=True), halves[half])
        return jnp.concatenate(halves, axis=0)

    def sub(jj, carry):
        subs = []
        for q in range(HGRN_UNROLL):
            step = jj * HGRN_UNROLL + q
            ji = (nsub - 1 - step) if reverse else step
            rows = pl.ds(pl.multiple_of(ji * c, c), c)
            subs.append(dict(rows=rows, gcs=[gc_s[rows, sl] for sl in sls],
                             qs=[q_s[rows, sl] for sl in sls], ks=[k_s[rows, sl] for sl in sls],
                             vs=[v_s[rows, sl] for sl in sls]))
        for d in subs:
            d['atts'] = []
            for gcj, qj, kj in zip(d['gcs'], d['qs'], d['ks']):
                gb = gcj[8:9] if reverse else gcj[7:8]
                qx = qj[second] * jnp.exp2(gcj[second] - gb)
                kx = kj[first] * jnp.exp2(gb - gcj[first])
                z8 = jnp.zeros_like(qx)
                q16 = jnp.concatenate([qx, z8] if reverse else [z8, qx], axis=0)
                k16 = jnp.concatenate([z8, kx] if reverse else [kx, z8], axis=0)
                d['atts'].append(_dot_nt(_bf(q16), _bf(k16)))
            d['upds'] = [_dot_tn(_bf(vj), _bf(kd_s[d['rows'], sl])) for sl, vj in zip(sls, d['vs'])]
        sts = [s_ref[hd] for hd in range(nh)]
        for d in subs:
            d['o_state'] = [_dot_nt(_bf(qg_s[d['rows'], sl]), _bf(st)) for sl, st in zip(sls, sts)]
            sts = [sts[hd] * pt_s[d['rows'], sls[hd]][0:1] + d['upds'][hd] for hd in range(nh)]
        for hd in range(nh):
            s_ref[hd] = sts[hd]
        for d in subs:
            for hd in range(nh):
                att = pairwise(d['gcs'][hd], d['qs'][hd], d['ks'][hd])[:, :c] + d['atts'][hd]
                o_s[d['rows'], sls[hd]] = d['o_state'][hd] + _dot(_bf(att), _bf(d['vs'][hd]))
        return carry

    lax.fori_loop(0, nsub // HGRN_UNROLL, sub, 0)

    if not last:
        oo_ref[0] = o_s[...]
    else:
        o = o0_ref[0] + o_s[...]
        gate = gate_ref[0].astype(F32)
        gn = gn_ref[...]
        for hd in range(nh):
            sl = slice(hd * HGRN_HEAD, (hd + 1) * HGRN_HEAD)
            oh = o[:, sl]
            ms = jnp.mean(oh * oh, axis=-1, keepdims=True)
            u_ref[0, :, sl] = (oh * lax.rsqrt(ms + NORM_EPS) * gn[:, sl] * _silu(gate[:, sl])).astype(BF16)


def _hgrn_scan(proj_qig, proj_f, lb_logits, gn, o0, *, reverse, last, nxb, layer):
    b, t, e3 = proj_qig.shape
    e = e3 // 3
    tb = TOKEN_BLOCK
    lg = HGRN_LANES
    ng = e // lg
    nblk = t // tb
    assert nblk == nxb + 1, "the context prefix must be exactly one token block"
    if reverse:
        blk = lambda i: jnp.where(i == 0, nxb, nxb - i)
    else:
        blk = lambda i: jnp.where(i == 0, nxb, i - 1)

    def col(off):
        return pl.BlockSpec((1, tb, lg), lambda bb, g, i: (bb, blk(i), off * ng + g))

    row = pl.BlockSpec((1, lg), lambda bb, g, i: (0, g))
    act = pl.BlockSpec((1, tb, lg), lambda bb, g, i: (bb, blk(i), g))
    in_specs = [col(0), col(1 if reverse else 0), col(1),
                pl.BlockSpec((lb_logits.shape[0], lg), lambda bb, g, i: (0, g))]
    args = [proj_qig, proj_f, proj_qig, lb_logits]
    if last:
        in_specs += [col(2), act, row]
        args += [proj_qig, o0, gn]
        out_shape = jax.ShapeDtypeStruct((b, t, e), BF16)
    else:
        out_shape = jax.ShapeDtypeStruct((b, t, e), F32)
    scratch = [pltpu.VMEM((lg // HGRN_HEAD, HGRN_HEAD, HGRN_HEAD), F32)] + [pltpu.VMEM((tb, lg), F32)] * 8
    return pl.pallas_call(
        functools.partial(_hgrn_scan_kernel, reverse=reverse, last=last, layer=layer),
        grid=(b, ng, nblk),
        in_specs=in_specs, out_specs=act, out_shape=out_shape,
        scratch_shapes=scratch,
        compiler_params=_cparams("parallel", "parallel", "arbitrary"),
        name="hgrn_scan_bwd" if reverse else "hgrn_scan_fwd",
    )(*args)


def _hgrn_layer(x, mod, g_pre, g_post, w_out, p, lb_logits, *, nxb, nblk_out, layer):
    b, t, d = x.shape
    e = lb_logits.shape[1]
    h = _pre_norm(x, g_pre, mod, shift=False, nxb=nxb)
    h2 = h.reshape(b * t, d)
    w_in = p['w_in']
    w_qig = _bf(jnp.concatenate([w_in[:, :e], w_in[:, 3 * e:]], axis=1))
    proj_f = _project(h2, _bf(w_in[:, e:3 * e]), out_dtype=F32).reshape(b, t, 2 * e)
    proj_qig = _project(h2, w_qig, out_dtype=BF16).reshape(b, t, 3 * e)
    gn = jnp.tile(p['g_norm'], e // HGRN_HEAD)[None]
    o0 = _hgrn_scan(proj_qig, proj_f, lb_logits, gn, None, reverse=False, last=False, nxb=nxb,
                    layer=layer)
    u = _hgrn_scan(proj_qig, proj_f, lb_logits, gn, o0, reverse=True, last=True, nxb=nxb,
                   layer=layer)
    return _out_residual(u, _bf(w_out), x, g_post, mod, nxb=nxb, nblk_out=nblk_out)


def _hy_pre_kernel(*refs, nxb, nblk):
    cur = refs[0:4]
    prv = refs[4:7]
    nxt = refs[7:10]
    w = refs[10:13]
    bias = refs[13:16]
    u_ref, g0_ref = refs[16], refs[17]
    i = pl.program_id(1)
    tb = cur[0].shape[1]
    row = lax.broadcasted_iota(jnp.int32, (tb, 1), 0)
    has_prev = jnp.logical_and(i != 0, i != nxb)
    has_next = jnp.logical_and(i != nxb - 1, i != nblk - 1)

    hr = prv[0].shape[1]

    def conv(j):
        x = cur[j][0].astype(F32)
        before = prv[j][0].astype(F32)[hr - 1:hr]
        after = nxt[j][0].astype(F32)[0:1]
        up = jnp.where(row == 0, jnp.where(has_prev, before, 0.0), pltpu.roll(x, 1, 0))
        dn = jnp.where(row == tb - 1, jnp.where(has_next, after, 0.0), pltpu.roll(x, tb - 1, 0))
        wj = w[j][...]
        return wj[0:1] * up + wj[1:2] * x + wj[2:3] * dn + bias[j][...]

    u_ref[0] = conv(2) * conv(1)
    g0_ref[0] = conv(0) * _silu(cur[3][0].astype(F32))


def _hy_pre(proj, conv_w, conv_b, *, nxb):
    b, t, e4 = proj.shape
    e = e4 // 4
    tb = TOKEN_BLOCK
    lg = 512
    ng = e // lg
    nblk = t // tb
    hr = 16
    hb = tb // hr
    nh = t // hr

    def col(off):
        return pl.BlockSpec((1, tb, lg), lambda bb, i, g: (bb, i, off * ng + g))

    def halo_prev(off):
        return pl.BlockSpec((1, hr, lg), lambda bb, i, g: (bb, jnp.maximum(i * hb - 1, 0), off * ng + g))

    def halo_next(off):
        return pl.BlockSpec((1, hr, lg), lambda bb, i, g: (bb, jnp.minimum((i + 1) * hb, nh - 1), off * ng + g))

    def wcol(rows, off):
        return pl.BlockSpec((rows, lg), lambda bb, i, g: (0, off * ng + g))

    in_specs = ([col(o) for o in range(4)] + [halo_prev(o) for o in range(3)]
                + [halo_next(o) for o in range(3)] + [wcol(3, o) for o in range(3)]
                + [wcol(1, o) for o in range(3)])
    args = [proj] * 10 + [conv_w] * 3 + [conv_b[None]] * 3
    act = pl.BlockSpec((1, tb, lg), lambda bb, i, g: (bb, i, g))
    return pl.pallas_call(
        functools.partial(_hy_pre_kernel, nxb=nxb, nblk=nblk),
        grid=(b, nblk, ng),
        in_specs=in_specs, out_specs=[act, act],
        out_shape=[jax.ShapeDtypeStruct((b, t, e), F32)] * 2,
        compiler_params=_cparams("parallel", "parallel", "parallel"),
        name="hyena_short_conv",
    )(*args)


def _hy_filter_kernel(fv_ref, w1_ref, b1_ref, w2_ref, b2_ref, w3_ref, b3_ref, sf_ref,
                      w4f_ref, w4b_ref, dl_ref, hf_ref, hb_ref, *, length):
    tl = hf_ref.shape[0]
    n = (pl.program_id(0) * tl + lax.broadcasted_iota(jnp.int32, (tl, 1), 0)).astype(F32)
    t = n * (1.0 / (length - 1))
    lane = lax.broadcasted_iota(jnp.int32, (tl, LANE), 1)
    nb = (HYENA_EMB - 1) // 2
    ang = (2.0 * math.pi / length) * n * fv_ref[...]
    z = jnp.where(lane == 0, t,
                  jnp.where(lane <= nb, jnp.cos(ang),
                            jnp.where(lane <= 2 * nb, -jnp.sin(ang), 0.0)))
    sf = sf_ref[...]
    hdn = jnp.sin(sf * (_dot(z, w1_ref[...], HIGHEST) + b1_ref[...]))
    hdn = jnp.sin(sf * (_dot(hdn, w2_ref[...], HIGHEST) + b2_ref[...]))
    hdn = jnp.sin(sf * (_dot(hdn, w3_ref[...], HIGHEST) + b3_ref[...]))
    window = jnp.exp(-t * dl_ref[...])
    hf_ref[...] = _dot(hdn, w4f_ref[...], HIGHEST) * window
    hb_ref[...] = _dot(hdn, w4b_ref[...], HIGHEST) * window


def _hy_filters(length, p, e):
    tl = min(length, 1024)
    lg = 512
    ng = e // lg
    nb = (HYENA_EMB - 1) // 2
    freqs = np.linspace(1e-4, nb - 1, nb, dtype=np.float32)
    fv = np.zeros((1, LANE), np.float32)
    fv[0, 1:1 + nb] = freqs
    fv[0, 1 + nb:1 + 2 * nb] = freqs
    deltas = np.abs(np.linspace(math.log(HYENA_TARGET) / HYENA_SLOW_DECAY,
                                math.log(HYENA_TARGET) / HYENA_FAST_DECAY, e, dtype=np.float32))[None]
    fw = HYENA_FILTER_WIDTH
    w1 = jnp.pad(p['f_w1'], ((0, LANE - HYENA_EMB), (0, 0)))
    full = lambda shape: pl.BlockSpec(shape, lambda r, g: (0, 0))
    in_specs = [full((1, LANE)), full((LANE, fw)), full((1, fw)), full((fw, fw)), full((1, fw)),
                full((fw, fw)), full((1, fw)), full((1, fw)),
                pl.BlockSpec((fw, lg), lambda r, g: (0, g)),
                pl.BlockSpec((fw, lg), lambda r, g: (0, ng + g)),
                pl.BlockSpec((1, lg), lambda r, g: (0, g))]
    out = pl.BlockSpec((tl, lg), lambda r, g: (r, g))
    return pl.pallas_call(
        functools.partial(_hy_filter_kernel, length=length),
        grid=(length // tl, ng),
        in_specs=in_specs, out_specs=[out, out],
        out_shape=[jax.ShapeDtypeStruct((length, e), F32)] * 2,
        compiler_params=_cparams("parallel", "parallel"),
        name="hyena_filters",
    )(jnp.asarray(fv), w1, p['f_b1'][None], p['f_w2'], p['f_b2'][None], p['f_w3'], p['f_b3'][None],
      p['sin_freq'][None], p['f_w4'], p['f_w4'], jnp.asarray(deltas))


def _cmul(x, h, half):
    xr, xi = x[:half], x[half:]
    hr, hi = h[:half], h[half:]
    return jnp.concatenate([xr * hr - xi * hi, xr * hi + xi * hr], axis=0)


def _conj(x, half):
    return jnp.concatenate([x[:half], -x[half:]], axis=0)


def _tdot(tab, x, idx=None):
    get = (lambda ref: ref[...]) if idx is None else (lambda ref: ref[idx])
    xh = x.astype(BF16)
    out = _dot(get(tab[0]), xh)
    if len(tab) == 2:
        xl = (x - xh.astype(F32)).astype(BF16)
        out = out + _dot(get(tab[0]), xl) + _dot(get(tab[1]), xh)
    return out


def _split_tables(refs, count):
    per = len(FFT_TABLE_PARTS)
    return [refs[i * per:(i + 1) * per] for i in range(count)], refs[count * per:]


def _long_conv_direct_kernel(u_ref, g0_ref, hf_ref, hb_ref, fb_ref, *rest):
    (fd_t, fi_t), (o_ref, h_s) = _split_tables(rest, 2)
    half = fd_t[0].shape[0] // 2

    @pl.when(pl.program_id(1) == 0)
    def _():
        h_s[...] = _tdot(fd_t, hf_ref[...]) + _conj(_tdot(fd_t, hb_ref[...]), half)

    u = u_ref[0]
    y = _tdot(fi_t, _cmul(_tdot(fd_t, u), h_s[...], half))
    o_ref[0] = (y + u * fb_ref[...]) * g0_ref[0]


def _long_conv_kernel(u_ref, g0_ref, hf_ref, hb_ref, fb_ref, *rest):
    (t1_t, t1t_t, f2_t, f2t_t), (o_ref, x_s, a_s, h_s) = _split_tables(rest, 4)
    n2, rows1, n1h = t1_t[0].shape
    n1 = rows1 // 2
    un = min(FFT_UNROLL, n1, n2)
    assert n1 % un == 0 and n2 % un == 0
    xp = _odd_pitch(n2)
    ap = _odd_pitch(rows1)

    def a_block(m2):
        return pl.ds(pl.multiple_of(m2 * ap, 8), rows1)

    def stage1(src):
        def copy(m1, carry):
            x_s[pl.ds(pl.multiple_of(m1 * xp, 8), n2), :] = src(pl.ds(pl.multiple_of(m1 * n2, n2), n2))
            return carry
        lax.fori_loop(0, n1h, copy, 0, unroll=4)

        def body(j, carry):
            m2s = [j * un + q for q in range(un)]
            xs = [x_s[pl.ds(m2, n1h, stride=xp), :] for m2 in m2s]
            outs = [_tdot(t1_t, x, m2) for m2, x in zip(m2s, xs)]
            for m2, o in zip(m2s, outs):
                a_s[a_block(m2), :] = o
            return carry
        lax.fori_loop(0, n2 // un, body, 0)

    def stage2(k1s):
        zs = [jnp.concatenate([a_s[pl.ds(k1, n2, stride=ap), :],
                               a_s[pl.ds(n1 + k1, n2, stride=ap), :]], axis=0) for k1 in k1s]
        return [_tdot(f2_t, z) for z in zs]

    def spec_rows(k1):
        return pl.ds(pl.multiple_of(k1 * 2 * n2, 2 * n2), 2 * n2)

    @pl.when(pl.program_id(1) == 0)
    def _():
        stage1(lambda rows: hf_ref[rows, :])

        def spec_f(j, carry):
            k1s = [j * un + q for q in range(un)]
            for k1, x in zip(k1s, stage2(k1s)):
                h_s[spec_rows(k1), :] = x
            return carry
        lax.fori_loop(0, n1 // un, spec_f, 0)
        stage1(lambda rows: hb_ref[rows, :])

        def spec_b(j, carry):
            k1s = [j * un + q for q in range(un)]
            for k1, x in zip(k1s, stage2(k1s)):
                h_s[spec_rows(k1), :] = h_s[spec_rows(k1), :] + _conj(x, n2)
            return carry
        lax.fori_loop(0, n1 // un, spec_b, 0)

    stage1(lambda rows: u_ref[0, rows, :])

    def mid(j, carry):
        k1s = [j * un + q for q in range(un)]
        ys = [_cmul(x, h_s[spec_rows(k1), :], n2) for k1, x in zip(k1s, stage2(k1s))]
        zs = [_tdot(f2t_t, y) for y in ys]
        for k1, z in zip(k1s, zs):
            a_s[pl.ds(k1, n2, stride=ap), :] = z[:n2]
            a_s[pl.ds(n1 + k1, n2, stride=ap), :] = z[n2:]
        return carry
    lax.fori_loop(0, n1 // un, mid, 0)

    def inv1(j, carry):
        m2s = [j * un + q for q in range(un)]
        ys = [_tdot(t1t_t, a_s[a_block(m2), :], m2) for m2 in m2s]
        for m2, y in zip(m2s, ys):
            x_s[pl.ds(m2, n1h, stride=xp), :] = y
        return carry
    lax.fori_loop(0, n2 // un, inv1, 0)

    fb = fb_ref[...]

    def finish(m1, carry):
        rows = pl.ds(pl.multiple_of(m1 * n2, n2), n2)
        y = x_s[pl.ds(pl.multiple_of(m1 * xp, 8), n2), :]
        o_ref[0, rows, :] = (y + u_ref[0, rows, :] * fb) * g0_ref[0, rows, :]
        return carry
    lax.fori_loop(0, n1h, finish, 0, unroll=4)


def _odd_pitch(rows):
    return rows if (rows // 8) % 2 else rows + 8


def _fft_tables(length):
    n = 2 * length
    n2 = FFT_N2
    n1 = n // n2
    n1h = n1 // 2
    k1 = np.arange(n1, dtype=np.float64)[:, None]
    m1 = np.arange(n1h, dtype=np.float64)[None, :]
    t1 = []
    for m2 in range(n2):
        phi = 2.0 * np.pi * (k1 * m1 / n1 + k1 * m2 / n)
        t1.append(np.concatenate([np.cos(phi), -np.sin(phi)], axis=0))
    t1 = np.stack(t1)
    t1t = np.transpose(t1, (0, 2, 1)) / n
    k2 = np.arange(n2, dtype=np.float64)[:, None]
    m2 = np.arange(n2, dtype=np.float64)[None, :]
    th = 2.0 * np.pi * k2 * m2 / n2
    mr, mi = np.cos(th), -np.sin(th)
    f2 = np.block([[mr, -mi], [mi, mr]])
    return _hi_lo(t1) + _hi_lo(t1t) + _hi_lo(f2) + _hi_lo(f2.T)


def _hi_lo(a):
    a32 = jnp.asarray(a.astype(np.float32))
    hi = a32.astype(BF16)
    parts = {"hi": hi, "lo": (a32 - hi.astype(F32)).astype(BF16)}
    return [parts[name] for name in FFT_TABLE_PARTS]


def _dft_tables(length):
    n = 2 * length
    k = np.arange(n, dtype=np.float64)[:, None]
    m = np.arange(length, dtype=np.float64)[None, :]
    phi = 2.0 * np.pi * k * m / n
    fd = np.concatenate([np.cos(phi), -np.sin(phi)], axis=0)
    return _hi_lo(fd) + _hi_lo(fd.T / n)


def _long_conv(u, g0, hf, hb, fbias, *, length, blk_index):
    b, t, e = u.shape
    lg = FFT_LANES
    ng = e // lg
    seq_spec = pl.BlockSpec((1, length, lg), lambda g, bb: (bb, blk_index, g))
    filt_spec = pl.BlockSpec((length, lg), lambda g, bb: (0, g))
    row_spec = pl.BlockSpec((1, lg), lambda g, bb: (0, g))
    out_spec = pl.BlockSpec((1, length, lg), lambda g, bb: (bb, 0, g))
    common = dict(
        grid=(ng, b),
        out_specs=out_spec,
        out_shape=jax.ShapeDtypeStruct((b, length, e), F32),
        compiler_params=_cparams("parallel", "arbitrary"),
    )
    const = lambda a: pl.BlockSpec(a.shape, lambda g, bb: (0,) * a.ndim)
    if length <= TOKEN_BLOCK:
        tabs = _dft_tables(length)
        return pl.pallas_call(
            _long_conv_direct_kernel,
            in_specs=[seq_spec, seq_spec, filt_spec, filt_spec, row_spec] + [const(a) for a in tabs],
            scratch_shapes=[pltpu.VMEM((4 * length, lg), F32)],
            name="hyena_long_conv_ctx", **common,
        )(u, g0, hf, hb, fbias, *tabs)
    tabs = _fft_tables(length)
    n2 = FFT_N2
    n1 = 2 * length // n2
    return pl.pallas_call(
        _long_conv_kernel,
        in_specs=[seq_spec, seq_spec, filt_spec, filt_spec, row_spec] + [const(a) for a in tabs],
        scratch_shapes=[pltpu.VMEM((n1 // 2 * _odd_pitch(n2), lg), F32),
                        pltpu.VMEM((n2 * _odd_pitch(2 * n1), lg), F32),
                        pltpu.VMEM((4 * length, lg), F32)],
        name="hyena_long_conv", **common,
    )(u, g0, hf, hb, fbias, *tabs)


def _hyena_layer(x, mod, g_pre, g_post, w_out, p, *, nxb, nblk_out):
    b, t, d = x.shape
    e = p['filter_bias'].shape[0]
    seq = nxb * TOKEN_BLOCK
    ctx_len = t - seq
    h = _pre_norm(x, g_pre, mod, shift=False, nxb=nxb)
    proj = _project(h.reshape(b * t, d), _bf(p['w_in']), out_dtype=BF16).reshape(b, t, 4 * e)
    u, g0 = _hy_pre(proj, p['conv_w'], p['conv_b'], nxb=nxb)
    fbias = p['filter_bias'][None]
    hf, hb = _hy_filters(seq, p, e)
    yx = _long_conv(u, g0, hf, hb, fbias, length=seq, blk_index=0)
    hf, hb = _hy_filters(ctx_len, p, e)
    yc = _long_conv(u, g0, hf, hb, fbias, length=ctx_len, blk_index=seq // ctx_len)
    y = jnp.concatenate([yx, yc], axis=1)
    return _out_residual(y, _bf(w_out), x, g_post, mod, nxb=nxb, nblk_out=nblk_out)


def _modulation(c, c_ctx, ada_w, ada_b):
    b, d = c.shape
    depth = ada_w.shape[0]
    rows = -(-(b + 1) // 8) * 8
    cstack = jnp.zeros((rows, d), F32).at[:b].set(c).at[b].set(c_ctx)
    ada = _ada_all(cstack, ada_w, ada_b).reshape(depth, rows, 3, d)
    lat = ada[:, :b]
    cx = jnp.broadcast_to(ada[:, b:b + 1], lat.shape)
    return jnp.stack([lat, cx], axis=2)


def kernel(x, c, ctx, c_ctx, ada_w, ada_b, norm_pre, norm_post, w_out,
           l0_w_in, l0_mu, l0_w0, l0_w2, l0_a0, l0_a2, l0_k_k, l0_k_a, l0_r_k, l0_ln_w, l0_ln_b,
           l1_w_in, l1_conv_w, l1_conv_b, l1_f_w1, l1_f_b1, l1_f_w2, l1_f_b2, l1_f_w3, l1_f_b3,
           l1_f_w4, l1_sin_freq, l1_filter_bias,
           l2_w_in, l2_g_norm, hgrn_lb_logits,
           l3_w_in, l3_mu, l3_w0, l3_w2, l3_a0, l3_a2, l3_k_k, l3_k_a, l3_r_k, l3_ln_w, l3_ln_b,
           l3_v0, l3_v2):
    rwkv0 = dict(w_in=l0_w_in, mu=l0_mu, w0=l0_w0, w2=l0_w2, a0=l0_a0, a2=l0_a2, k_k=l0_k_k,
                 k_a=l0_k_a, r_k=l0_r_k, ln_w=l0_ln_w, ln_b=l0_ln_b)
    hyena1 = dict(w_in=l1_w_in, conv_w=l1_conv_w, conv_b=l1_conv_b, f_w1=l1_f_w1, f_b1=l1_f_b1,
                  f_w2=l1_f_w2, f_b2=l1_f_b2, f_w3=l1_f_w3, f_b3=l1_f_b3, f_w4=l1_f_w4,
                  sin_freq=l1_sin_freq, filter_bias=l1_filter_bias)
    hgrn2 = dict(w_in=l2_w_in, g_norm=l2_g_norm)
    rwkv3 = dict(w_in=l3_w_in, mu=l3_mu, w0=l3_w0, w2=l3_w2, a0=l3_a0, a2=l3_a2, k_k=l3_k_k,
                 k_a=l3_k_a, r_k=l3_r_k, ln_w=l3_ln_w, ln_b=l3_ln_b, v0=l3_v0, v2=l3_v2)
    b, seq, d = x.shape
    assert ctx.shape[1] == TOKEN_BLOCK and seq % TOKEN_BLOCK == 0
    nxb = seq // TOKEN_BLOCK
    mods = _modulation(c, c_ctx, ada_w, ada_b)
    xa = jnp.concatenate([x, ctx], axis=1)
    xa, proj0 = _rwkv_layer(xa, mods[0], norm_pre[0][None], norm_post[0][None], w_out[0], rwkv0,
                            None, nxb=nxb, nblk_out=nxb + 1)
    xa = _hyena_layer(xa, mods[1], norm_pre[1][None], norm_post[1][None], w_out[1], hyena1,
                      nxb=nxb, nblk_out=nxb + 1)
    xa = _hgrn_layer(xa, mods[2], norm_pre[2][None], norm_post[2][None], w_out[2], hgrn2,
                     hgrn_lb_logits, nxb=nxb, nblk_out=nxb + 1, layer=2)
    xa, _ = _rwkv_layer(xa, mods[3], norm_pre[3][None], norm_post[3][None], w_out[3], rwkv3,
                        proj0, nxb=nxb, nblk_out=nxb)
    return xa
```

```python
import functools
import math

import jax
import jax.numpy as jnp
import numpy as np
from jax import lax
from jax.experimental import pallas as pl
from jax.experimental.pallas import tpu as pltpu

F32 = jnp.float32
BF16 = jnp.bfloat16
HIGHEST = lax.Precision.HIGHEST

NORM_EPS = 1e-6
GRID_W = 64
TOKEN_BLOCK = 256
LANE = 128
VMEM_LIMIT = 56 * 1024 * 1024

RWKV_HEAD = 64
RWKV_CHUNK = 64
RWKV_LANES = 512
RWKV_GN_EPS = 64e-5
TAIL_WIDTH = 384

HGRN_HEAD = 128
HGRN_SUB = 16
HGRN_LANES = 512
HGRN_UNROLL = 16

HYENA_EMB = 33
HYENA_FILTER_WIDTH = 64
HYENA_FAST_DECAY = 0.3
HYENA_SLOW_DECAY = 1.5
HYENA_TARGET = 1e-2
FFT_N2 = 64
FFT_LANES = 128
FFT_TABLE_PARTS = ("hi",)
FFT_UNROLL = 16


def _cparams(*sem):
    return pltpu.CompilerParams(dimension_semantics=sem, vmem_limit_bytes=VMEM_LIMIT)


def _dot(a, b, precision=None):
    return jnp.dot(a, b, preferred_element_type=F32, precision=precision)


def _dot_nt(a, b, precision=None):
    return lax.dot_general(a, b, (((1,), (1,)), ((), ())),
                           preferred_element_type=F32, precision=precision)


def _dot_tn(a, b):
    return lax.dot_general(a, b, (((0,), (0,)), ((), ())), preferred_element_type=F32)


def _bf(x):
    return x.astype(BF16)


def _split_dot(x, w_bf16):
    hi = x.astype(BF16)
    lo = (x - hi.astype(F32)).astype(BF16)
    return _dot(hi, w_bf16) + _dot(lo, w_bf16)


def _mask_dot(m_bf16, x):
    hi = x.astype(BF16)
    lo = (x - hi.astype(F32)).astype(BF16)
    return _dot(m_bf16, hi) + _dot(m_bf16, lo)


def _tree_sum(terms):
    terms = list(terms)
    while len(terms) > 1:
        terms = [a + b for a, b in zip(terms[0::2], terms[1::2])] + (terms[-1:] if len(terms) % 2 else [])
    return terms[0]


def _sigmoid(x):
    return 1.0 / (1.0 + jnp.exp(-x))


def _silu(x):
    return x * _sigmoid(x)


def _ada_kernel(c_ref, w_ref, b_ref, o_ref):
    o_ref[0] = _dot(_silu(c_ref[...]), w_ref[0], HIGHEST) + b_ref[0]


def _ada_all(cstack, ada_w, ada_b):
    depth, d, d3 = ada_w.shape
    rows = cstack.shape[0]
    nt = d3 // d
    return pl.pallas_call(
        _ada_kernel,
        grid=(depth, nt),
        in_specs=[
            pl.BlockSpec((rows, d), lambda l, j: (0, 0)),
            pl.BlockSpec((1, d, d), lambda l, j: (l, 0, j)),
            pl.BlockSpec((1, 1, d), lambda l, j: (l, 0, j)),
        ],
        out_specs=pl.BlockSpec((1, rows, d), lambda l, j: (l, 0, j)),
        out_shape=jax.ShapeDtypeStruct((depth, rows, d3), F32),
        compiler_params=_cparams("parallel", "parallel"),
        name="adaln",
    )(cstack, ada_w, ada_b.reshape(depth, 1, d3))


def _norm_kernel(*refs, shift, nxb):
    if shift:
        x_ref, xp_ref, xn_ref, g_ref, mod_ref, h_ref, d_ref = refs
    else:
        x_ref, g_ref, mod_ref, h_ref = refs
    i = pl.program_id(1)
    g = g_ref[...]
    shift_v = mod_ref[0, 0, 0:1, :]
    scale1p = 1.0 + mod_ref[0, 0, 1:2, :]

    def nrm(x):
        ms = jnp.mean(x * x, axis=-1, keepdims=True)
        return x * lax.rsqrt(ms + NORM_EPS) * g * scale1p + shift_v

    h = nrm(x_ref[0])
    h_ref[0] = h.astype(BF16)
    if not shift:
        return
    tb, d = h.shape
    q = d // 4
    row = lax.broadcasted_iota(jnp.int32, (tb, 1), 0)

    @pl.when(i < nxb)
    def _():
        col = row % GRID_W
        left = jnp.where(col > 0, pltpu.roll(h[:, 0:q], 1, 0), 0.0)
        right = jnp.where(col < GRID_W - 1, pltpu.roll(h[:, q:2 * q], tb - 1, 0), 0.0)
        hp = nrm(xp_ref[0])[:, 2 * q:3 * q]
        hn = nrm(xn_ref[0])[:, 3 * q:]
        hp = jnp.where(i > 0, hp, 0.0)
        hn = jnp.where(i < nxb - 1, hn, 0.0)
        up = jnp.concatenate([hp, h[:tb - GRID_W, 2 * q:3 * q]], axis=0)
        down = jnp.concatenate([h[GRID_W:, 3 * q:], hn], axis=0)
        hs = jnp.concatenate([left, right, up, down], axis=-1)
        d_ref[0] = (hs - h).astype(BF16)

    @pl.when(i >= nxb)
    def _():
        half = d // 2
        prev = jnp.where(row > 0, pltpu.roll(h[:, :half], 1, 0), 0.0)
        nxt = jnp.where(row < tb - 1, pltpu.roll(h[:, half:], tb - 1, 0), 0.0)
        hs = jnp.concatenate([prev, nxt], axis=-1)
        d_ref[0] = (hs - h).astype(BF16)


def _pre_norm(x, g, mod, *, shift, nxb):
    b, t, d = x.shape
    tb = TOKEN_BLOCK
    nblk = t // tb
    hb = tb // GRID_W
    nhalo = t // GRID_W
    seg = lambda i: jnp.where(i < nxb, 0, 1)
    x_spec = pl.BlockSpec((1, tb, d), lambda bb, i: (bb, i, 0))
    g_spec = pl.BlockSpec((1, d), lambda bb, i: (0, 0))
    mod_spec = pl.BlockSpec((1, 1, 3, d), lambda bb, i: (bb, seg(i), 0, 0))
    out_spec = pl.BlockSpec((1, tb, d), lambda bb, i: (bb, i, 0))
    if shift:
        in_specs = [
            x_spec,
            pl.BlockSpec((1, GRID_W, d), lambda bb, i: (bb, jnp.maximum(i * hb - 1, 0), 0)),
            pl.BlockSpec((1, GRID_W, d), lambda bb, i: (bb, jnp.minimum((i + 1) * hb, nhalo - 1), 0)),
            g_spec, mod_spec,
        ]
        args = (x, x, x, g, mod)
        out_specs = [out_spec, out_spec]
        out_shape = [jax.ShapeDtypeStruct((b, t, d), BF16)] * 2
    else:
        in_specs = [x_spec, g_spec, mod_spec]
        args = (x, g, mod)
        out_specs = out_spec
        out_shape = jax.ShapeDtypeStruct((b, t, d), BF16)
    return pl.pallas_call(
        functools.partial(_norm_kernel, shift=shift, nxb=nxb),
        grid=(b, nblk),
        in_specs=in_specs,
        out_specs=out_specs,
        out_shape=out_shape,
        compiler_params=_cparams("parallel", "parallel"),
        name="pre_norm_shift" if shift else "pre_norm",
    )(*args)


def _row_tile(m):
    for tm in (1024, 512, 256):
        if m % tm == 0:
            return tm
    raise ValueError(f"token count {m} is not a multiple of {TOKEN_BLOCK}")


def _proj_lerp_kernel(h_ref, d_ref, mu_ref, w_ref, o_ref, lhs_ref, *, tiles_per_group):
    j = pl.program_id(1)

    @pl.when(j % tiles_per_group == 0)
    def _():
        lhs_ref[...] = (h_ref[...].astype(F32) + mu_ref[0] * d_ref[...].astype(F32)).astype(BF16)

    o_ref[...] = _dot(lhs_ref[...], w_ref[...]).astype(o_ref.dtype)


def _proj_kernel(h_ref, w_ref, o_ref):
    o_ref[...] = _dot(h_ref[...], w_ref[...]).astype(o_ref.dtype)


def _project(h2, w_bf16, *, out_dtype, d2=None, mu=None, group_width=None):
    m, d = h2.shape
    n = w_bf16.shape[1]
    tm = _row_tile(m)
    tn = 1024
    lhs_spec = pl.BlockSpec((tm, d), lambda i, j: (i, 0))
    w_spec = pl.BlockSpec((d, tn), lambda i, j: (0, j))
    o_spec = pl.BlockSpec((tm, tn), lambda i, j: (i, j))
    if d2 is None:
        return pl.pallas_call(
            _proj_kernel, grid=(m // tm, n // tn),
            in_specs=[lhs_spec, w_spec], out_specs=o_spec,
            out_shape=jax.ShapeDtypeStruct((m, n), out_dtype),
            compiler_params=_cparams("parallel", "parallel"),
            name="project",
        )(h2, w_bf16)
    tpg = group_width // tn
    return pl.pallas_call(
        functools.partial(_proj_lerp_kernel, tiles_per_group=tpg),
        grid=(m // tm, n // tn),
        in_specs=[lhs_spec, lhs_spec,
                  pl.BlockSpec((1, 1, d), lambda i, j: (j // tpg, 0, 0)),
                  w_spec],
        out_specs=o_spec,
        out_shape=jax.ShapeDtypeStruct((m, n), out_dtype),
        scratch_shapes=[pltpu.VMEM((tm, d), BF16)],
        compiler_params=_cparams("parallel", "arbitrary"),
        name="project_lerp",
    )(h2, d2, mu, w_bf16)


def _tail_kernel(h_ref, d_ref, w_ref, mu_ref, o_ref):
    w = w_ref[...]
    o_ref[...] = _dot(h_ref[...], _bf(w)) + _dot(d_ref[...], _bf(w * mu_ref[...]))


def _project_tail(h2, d2, w_tail, mu_cols):
    m, d = h2.shape
    n = w_tail.shape[1]
    tm = _row_tile(m)
    lhs_spec = pl.BlockSpec((tm, d), lambda i: (i, 0))
    w_spec = pl.BlockSpec((d, n), lambda i: (0, 0))
    return pl.pallas_call(
        _tail_kernel, grid=(m // tm,),
        in_specs=[lhs_spec, lhs_spec, w_spec, w_spec],
        out_specs=pl.BlockSpec((tm, n), lambda i: (i, 0)),
        out_shape=jax.ShapeDtypeStruct((m, n), F32),
        compiler_params=_cparams("parallel"),
        name="project_tail",
    )(h2, d2, w_tail, mu_cols)


def _out_kernel(u_ref, w_ref, x_ref, g_ref, mod_ref, *rest):
    y = _dot(_bf(u_ref[0]), w_ref[...])
    ms = jnp.mean(y * y, axis=-1, keepdims=True)
    yn = y * lax.rsqrt(ms + NORM_EPS) * g_ref[...]
    x_new = x_ref[0] + yn * mod_ref[0, 0, 2:3, :]
    if len(rest) == 1:
        rest[0][0] = x_new
        return
    gn_ref, modn_ref, o_ref, h_ref = rest
    o_ref[0] = x_new
    msn = jnp.mean(x_new * x_new, axis=-1, keepdims=True)
    h = x_new * lax.rsqrt(msn + NORM_EPS) * gn_ref[...] * (1.0 + modn_ref[0, 0, 1:2, :]) + modn_ref[0, 0, 0:1, :]
    h_ref[0] = h.astype(BF16)


def _out_residual(u, w_bf16, x, g, mod, *, nxb, nblk_out, next_norm=None):
    b, t, e = u.shape
    d = x.shape[-1]
    tb = TOKEN_BLOCK
    seg = lambda i: jnp.where(i < nxb, 0, 1)
    act = pl.BlockSpec((1, tb, d), lambda bb, i: (bb, i, 0))
    gain = pl.BlockSpec((1, d), lambda bb, i: (0, 0))
    mods = pl.BlockSpec((1, 1, 3, d), lambda bb, i: (bb, seg(i), 0, 0))
    in_specs = [pl.BlockSpec((1, tb, e), lambda bb, i: (bb, i, 0)),
                pl.BlockSpec((e, d), lambda bb, i: (0, 0)),
                act, gain, mods]
    args = [u, w_bf16, x, g, mod]
    out_specs = act
    out_shape = jax.ShapeDtypeStruct((b, nblk_out * tb, d), F32)
    if next_norm is not None:
        in_specs += [gain, mods]
        args += list(next_norm)
        out_specs = [act, act]
        out_shape = [out_shape, jax.ShapeDtypeStruct((b, nblk_out * tb, d), BF16)]
    return pl.pallas_call(
        _out_kernel, grid=(b, nblk_out),
        in_specs=in_specs, out_specs=out_specs, out_shape=out_shape,
        compiler_params=_cparams("parallel", "parallel"),
        name="out_residual",
    )(*args)


def _rwkv_scan_kernel(*refs, reverse, vres, last):
    it = iter(refs)
    r_ref, k_ref, v_ref, lo_ref, w2_ref, a2_ref = (next(it) for _ in range(6))
    v2_ref = next(it) if vres else None
    vec_ref = next(it)
    vf_ref = next(it) if vres else None
    if last:
        gate_ref, y0_ref, b0_ref, u_ref = (next(it) for _ in range(4))
    else:
        yo_ref, bo_ref = next(it), next(it)
    s_ref, y_s, b_s = it

    i = pl.program_id(2)
    tb, lg = y_s.shape
    npair = lg // LANE
    c = RWKV_CHUNK
    nch = tb // c
    hd = RWKV_HEAD

    @pl.when(i == 0)
    def _():
        s_ref[...] = jnp.zeros_like(s_ref)

    vec = vec_ref[...]
    lo = lo_ref[0]
    r = r_ref[0].astype(F32)
    k = k_ref[0].astype(F32)
    v = v_ref[0].astype(F32)
    zt = vec[0:1] + _dot(_bf(jnp.tanh(lo[:, :LANE])), w2_ref[...])
    logw = -(math.exp(-0.5) * math.log2(math.e)) * _sigmoid(zt)
    a = _sigmoid(vec[1:2] + _dot(_bf(lo[:, LANE:2 * LANE]), a2_ref[...]))
    if vres:
        v = v + (vf_ref[0].astype(F32) - v) * _sigmoid(vec[5:6] + _dot(_bf(lo[:, 2 * LANE:]), v2_ref[...]))
    kdir = k * (1.0 + (a - 1.0) * vec[3:4])
    kkr = k * vec[2:3]
    bd = (lax.broadcasted_iota(jnp.int32, (LANE, LANE), 0) // hd
          == lax.broadcasted_iota(jnp.int32, (LANE, LANE), 1) // hd)
    ones_bd = jnp.where(bd, 1.0, 0.0).astype(BF16)

    def segsum(x):
        return jnp.concatenate(
            [_dot(_bf(x[:, p * LANE:(p + 1) * LANE]), ones_bd) for p in range(npair)], axis=-1)

    kk = kkr * lax.rsqrt(jnp.maximum(segsum(kkr * kkr), 1e-24))
    b_s[...] = segsum(r * kdir * vec[4:5]) * v
    alpha = -kk
    beta = kk * a

    t1 = lax.broadcasted_iota(jnp.int32, (c, c), 0)
    s1 = lax.broadcasted_iota(jnp.int32, (c, c), 1)
    t2 = lax.broadcasted_iota(jnp.int32, (c, LANE), 0)
    lane2 = lax.broadcasted_iota(jnp.int32, (c, LANE), 1)
    s2 = lane2 % c
    if reverse:
        incl1, incl2, strict2 = s1 >= t1, s2 >= t2, s2 > t2
    else:
        incl1, incl2, strict2 = s1 <= t1, s2 <= t2, s2 < t2
    tri = jnp.where(incl1, 1.0, 0.0).astype(BF16)
    low = lane2 < hd
    high = jnp.logical_not(low)
    low_x = lax.broadcasted_iota(jnp.int32, (2 * c, LANE), 1) < hd
    strict_lo, strict_hi = strict2 & low, strict2 & high
    incl_lo, incl_hi = incl2 & low, incl2 & high
    eye2 = jnp.where(lax.broadcasted_iota(jnp.int32, (LANE, LANE), 0)
                     == lax.broadcasted_iota(jnp.int32, (LANE, LANE), 1), 1.0, 0.0)
    zeros_cv = jnp.zeros((c, LANE), F32)

    def stack(top, bot):
        return jnp.concatenate([top, bot], axis=0)

    def fold(z):
        return z[:c] + z[c:]

    chains = {}
    rounds = int(math.log2(c)) - 1

    def st_decay(ci):
        rows = slice(ci * c, (ci + 1) * c)
        lw = logw[rows]
        lc = _mask_dot(tri, lw)
        ltot = lc[0:1] if reverse else lc[c - 1:c]
        p_inv = jnp.exp2(-lc)
        p_all = jnp.exp2(ltot)
        ab = alpha[rows] * jnp.exp2(lc - lw)
        rb = r[rows] * jnp.exp2(lc)
        bt = beta[rows] * p_inv
        kt = kdir[rows] * p_inv
        bh = bt * p_all
        kh = kt * p_all
        chains[ci] = []
        for p in range(npair):
            sl = slice(p * LANE, (p + 1) * LANE)
            chains[ci].append(dict(p=p, ab=ab[:, sl], rb=rb[:, sl], bt=bt[:, sl], kt=kt[:, sl],
                                   bh=bh[:, sl], kh=kh[:, sl], v=v[rows, sl], p_all=p_all[:, sl]))

    def st_gram(ci):
        for d in chains[ci]:
            y01 = jnp.concatenate([jnp.where(low_x, stack(d['bt'], d['kt']), 0.0),
                                   jnp.where(low_x, 0.0, stack(d['kt'], d['bt']))], axis=0)
            d['g'] = _dot_nt(_bf(stack(d['ab'], d['rb'])), _bf(y01))

    def st_blocks(ci):
        for d in chains[ci]:
            g0t, g0b = d['g'][:c, :LANE], d['g'][c:, :LANE]
            g1t, g1b = d['g'][:c, LANE:], d['g'][c:, LANE:]
            d['a'] = stack(jnp.where(strict_lo, g0t, 0.0), jnp.where(strict_hi, g1t, 0.0))
            arb = stack(jnp.where(incl_lo, g0b, 0.0), jnp.where(incl_hi, g1b, 0.0))
            ark = stack(jnp.where(incl_hi, g0b, 0.0), jnp.where(incl_lo, g1b, 0.0))
            d['arbk'] = _bf(jnp.concatenate([arb, ark], axis=1))
            ak = stack(jnp.where(strict_hi, g0t, 0.0), jnp.where(strict_lo, g1t, 0.0))
            d['vx'] = stack(jnp.where(high, d['v'], 0.0), jnp.where(low, d['v'], 0.0))
            d['w'] = _dot(_bf(ak), _bf(d['vx']))
            del d['g']

    def st_square(ci):
        for d in chains[ci]:
            d['tm'] = eye2 + d['a']
            apb = _bf(d['a'])
            d['a'] = _dot(apb, apb)

    def st_round(ci):
        for d in chains[ci]:
            pt = _dot(_bf(d['a']), _bf(jnp.concatenate([d['a'], d['tm']], axis=1)))
            d['a'] = pt[:, :LANE]
            d['tm'] = d['tm'] + pt[:, LANE:]

    def st_last_round(ci):
        for d in chains[ci]:
            d['tm'] = d['tm'] + _dot(_bf(d['a']), _bf(d['tm']))

    def st_apply(ci):
        for d in chains[ci]:
            ab_st = stack(jnp.where(low, d['ab'], 0.0), jnp.where(high, d['ab'], 0.0))
            d['tz'] = _dot(_bf(d['tm']), _bf(jnp.concatenate([ab_st, d['w']], axis=1)))

    def st_out(ci):
        for d in chains[ci]:
            lower = jnp.concatenate([jnp.zeros((2 * c, LANE), F32), d['vx']], axis=1)
            yz = _dot(d['arbk'], _bf(stack(d['tz'], lower)))
            ta, tw = fold(d['tz'][:, :LANE]), fold(d['tz'][:, LANE:])
            d['ra'] = _bf(d['rb'] + fold(yz[:, :LANE]))
            d['yw'] = fold(yz[:, LANE:])
            lhs = stack(jnp.concatenate([ta, tw], axis=1), jnp.concatenate([zeros_cv, d['v']], axis=1))
            mn = _dot_tn(_bf(lhs), _bf(stack(d['bh'], d['kh'])))
            d['m'] = _bf(jnp.where(bd, mn[:LANE], 0.0))
            d['n'] = jnp.where(bd, mn[LANE:], 0.0)

    state = [s_ref[p] for p in range(npair)]

    def st_recur(ci):
        for d in chains[ci]:
            p = d['p']
            sp = state[p]
            spb = _bf(sp)
            y_s[ci * c:(ci + 1) * c, p * LANE:(p + 1) * LANE] = _dot_nt(d['ra'], spb) + d['yw']
            state[p] = sp * d['p_all'] + _dot(spb, d['m']) + d['n']

    stages = ([st_decay, st_gram, st_blocks, st_square] + [st_round] * (rounds - 1)
              + [st_last_round, st_apply, st_out])
    for stage in stages:
        for ci in range(nch):
            stage(ci)
    for ci in (reversed(range(nch)) if reverse else range(nch)):
        st_recur(ci)
    for p in range(npair):
        s_ref[p] = state[p]

    bonus = b_s[...]
    if not last:
        yo_ref[0] = y_s[...]
        bo_ref[0] = bonus
    else:
        yy = y0_ref[0] + y_s[...]
        mean = segsum(yy) * (1.0 / hd)
        yc = yy - mean
        var = segsum(yc * yc) * (1.0 / hd)
        yn = yc * lax.rsqrt(var + RWKV_GN_EPS) * vec[6:7] + vec[7:8]
        u_ref[0] = ((yn + b0_ref[0] + bonus) * _silu(gate_ref[0].astype(F32))).astype(BF16)


def _rwkv_scan(proj, tail, w2p, a2p, v2p, vec, proj0, y0, b0, *, reverse, vres, last, nxb):
    b, t, e4 = proj.shape
    e = e4 // 4
    tb = TOKEN_BLOCK
    lg = RWKV_LANES
    ng = e // lg
    nblk = t // tb
    assert nblk == nxb + 1, "the context prefix must be exactly one token block"
    if reverse:
        blk = lambda i: jnp.where(i == 0, nxb, nxb - i)
    else:
        blk = lambda i: jnp.where(i == 0, nxb, i - 1)

    def col(off):
        return pl.BlockSpec((1, tb, lg), lambda bb, g, i: (bb, blk(i), off * ng + g))

    wspec = pl.BlockSpec((LANE, lg), lambda bb, g, i: (0, g))
    in_specs = [col(0), col(1), col(2),
                pl.BlockSpec((1, tb, TAIL_WIDTH), lambda bb, g, i: (bb, blk(i), 0)),
                wspec, wspec]
    args = [proj, proj, proj, tail, w2p, a2p]
    if vres:
        in_specs.append(wspec)
        args.append(v2p)
    in_specs.append(pl.BlockSpec((8, lg), lambda bb, g, i: (0, g)))
    args.append(vec)
    if vres:
        in_specs.append(col(2))
        args.append(proj0)
    act = pl.BlockSpec((1, tb, lg), lambda bb, g, i: (bb, blk(i), g))
    if last:
        in_specs += [col(3), act, act]
        args += [proj, y0, b0]
        out_specs = act
        out_shape = jax.ShapeDtypeStruct((b, t, e), BF16)
    else:
        out_specs = [act, act]
        out_shape = [jax.ShapeDtypeStruct((b, t, e), F32)] * 2
    scratch = [pltpu.VMEM((lg // LANE, LANE, LANE), F32)] + [pltpu.VMEM((tb, lg), F32)] * 2
    return pl.pallas_call(
        functools.partial(_rwkv_scan_kernel, reverse=reverse, vres=vres, last=last),
        grid=(b, ng, nblk),
        in_specs=in_specs, out_specs=out_specs, out_shape=out_shape,
        scratch_shapes=scratch,
        compiler_params=_cparams("parallel", "parallel", "arbitrary"),
        name="rwkv_scan_bwd" if reverse else "rwkv_scan_fwd",
    )(*args)


def _rwkv_layer(x, mod, g_pre, g_post, w_out, p, proj0, *, nxb, nblk_out, next_norm=None):
    b, t, d = x.shape
    e = p['k_k'].shape[0]
    vres = 'v0' in p
    h, dl = _pre_norm(x, g_pre, mod, shift=True, nxb=nxb)
    h2, d2 = h.reshape(b * t, d), dl.reshape(b * t, d)
    w_in = p['w_in']
    proj = _project(h2, _bf(w_in[:, :4 * e]), out_dtype=BF16, d2=d2, mu=p['mu'][:4, None, :],
                    group_width=e).reshape(b, t, 4 * e)
    n_lo = w_in.shape[1] - 4 * e
    groups = [4] * 128 + [5] * 128 + [2] * (n_lo - 256)
    pad = TAIL_WIDTH - n_lo
    w_tail = jnp.pad(w_in[:, 4 * e:], ((0, 0), (0, pad)))
    mu_cols = jnp.pad(p['mu'][np.asarray(groups)].T, ((0, 0), (0, pad)))
    tail = _project_tail(h2, d2, w_tail, mu_cols).reshape(b, t, TAIL_WIDTH)

    def lora(w, row0):
        return _bf(jnp.pad(w, ((row0, LANE - row0 - w.shape[0]), (0, 0))))

    zero = jnp.zeros((e,), F32)
    y0 = b0 = None
    for z in range(2):
        vec = jnp.stack([p['w0'][z], p['a0'][z], p['k_k'], p['k_a'], p['r_k'].reshape(e),
                         p['v0'] if vres else zero, p['ln_w'], p['ln_b']])
        out = _rwkv_scan(proj, tail, lora(p['w2'][z], 64 * z), lora(p['a2'][z], 64 * z),
                         lora(p['v2'], 0) if vres else None, vec, proj0, y0, b0,
                         reverse=(z == 1), vres=vres, last=(z == 1), nxb=nxb)
        if z == 0:
            y0, b0 = out
    x_new = _out_residual(out, _bf(w_out), x, g_post, mod, nxb=nxb, nblk_out=nblk_out,
                          next_norm=next_norm)
    return x_new, proj


def _hgrn_scan_kernel(*refs, reverse, last, layer):
    it = iter(refs)
    q_ref, f_ref, i_ref, lb_ref = (next(it) for _ in range(4))
    if last:
        gate_ref, o0_ref, gn_ref, u_ref = (next(it) for _ in range(4))
    else:
        oo_ref = next(it)
    s_ref, gc_s, q_s, k_s, v_s, qg_s, kd_s, pt_s, o_s = it

    i = pl.program_id(2)
    tb, lg = q_s.shape
    nh = lg // HGRN_HEAD
    c = HGRN_SUB
    nsub = tb // c

    @pl.when(i == 0)
    def _():
        s_ref[...] = jnp.zeros_like(s_ref)

    logits = lb_ref[...]
    ex = jnp.exp(logits - jnp.max(logits, axis=0, keepdims=True))
    lb = jnp.sum(ex[1:layer + 1], axis=0, keepdims=True) / jnp.sum(ex, axis=0, keepdims=True)
    f = lb + (1.0 - lb) * _sigmoid(f_ref[0])
    g = jnp.log2(f)
    r1 = lax.broadcasted_iota(jnp.int32, (tb, tb), 0)
    c1 = lax.broadcasted_iota(jnp.int32, (tb, tb), 1)
    same = (r1 // c) == (c1 // c)
    before = (c1 >= r1) if reverse else (c1 <= r1)
    masks = jnp.concatenate([jnp.where(same & before, 1.0, 0.0), jnp.where(same, 1.0, 0.0)],
                            axis=0).astype(BF16)
    sums = _mask_dot(masks, g)
    gc, gtot = sums[:tb], sums[tb:]
    q = _silu(q_ref[0].astype(F32))
    kk = 1.0 - f
    gc_s[...] = gc
    q_s[...] = q
    k_s[...] = kk
    v_s[...] = i_ref[0].astype(F32)
    qg_s[...] = q * jnp.exp2(gc)
    kd_s[...] = kk * jnp.exp2(gtot - gc)
    pt_s[...] = jnp.exp2(gtot)

    trow = lax.broadcasted_iota(jnp.int32, (c, 1), 0)

    sls = [slice(hd * HGRN_HEAD, (hd + 1) * HGRN_HEAD) for hd in range(nh)]
    first, second = (slice(8, 16), slice(0, 8)) if reverse else (slice(0, 8), slice(8, 16))

    lane8 = lax.broadcasted_iota(jnp.int32, (8, HGRN_HEAD), 1)

    def pairwise(gcj, qj, kj):
        halves = [jnp.zeros((8, HGRN_HEAD), F32), jnp.zeros((8, HGRN_HEAD), F32)]
        for s in range(c):
            half = s // 8
            hs = slice(8 * half, 8 * half + 8)
            dlt = gcj[hs] - gcj[s:s + 1]
            if s != (8 * half + 7 if reverse else 8 * half):
                th = trow[hs]
                dlt = jnp.where((th <= s) if reverse else (th >= s), dlt, -1e30)
            w = qj[hs] * jnp.exp2(dlt) * kj[s:s + 1]
            halves[half] = jnp.where(lane8 == s, jnp.sum(w, axis=-1, keepdims=True), halves[half])
        return jnp.concatenate(halves, axis=0)

    def sub(jj, carry):
        subs = []
        for q in range(HGRN_UNROLL):
            step = jj * HGRN_UNROLL + q
            ji = (nsub - 1 - step) if reverse else step
            rows = pl.ds(pl.multiple_of(ji * c, c), c)
            subs.append(dict(rows=rows, gcs=[gc_s[rows, sl] for sl in sls],
                             qs=[q_s[rows, sl] for sl in sls], ks=[k_s[rows, sl] for sl in sls],
                             vs=[v_s[rows, sl] for sl in sls]))
        for d in subs:
            d['atts'] = []
            for gcj, qj, kj in zip(d['gcs'], d['qs'], d['ks']):
                gb = gcj[8:9] if reverse else gcj[7:8]
                qx = qj[second] * jnp.exp2(gcj[second] - gb)
                kx = kj[first] * jnp.exp2(gb - gcj[first])
                z8 = jnp.zeros_like(qx)
                q16 = jnp.concatenate([qx, z8] if reverse else [z8, qx], axis=0)
                k16 = jnp.concatenate([z8, kx] if reverse else [kx, z8], axis=0)
                d['atts'].append(_dot_nt(_bf(q16), _bf(k16)))
            d['upds'] = [_dot_tn(_bf(vj), _bf(kd_s[d['rows'], sl])) for sl, vj in zip(sls, d['vs'])]
        sts = [s_ref[hd] for hd in range(nh)]
        for d in subs:
            d['o_state'] = [_dot_nt(_bf(qg_s[d['rows'], sl]), _bf(st)) for sl, st in zip(sls, sts)]
            sts = [sts[hd] * pt_s[d['rows'], sls[hd]][0:1] + d['upds'][hd] for hd in range(nh)]
        for hd in range(nh):
            s_ref[hd] = sts[hd]
        for d in subs:
            for hd in range(nh):
                att = pairwise(d['gcs'][hd], d['qs'][hd], d['ks'][hd])[:, :c] + d['atts'][hd]
                o_s[d['rows'], sls[hd]] = d['o_state'][hd] + _dot(_bf(att), _bf(d['vs'][hd]))
        return carry

    lax.fori_loop(0, nsub // HGRN_UNROLL, sub, 0)

    if not last:
        oo_ref[0] = o_s[...]
    else:
        o = o0_ref[0] + o_s[...]
        gate = gate_ref[0].astype(F32)
        gn = gn_ref[...]
        for hd in range(nh):
            sl = slice(hd * HGRN_HEAD, (hd + 1) * HGRN_HEAD)
            oh = o[:, sl]
            ms = jnp.mean(oh * oh, axis=-1, keepdims=True)
            u_ref[0, :, sl] = (oh * lax.rsqrt(ms + NORM_EPS) * gn[:, sl] * _silu(gate[:, sl])).astype(BF16)


def _hgrn_scan(proj_qig, proj_f, lb_logits, gn, o0, *, reverse, last, nxb, layer):
    b, t, e3 = proj_qig.shape
    e = e3 // 3
    tb = TOKEN_BLOCK
    lg = HGRN_LANES
    ng = e // lg
    nblk = t // tb
    assert nblk == nxb + 1, "the context prefix must be exactly one token block"
    if reverse:
        blk = lambda i: jnp.where(i == 0, nxb, nxb - i)
    else:
        blk = lambda i: jnp.where(i == 0, nxb, i - 1)

    def col(off):
        return pl.BlockSpec((1, tb, lg), lambda bb, g, i: (bb, blk(i), off * ng + g))

    row = pl.BlockSpec((1, lg), lambda bb, g, i: (0, g))
    act = pl.BlockSpec((1, tb, lg), lambda bb, g, i: (bb, blk(i), g))
    in_specs = [col(0), col(1 if reverse else 0), col(1),
                pl.BlockSpec((lb_logits.shape[0], lg), lambda bb, g, i: (0, g))]
    args = [proj_qig, proj_f, proj_qig, lb_logits]
    if last:
        in_specs += [col(2), act, row]
        args += [proj_qig, o0, gn]
        out_shape = jax.ShapeDtypeStruct((b, t, e), BF16)
    else:
        out_shape = jax.ShapeDtypeStruct((b, t, e), F32)
    scratch = [pltpu.VMEM((lg // HGRN_HEAD, HGRN_HEAD, HGRN_HEAD), F32)] + [pltpu.VMEM((tb, lg), F32)] * 8
    return pl.pallas_call(
        functools.partial(_hgrn_scan_kernel, reverse=reverse, last=last, layer=layer),
        grid=(b, ng, nblk),
        in_specs=in_specs, out_specs=act, out_shape=out_shape,
        scratch_shapes=scratch,
        compiler_params=_cparams("parallel", "parallel", "arbitrary"),
        name="hgrn_scan_bwd" if reverse else "hgrn_scan_fwd",
    )(*args)


def _hgrn_layer(x, mod, g_pre, g_post, w_out, p, lb_logits, *, nxb, nblk_out, layer, h=None):
    b, t, d = x.shape
    e = lb_logits.shape[1]
    if h is None:
        h = _pre_norm(x, g_pre, mod, shift=False, nxb=nxb)
    h2 = h.reshape(b * t, d)
    w_in = p['w_in']
    w_qig = _bf(jnp.concatenate([w_in[:, :e], w_in[:, 3 * e:]], axis=1))
    proj_f = _project(h2, _bf(w_in[:, e:3 * e]), out_dtype=F32).reshape(b, t, 2 * e)
    proj_qig = _project(h2, w_qig, out_dtype=BF16).reshape(b, t, 3 * e)
    gn = jnp.tile(p['g_norm'], e // HGRN_HEAD)[None]
    o0 = _hgrn_scan(proj_qig, proj_f, lb_logits, gn, None, reverse=False, last=False, nxb=nxb,
                    layer=layer)
    u = _hgrn_scan(proj_qig, proj_f, lb_logits, gn, o0, reverse=True, last=True, nxb=nxb,
                   layer=layer)
    return _out_residual(u, _bf(w_out), x, g_post, mod, nxb=nxb, nblk_out=nblk_out)


def _hy_pre_kernel(*refs, nxb, nblk):
    cur = refs[0:4]
    prv = refs[4:7]
    nxt = refs[7:10]
    w = refs[10:13]
    bias = refs[13:16]
    u_ref, g0_ref = refs[16], refs[17]
    i = pl.program_id(1)
    tb = cur[0].shape[1]
    row = lax.broadcasted_iota(jnp.int32, (tb, 1), 0)
    has_prev = jnp.logical_and(i != 0, i != nxb)
    has_next = jnp.logical_and(i != nxb - 1, i != nblk - 1)

    hr = prv[0].shape[1]

    def conv(j):
        x = cur[j][0].astype(F32)
        before = prv[j][0].astype(F32)[hr - 1:hr]
        after = nxt[j][0].astype(F32)[0:1]
        up = jnp.where(row == 0, jnp.where(has_prev, before, 0.0), pltpu.roll(x, 1, 0))
        dn = jnp.where(row == tb - 1, jnp.where(has_next, after, 0.0), pltpu.roll(x, tb - 1, 0))
        wj = w[j][...]
        return wj[0:1] * up + wj[1:2] * x + wj[2:3] * dn + bias[j][...]

    u_ref[0] = (conv(2) * conv(1)).astype(u_ref.dtype)
    g0_ref[0] = (conv(0) * _silu(cur[3][0].astype(F32))).astype(g0_ref.dtype)


def _hy_pre(proj, conv_w, conv_b, *, nxb):
    b, t, e4 = proj.shape
    e = e4 // 4
    tb = TOKEN_BLOCK
    lg = 512
    ng = e // lg
    nblk = t // tb
    hr = 16
    hb = tb // hr
    nh = t // hr

    def col(off):
        return pl.BlockSpec((1, tb, lg), lambda bb, i, g: (bb, i, off * ng + g))

    def halo_prev(off):
        return pl.BlockSpec((1, hr, lg), lambda bb, i, g: (bb, jnp.maximum(i * hb - 1, 0), off * ng + g))

    def halo_next(off):
        return pl.BlockSpec((1, hr, lg), lambda bb, i, g: (bb, jnp.minimum((i + 1) * hb, nh - 1), off * ng + g))

    def wcol(rows, off):
        return pl.BlockSpec((rows, lg), lambda bb, i, g: (0, off * ng + g))

    in_specs = ([col(o) for o in range(4)] + [halo_prev(o) for o in range(3)]
                + [halo_next(o) for o in range(3)] + [wcol(3, o) for o in range(3)]
                + [wcol(1, o) for o in range(3)])
    args = [proj] * 10 + [conv_w] * 3 + [conv_b[None]] * 3
    act = pl.BlockSpec((1, tb, lg), lambda bb, i, g: (bb, i, g))
    return pl.pallas_call(
        functools.partial(_hy_pre_kernel, nxb=nxb, nblk=nblk),
        grid=(b, nblk, ng),
        in_specs=in_specs, out_specs=[act, act],
        out_shape=[jax.ShapeDtypeStruct((b, t, e), BF16)] * 2,
        compiler_params=_cparams("parallel", "parallel", "parallel"),
        name="hyena_short_conv",
    )(*args)


def _hy_filter_kernel(fv_ref, w1_ref, b1_ref, w2_ref, b2_ref, w3_ref, b3_ref, sf_ref,
                      w4f_ref, w4b_ref, dl_ref, hf_ref, hb_ref, *, length):
    tl = hf_ref.shape[0]
    n = (pl.program_id(0) * tl + lax.broadcasted_iota(jnp.int32, (tl, 1), 0)).astype(F32)
    t = n * (1.0 / (length - 1))
    lane = lax.broadcasted_iota(jnp.int32, (tl, LANE), 1)
    nb = (HYENA_EMB - 1) // 2
    ang = (2.0 * math.pi / length) * n * fv_ref[...]
    z = jnp.where(lane == 0, t,
                  jnp.where(lane <= nb, jnp.cos(ang),
                            jnp.where(lane <= 2 * nb, -jnp.sin(ang), 0.0)))
    sf = sf_ref[...]
    hdn = jnp.sin(sf * (_dot(z, w1_ref[...], HIGHEST) + b1_ref[...]))
    hdn = jnp.sin(sf * (_dot(hdn, w2_ref[...], HIGHEST) + b2_ref[...]))
    hdn = jnp.sin(sf * (_dot(hdn, w3_ref[...], HIGHEST) + b3_ref[...]))
    window = jnp.exp(-t * dl_ref[...])
    hf_ref[...] = _dot(hdn, w4f_ref[...], HIGHEST) * window
    hb_ref[...] = _dot(hdn, w4b_ref[...], HIGHEST) * window


def _hy_filters(length, p, e):
    tl = min(length, 1024)
    lg = 512
    ng = e // lg
    nb = (HYENA_EMB - 1) // 2
    freqs = np.linspace(1e-4, nb - 1, nb, dtype=np.float32)
    fv = np.zeros((1, LANE), np.float32)
    fv[0, 1:1 + nb] = freqs
    fv[0, 1 + nb:1 + 2 * nb] = freqs
    deltas = np.abs(np.linspace(math.log(HYENA_TARGET) / HYENA_SLOW_DECAY,
                                math.log(HYENA_TARGET) / HYENA_FAST_DECAY, e, dtype=np.float32))[None]
    fw = HYENA_FILTER_WIDTH
    w1 = jnp.pad(p['f_w1'], ((0, LANE - HYENA_EMB), (0, 0)))
    full = lambda shape: pl.BlockSpec(shape, lambda r, g: (0, 0))
    in_specs = [full((1, LANE)), full((LANE, fw)), full((1, fw)), full((fw, fw)), full((1, fw)),
                full((fw, fw)), full((1, fw)), full((1, fw)),
                pl.BlockSpec((fw, lg), lambda r, g: (0, g)),
                pl.BlockSpec((fw, lg), lambda r, g: (0, ng + g)),
                pl.BlockSpec((1, lg), lambda r, g: (0, g))]
    out = pl.BlockSpec((tl, lg), lambda r, g: (r, g))
    return pl.pallas_call(
        functools.partial(_hy_filter_kernel, length=length),
        grid=(length // tl, ng),
        in_specs=in_specs, out_specs=[out, out],
        out_shape=[jax.ShapeDtypeStruct((length, e), F32)] * 2,
        compiler_params=_cparams("parallel", "parallel"),
        name="hyena_filters",
    )(jnp.asarray(fv), w1, p['f_b1'][None], p['f_w2'], p['f_b2'][None], p['f_w3'], p['f_b3'][None],
      p['sin_freq'][None], p['f_w4'], p['f_w4'], jnp.asarray(deltas))


def _cmul(x, h, half):
    xr, xi = x[:half], x[half:]
    hr, hi = h[:half], h[half:]
    return jnp.concatenate([xr * hr - xi * hi, xr * hi + xi * hr], axis=0)


def _conj(x, half):
    return jnp.concatenate([x[:half], -x[half:]], axis=0)


def _tdot(tab, x, idx=None):
    get = (lambda ref: ref[...]) if idx is None else (lambda ref: ref[idx])
    xh = x.astype(BF16)
    out = _dot(get(tab[0]), xh)
    if len(tab) == 2:
        xl = (x - xh.astype(F32)).astype(BF16)
        out = out + _dot(get(tab[0]), xl) + _dot(get(tab[1]), xh)
    return out


def _split_tables(refs, count):
    per = len(FFT_TABLE_PARTS)
    return [refs[i * per:(i + 1) * per] for i in range(count)], refs[count * per:]


def _long_conv_direct_kernel(u_ref, g0_ref, hf_ref, hb_ref, fb_ref, *rest):
    (fd_t, fi_t), (o_ref, h_s) = _split_tables(rest, 2)
    half = fd_t[0].shape[0] // 2

    @pl.when(pl.program_id(1) == 0)
    def _():
        h_s[...] = _tdot(fd_t, hf_ref[...]) + _conj(_tdot(fd_t, hb_ref[...]), half)

    u = u_ref[0].astype(F32)
    y = _tdot(fi_t, _cmul(_tdot(fd_t, u), h_s[...], half))
    o_ref[0] = ((y + u * fb_ref[...]) * g0_ref[0].astype(F32)).astype(o_ref.dtype)


def _long_conv_kernel(u_ref, g0_ref, hf_ref, hb_ref, fb_ref, *rest):
    (t1_t, t1t_t, f2_t, f2t_t), (o_ref, x_s, a_s, h_s) = _split_tables(rest, 4)
    n2, rows1, n1h = t1_t[0].shape
    n1 = rows1 // 2
    un = min(FFT_UNROLL, n1, n2)
    assert n1 % un == 0 and n2 % un == 0
    xp = _odd_pitch(n2)
    ap = _odd_pitch(rows1)

    def a_block(m2):
        return pl.ds(pl.multiple_of(m2 * ap, 8), rows1)

    def stage1(src):
        def copy(m1, carry):
            x_s[pl.ds(pl.multiple_of(m1 * xp, 8), n2), :] = src(pl.ds(pl.multiple_of(m1 * n2, n2), n2))
            return carry
        lax.fori_loop(0, n1h, copy, 0, unroll=4)

        def body(j, carry):
            m2s = [j * un + q for q in range(un)]
            xs = [x_s[pl.ds(m2, n1h, stride=xp), :] for m2 in m2s]
            outs = [_tdot(t1_t, x, m2) for m2, x in zip(m2s, xs)]
            for m2, o in zip(m2s, outs):
                a_s[a_block(m2), :] = o
            return carry
        lax.fori_loop(0, n2 // un, body, 0)

    def stage2(k1s):
        zs = [jnp.concatenate([a_s[pl.ds(k1, n2, stride=ap), :],
                               a_s[pl.ds(n1 + k1, n2, stride=ap), :]], axis=0) for k1 in k1s]
        return [_tdot(f2_t, z) for z in zs]

    def spec_rows(k1):
        return pl.ds(pl.multiple_of(k1 * 2 * n2, 2 * n2), 2 * n2)

    @pl.when(pl.program_id(1) == 0)
    def _():
        stage1(lambda rows: hf_ref[rows, :])

        def spec_f(j, carry):
            k1s = [j * un + q for q in range(un)]
            for k1, x in zip(k1s, stage2(k1s)):
                h_s[spec_rows(k1), :] = x
            return carry
        lax.fori_loop(0, n1 // un, spec_f, 0)
        stage1(lambda rows: hb_ref[rows, :])

        def spec_b(j, carry):
            k1s = [j * un + q for q in range(un)]
            for k1, x in zip(k1s, stage2(k1s)):
                h_s[spec_rows(k1), :] = h_s[spec_rows(k1), :] + _conj(x, n2)
            return carry
        lax.fori_loop(0, n1 // un, spec_b, 0)

    stage1(lambda rows: u_ref[0, rows, :].astype(F32))

    def mid(j, carry):
        k1s = [j * un + q for q in range(un)]
        ys = [_cmul(x, h_s[spec_rows(k1), :], n2) for k1, x in zip(k1s, stage2(k1s))]
        zs = [_tdot(f2t_t, y) for y in ys]
        for k1, z in zip(k1s, zs):
            a_s[pl.ds(k1, n2, stride=ap), :] = z[:n2]
            a_s[pl.ds(n1 + k1, n2, stride=ap), :] = z[n2:]
        return carry
    lax.fori_loop(0, n1 // un, mid, 0)

    def inv1(j, carry):
        m2s = [j * un + q for q in range(un)]
        ys = [_tdot(t1t_t, a_s[a_block(m2), :], m2) for m2 in m2s]
        for m2, y in zip(m2s, ys):
            x_s[pl.ds(m2, n1h, stride=xp), :] = y
        return carry
    lax.fori_loop(0, n2 // un, inv1, 0)

    fb = fb_ref[...]

    def finish(m1, carry):
        rows = pl.ds(pl.multiple_of(m1 * n2, n2), n2)
        y = x_s[pl.ds(pl.multiple_of(m1 * xp, 8), n2), :]
        out = (y + u_ref[0, rows, :].astype(F32) * fb) * g0_ref[0, rows, :].astype(F32)
        o_ref[0, rows, :] = out.astype(o_ref.dtype)
        return carry
    lax.fori_loop(0, n1h, finish, 0, unroll=4)


def _odd_pitch(rows):
    return rows if (rows // 8) % 2 else rows + 8


def _fft_tables(length):
    n = 2 * length
    n2 = FFT_N2
    n1 = n // n2
    n1h = n1 // 2
    k1 = np.arange(n1, dtype=np.float64)[:, None]
    m1 = np.arange(n1h, dtype=np.float64)[None, :]
    t1 = []
    for m2 in range(n2):
        phi = 2.0 * np.pi * (k1 * m1 / n1 + k1 * m2 / n)
        t1.append(np.concatenate([np.cos(phi), -np.sin(phi)], axis=0))
    t1 = np.stack(t1)
    t1t = np.transpose(t1, (0, 2, 1)) / n
    k2 = np.arange(n2, dtype=np.float64)[:, None]
    m2 = np.arange(n2, dtype=np.float64)[None, :]
    th = 2.0 * np.pi * k2 * m2 / n2
    mr, mi = np.cos(th), -np.sin(th)
    f2 = np.block([[mr, -mi], [mi, mr]])
    return _hi_lo(t1) + _hi_lo(t1t) + _hi_lo(f2) + _hi_lo(f2.T)


def _hi_lo(a):
    a32 = jnp.asarray(a.astype(np.float32))
    hi = a32.astype(BF16)
    parts = {"hi": hi, "lo": (a32 - hi.astype(F32)).astype(BF16)}
    return [parts[name] for name in FFT_TABLE_PARTS]


def _dft_tables(length):
    n = 2 * length
    k = np.arange(n, dtype=np.float64)[:, None]
    m = np.arange(length, dtype=np.float64)[None, :]
    phi = 2.0 * np.pi * k * m / n
    fd = np.concatenate([np.cos(phi), -np.sin(phi)], axis=0)
    return _hi_lo(fd) + _hi_lo(fd.T / n)


def _long_conv(u, g0, hf, hb, fbias, *, length, blk_index):
    b, t, e = u.shape
    lg = FFT_LANES
    ng = e // lg
    seq_spec = pl.BlockSpec((1, length, lg), lambda g, bb: (bb, blk_index, g))
    filt_spec = pl.BlockSpec((length, lg), lambda g, bb: (0, g))
    row_spec = pl.BlockSpec((1, lg), lambda g, bb: (0, g))
    out_spec = pl.BlockSpec((1, length, lg), lambda g, bb: (bb, 0, g))
    common = dict(
        grid=(ng, b),
        out_specs=out_spec,
        out_shape=jax.ShapeDtypeStruct((b, length, e), BF16),
        compiler_params=_cparams("parallel", "arbitrary"),
    )
    const = lambda a: pl.BlockSpec(a.shape, lambda g, bb: (0,) * a.ndim)
    if length <= TOKEN_BLOCK:
        tabs = _dft_tables(length)
        return pl.pallas_call(
            _long_conv_direct_kernel,
            in_specs=[seq_spec, seq_spec, filt_spec, filt_spec, row_spec] + [const(a) for a in tabs],
            scratch_shapes=[pltpu.VMEM((4 * length, lg), F32)],
            name="hyena_long_conv_ctx", **common,
        )(u, g0, hf, hb, fbias, *tabs)
    tabs = _fft_tables(length)
    n2 = FFT_N2
    n1 = 2 * length // n2
    return pl.pallas_call(
        _long_conv_kernel,
        in_specs=[seq_spec, seq_spec, filt_spec, filt_spec, row_spec] + [const(a) for a in tabs],
        scratch_shapes=[pltpu.VMEM((n1 // 2 * _odd_pitch(n2), lg), F32),
                        pltpu.VMEM((n2 * _odd_pitch(2 * n1), lg), F32),
                        pltpu.VMEM((4 * length, lg), F32)],
        name="hyena_long_conv", **common,
    )(u, g0, hf, hb, fbias, *tabs)


def _hyena_layer(x, mod, g_pre, g_post, w_out, p, *, nxb, nblk_out, h=None, next_norm=None):
    b, t, d = x.shape
    e = p['filter_bias'].shape[0]
    seq = nxb * TOKEN_BLOCK
    ctx_len = t - seq
    if h is None:
        h = _pre_norm(x, g_pre, mod, shift=False, nxb=nxb)
    proj = _project(h.reshape(b * t, d), _bf(p['w_in']), out_dtype=BF16).reshape(b, t, 4 * e)
    u, g0 = _hy_pre(proj, p['conv_w'], p['conv_b'], nxb=nxb)
    fbias = p['filter_bias'][None]
    hf, hb = _hy_filters(seq, p, e)
    yx = _long_conv(u, g0, hf, hb, fbias, length=seq, blk_index=0)
    hf, hb = _hy_filters(ctx_len, p, e)
    yc = _long_conv(u, g0, hf, hb, fbias, length=ctx_len, blk_index=seq // ctx_len)
    y = jnp.concatenate([yx, yc], axis=1)
    return _out_residual(y, _bf(w_out), x, g_post, mod, nxb=nxb, nblk_out=nblk_out,
                         next_norm=next_norm)


def _modulation(c, c_ctx, ada_w, ada_b):
    b, d = c.shape
    depth = ada_w.shape[0]
    rows = -(-(b + 1) // 8) * 8
    cstack = jnp.zeros((rows, d), F32).at[:b].set(c).at[b].set(c_ctx)
    ada = _ada_all(cstack, ada_w, ada_b).reshape(depth, rows, 3, d)
    lat = ada[:, :b]
    cx = jnp.broadcast_to(ada[:, b:b + 1], lat.shape)
    return jnp.stack([lat, cx], axis=2)


def kernel(x, c, ctx, c_ctx, ada_w, ada_b, norm_pre, norm_post, w_out,
           l0_w_in, l0_mu, l0_w0, l0_w2, l0_a0, l0_a2, l0_k_k, l0_k_a, l0_r_k, l0_ln_w, l0_ln_b,
           l1_w_in, l1_conv_w, l1_conv_b, l1_f_w1, l1_f_b1, l1_f_w2, l1_f_b2, l1_f_w3, l1_f_b3,
           l1_f_w4, l1_sin_freq, l1_filter_bias,
           l2_w_in, l2_g_norm, hgrn_lb_logits,
           l3_w_in, l3_mu, l3_w0, l3_w2, l3_a0, l3_a2, l3_k_k, l3_k_a, l3_r_k, l3_ln_w, l3_ln_b,
           l3_v0, l3_v2):
    rwkv0 = dict(w_in=l0_w_in, mu=l0_mu, w0=l0_w0, w2=l0_w2, a0=l0_a0, a2=l0_a2, k_k=l0_k_k,
                 k_a=l0_k_a, r_k=l0_r_k, ln_w=l0_ln_w, ln_b=l0_ln_b)
    hyena1 = dict(w_in=l1_w_in, conv_w=l1_conv_w, conv_b=l1_conv_b, f_w1=l1_f_w1, f_b1=l1_f_b1,
                  f_w2=l1_f_w2, f_b2=l1_f_b2, f_w3=l1_f_w3, f_b3=l1_f_b3, f_w4=l1_f_w4,
                  sin_freq=l1_sin_freq, filter_bias=l1_filter_bias)
    hgrn2 = dict(w_in=l2_w_in, g_norm=l2_g_norm)
    rwkv3 = dict(w_in=l3_w_in, mu=l3_mu, w0=l3_w0, w2=l3_w2, a0=l3_a0, a2=l3_a2, k_k=l3_k_k,
                 k_a=l3_k_a, r_k=l3_r_k, ln_w=l3_ln_w, ln_b=l3_ln_b, v0=l3_v0, v2=l3_v2)
    b, seq, d = x.shape
    assert ctx.shape[1] == TOKEN_BLOCK and seq % TOKEN_BLOCK == 0
    nxb = seq // TOKEN_BLOCK
    mods = _modulation(c, c_ctx, ada_w, ada_b)
    xa = jnp.concatenate([x, ctx], axis=1)
    (xa, h1), proj0 = _rwkv_layer(xa, mods[0], norm_pre[0][None], norm_post[0][None], w_out[0],
                                  rwkv0, None, nxb=nxb, nblk_out=nxb + 1,
                                  next_norm=(norm_pre[1][None], mods[1]))
    xa, h2 = _hyena_layer(xa, mods[1], norm_pre[1][None], norm_post[1][None], w_out[1], hyena1,
                          nxb=nxb, nblk_out=nxb + 1, h=h1, next_norm=(norm_pre[2][None], mods[2]))
    xa = _hgrn_layer(xa, mods[2], norm_pre[2][None], norm_post[2][None], w_out[2], hgrn2,
                     hgrn_lb_logits, nxb=nxb, nblk_out=nxb + 1, layer=2, h=h2)
    xa, _ = _rwkv_layer(xa, mods[3], norm_pre[3][None], norm_post[3][None], w_out[3], rwkv3,
                        proj0, nxb=nxb, nblk_out=nxb)
    return xa
```

```python
import functools
import math

import jax
import jax.numpy as jnp
import numpy as np
from jax import lax
from jax.experimental import pallas as pl
from jax.experimental.pallas import tpu as pltpu

F32 = jnp.float32
BF16 = jnp.bfloat16
HIGHEST = lax.Precision.HIGHEST

NORM_EPS = 1e-6
GRID_W = 64
TOKEN_BLOCK = 256
LANE = 128
VMEM_LIMIT = 56 * 1024 * 1024

RWKV_HEAD = 64
RWKV_CHUNK = 64
RWKV_LANES = 1024
RWKV_GN_EPS = 64e-5
TAIL_WIDTH = 384

HGRN_HEAD = 128
HGRN_SUB = 16
HGRN_LANES = 1024
HGRN_UNROLL = 16

HYENA_EMB = 33
HYENA_FILTER_WIDTH = 64
HYENA_FAST_DECAY = 0.3
HYENA_SLOW_DECAY = 1.5
HYENA_TARGET = 1e-2
FFT_N2 = 64
FFT_LANES = 128
FFT_TABLE_PARTS = ("hi",)
FFT_UNROLL = 16


def _cparams(*sem):
    return pltpu.CompilerParams(dimension_semantics=sem, vmem_limit_bytes=VMEM_LIMIT)


def _dot(a, b, precision=None):
    return jnp.dot(a, b, preferred_element_type=F32, precision=precision)


def _dot_nt(a, b, precision=None):
    return lax.dot_general(a, b, (((1,), (1,)), ((), ())),
                           preferred_element_type=F32, precision=precision)


def _dot_tn(a, b):
    return lax.dot_general(a, b, (((0,), (0,)), ((), ())), preferred_element_type=F32)


def _bf(x):
    return x.astype(BF16)


def _split_dot(x, w_bf16):
    hi = x.astype(BF16)
    lo = (x - hi.astype(F32)).astype(BF16)
    return _dot(hi, w_bf16) + _dot(lo, w_bf16)


def _mask_dot(m_bf16, x):
    hi = x.astype(BF16)
    lo = (x - hi.astype(F32)).astype(BF16)
    return _dot(m_bf16, hi) + _dot(m_bf16, lo)


def _tree_sum(terms):
    terms = list(terms)
    while len(terms) > 1:
        terms = [a + b for a, b in zip(terms[0::2], terms[1::2])] + (terms[-1:] if len(terms) % 2 else [])
    return terms[0]


def _sigmoid(x):
    return 1.0 / (1.0 + jnp.exp(-x))


def _silu(x):
    return x * _sigmoid(x)


def _ada_kernel(c_ref, w_ref, b_ref, o_ref):
    o_ref[0] = _dot(_silu(c_ref[...]), w_ref[0], HIGHEST) + b_ref[0]


def _ada_all(cstack, ada_w, ada_b):
    depth, d, d3 = ada_w.shape
    rows = cstack.shape[0]
    nt = d3 // d
    return pl.pallas_call(
        _ada_kernel,
        grid=(depth, nt),
        in_specs=[
            pl.BlockSpec((rows, d), lambda l, j: (0, 0)),
            pl.BlockSpec((1, d, d), lambda l, j: (l, 0, j)),
            pl.BlockSpec((1, 1, d), lambda l, j: (l, 0, j)),
        ],
        out_specs=pl.BlockSpec((1, rows, d), lambda l, j: (l, 0, j)),
        out_shape=jax.ShapeDtypeStruct((depth, rows, d3), F32),
        compiler_params=_cparams("parallel", "parallel"),
        name="adaln",
    )(cstack, ada_w, ada_b.reshape(depth, 1, d3))


def _norm_kernel(*refs, shift, nxb):
    if shift:
        x_ref, xp_ref, xn_ref, g_ref, mod_ref, h_ref, d_ref = refs
    else:
        x_ref, g_ref, mod_ref, h_ref = refs
    i = pl.program_id(1)
    g = g_ref[...]
    shift_v = mod_ref[0, 0, 0:1, :]
    scale1p = 1.0 + mod_ref[0, 0, 1:2, :]

    def nrm(x):
        ms = jnp.mean(x * x, axis=-1, keepdims=True)
        return x * lax.rsqrt(ms + NORM_EPS) * g * scale1p + shift_v

    h = nrm(x_ref[0])
    h_ref[0] = h.astype(BF16)
    if not shift:
        return
    tb, d = h.shape
    q = d // 4
    row = lax.broadcasted_iota(jnp.int32, (tb, 1), 0)

    @pl.when(i < nxb)
    def _():
        col = row % GRID_W
        left = jnp.where(col > 0, pltpu.roll(h[:, 0:q], 1, 0), 0.0)
        right = jnp.where(col < GRID_W - 1, pltpu.roll(h[:, q:2 * q], tb - 1, 0), 0.0)
        hp = nrm(xp_ref[0])[:, 2 * q:3 * q]
        hn = nrm(xn_ref[0])[:, 3 * q:]
        hp = jnp.where(i > 0, hp, 0.0)
        hn = jnp.where(i < nxb - 1, hn, 0.0)
        up = jnp.concatenate([hp, h[:tb - GRID_W, 2 * q:3 * q]], axis=0)
        down = jnp.concatenate([h[GRID_W:, 3 * q:], hn], axis=0)
        hs = jnp.concatenate([left, right, up, down], axis=-1)
        d_ref[0] = (hs - h).astype(BF16)

    @pl.when(i >= nxb)
    def _():
        half = d // 2
        prev = jnp.where(row > 0, pltpu.roll(h[:, :half], 1, 0), 0.0)
        nxt = jnp.where(row < tb - 1, pltpu.roll(h[:, half:], tb - 1, 0), 0.0)
        hs = jnp.concatenate([prev, nxt], axis=-1)
        d_ref[0] = (hs - h).astype(BF16)


def _pre_norm(x, g, mod, *, shift, nxb):
    b, t, d = x.shape
    tb = TOKEN_BLOCK
    nblk = t // tb
    hb = tb // GRID_W
    nhalo = t // GRID_W
    seg = lambda i: jnp.where(i < nxb, 0, 1)
    x_spec = pl.BlockSpec((1, tb, d), lambda bb, i: (bb, i, 0))
    g_spec = pl.BlockSpec((1, d), lambda bb, i: (0, 0))
    mod_spec = pl.BlockSpec((1, 1, 3, d), lambda bb, i: (bb, seg(i), 0, 0))
    out_spec = pl.BlockSpec((1, tb, d), lambda bb, i: (bb, i, 0))
    if shift:
        in_specs = [
            x_spec,
            pl.BlockSpec((1, GRID_W, d), lambda bb, i: (bb, jnp.maximum(i * hb - 1, 0), 0)),
            pl.BlockSpec((1, GRID_W, d), lambda bb, i: (bb, jnp.minimum((i + 1) * hb, nhalo - 1), 0)),
            g_spec, mod_spec,
        ]
        args = (x, x, x, g, mod)
        out_specs = [out_spec, out_spec]
        out_shape = [jax.ShapeDtypeStruct((b, t, d), BF16)] * 2
    else:
        in_specs = [x_spec, g_spec, mod_spec]
        args = (x, g, mod)
        out_specs = out_spec
        out_shape = jax.ShapeDtypeStruct((b, t, d), BF16)
    return pl.pallas_call(
        functools.partial(_norm_kernel, shift=shift, nxb=nxb),
        grid=(b, nblk),
        in_specs=in_specs,
        out_specs=out_specs,
        out_shape=out_shape,
        compiler_params=_cparams("parallel", "parallel"),
        name="pre_norm_shift" if shift else "pre_norm",
    )(*args)


def _row_tile(m):
    for tm in (1024, 512, 256):
        if m % tm == 0:
            return tm
    raise ValueError(f"token count {m} is not a multiple of {TOKEN_BLOCK}")


def _proj_lerp_kernel(h_ref, d_ref, mu_ref, w_ref, o_ref, lhs_ref, *, tiles_per_group):
    j = pl.program_id(1)

    @pl.when(j % tiles_per_group == 0)
    def _():
        lhs_ref[...] = (h_ref[...].astype(F32) + mu_ref[0] * d_ref[...].astype(F32)).astype(BF16)

    o_ref[...] = _dot(lhs_ref[...], w_ref[...]).astype(o_ref.dtype)


def _proj_kernel(h_ref, w_ref, o_ref):
    o_ref[...] = _dot(h_ref[...], w_ref[...]).astype(o_ref.dtype)


def _project(h2, w_bf16, *, out_dtype, d2=None, mu=None, group_width=None):
    m, d = h2.shape
    n = w_bf16.shape[1]
    tm = _row_tile(m)
    tn = 1024
    lhs_spec = pl.BlockSpec((tm, d), lambda i, j: (i, 0))
    w_spec = pl.BlockSpec((d, tn), lambda i, j: (0, j))
    o_spec = pl.BlockSpec((tm, tn), lambda i, j: (i, j))
    if d2 is None:
        return pl.pallas_call(
            _proj_kernel, grid=(m // tm, n // tn),
            in_specs=[lhs_spec, w_spec], out_specs=o_spec,
            out_shape=jax.ShapeDtypeStruct((m, n), out_dtype),
            compiler_params=_cparams("parallel", "parallel"),
            name="project",
        )(h2, w_bf16)
    tpg = group_width // tn
    return pl.pallas_call(
        functools.partial(_proj_lerp_kernel, tiles_per_group=tpg),
        grid=(m // tm, n // tn),
        in_specs=[lhs_spec, lhs_spec,
                  pl.BlockSpec((1, 1, d), lambda i, j: (j // tpg, 0, 0)),
                  w_spec],
        out_specs=o_spec,
        out_shape=jax.ShapeDtypeStruct((m, n), out_dtype),
        scratch_shapes=[pltpu.VMEM((tm, d), BF16)],
        compiler_params=_cparams("parallel", "arbitrary"),
        name="project_lerp",
    )(h2, d2, mu, w_bf16)


def _tail_kernel(h_ref, d_ref, w_ref, mu_ref, o_ref):
    w = w_ref[...]
    o_ref[...] = _dot(h_ref[...], _bf(w)) + _dot(d_ref[...], _bf(w * mu_ref[...]))


def _project_tail(h2, d2, w_tail, mu_cols):
    m, d = h2.shape
    n = w_tail.shape[1]
    tm = _row_tile(m)
    lhs_spec = pl.BlockSpec((tm, d), lambda i: (i, 0))
    w_spec = pl.BlockSpec((d, n), lambda i: (0, 0))
    return pl.pallas_call(
        _tail_kernel, grid=(m // tm,),
        in_specs=[lhs_spec, lhs_spec, w_spec, w_spec],
        out_specs=pl.BlockSpec((tm, n), lambda i: (i, 0)),
        out_shape=jax.ShapeDtypeStruct((m, n), F32),
        compiler_params=_cparams("parallel"),
        name="project_tail",
    )(h2, d2, w_tail, mu_cols)


def _out_kernel(u_ref, w_ref, x_ref, g_ref, mod_ref, *rest):
    y = _dot(_bf(u_ref[0]), w_ref[...])
    ms = jnp.mean(y * y, axis=-1, keepdims=True)
    yn = y * lax.rsqrt(ms + NORM_EPS) * g_ref[...]
    x_new = x_ref[0] + yn * mod_ref[0, 0, 2:3, :]
    if len(rest) == 1:
        rest[0][0] = x_new
        return
    gn_ref, modn_ref, o_ref, h_ref = rest
    o_ref[0] = x_new
    msn = jnp.mean(x_new * x_new, axis=-1, keepdims=True)
    h = x_new * lax.rsqrt(msn + NORM_EPS) * gn_ref[...] * (1.0 + modn_ref[0, 0, 1:2, :]) + modn_ref[0, 0, 0:1, :]
    h_ref[0] = h.astype(BF16)


def _out_residual(u, w_bf16, x, g, mod, *, nxb, nblk_out, next_norm=None):
    b, t, e = u.shape
    d = x.shape[-1]
    tb = TOKEN_BLOCK
    seg = lambda i: jnp.where(i < nxb, 0, 1)
    act = pl.BlockSpec((1, tb, d), lambda bb, i: (bb, i, 0))
    gain = pl.BlockSpec((1, d), lambda bb, i: (0, 0))
    mods = pl.BlockSpec((1, 1, 3, d), lambda bb, i: (bb, seg(i), 0, 0))
    in_specs = [pl.BlockSpec((1, tb, e), lambda bb, i: (bb, i, 0)),
                pl.BlockSpec((e, d), lambda bb, i: (0, 0)),
                act, gain, mods]
    args = [u, w_bf16, x, g, mod]
    out_specs = act
    out_shape = jax.ShapeDtypeStruct((b, nblk_out * tb, d), F32)
    if next_norm is not None:
        in_specs += [gain, mods]
        args += list(next_norm)
        out_specs = [act, act]
        out_shape = [out_shape, jax.ShapeDtypeStruct((b, nblk_out * tb, d), BF16)]
    return pl.pallas_call(
        _out_kernel, grid=(b, nblk_out),
        in_specs=in_specs, out_specs=out_specs, out_shape=out_shape,
        compiler_params=_cparams("parallel", "parallel"),
        name="out_residual",
    )(*args)


def _rwkv_scan_kernel(*refs, reverse, vres, last):
    it = iter(refs)
    r_ref, k_ref, v_ref, lo_ref, w2_ref, a2_ref = (next(it) for _ in range(6))
    v2_ref = next(it) if vres else None
    vec_ref = next(it)
    vf_ref = next(it) if vres else None
    if last:
        gate_ref, y0_ref, b0_ref, u_ref = (next(it) for _ in range(4))
    else:
        yo_ref, bo_ref = next(it), next(it)
    s_ref, y_s, b_s = it

    i = pl.program_id(2)
    tb, lg = y_s.shape
    npair = lg // LANE
    c = RWKV_CHUNK
    nch = tb // c
    hd = RWKV_HEAD

    @pl.when(i == 0)
    def _():
        s_ref[...] = jnp.zeros_like(s_ref)

    vec = vec_ref[...]
    lo = lo_ref[0]
    r = r_ref[0].astype(F32)
    k = k_ref[0].astype(F32)
    v = v_ref[0].astype(F32)
    zt = vec[0:1] + _dot(_bf(jnp.tanh(lo[:, :LANE])), w2_ref[...])
    logw = -(math.exp(-0.5) * math.log2(math.e)) * _sigmoid(zt)
    a = _sigmoid(vec[1:2] + _dot(_bf(lo[:, LANE:2 * LANE]), a2_ref[...]))
    if vres:
        v = v + (vf_ref[0].astype(F32) - v) * _sigmoid(vec[5:6] + _dot(_bf(lo[:, 2 * LANE:]), v2_ref[...]))
    kdir = k * (1.0 + (a - 1.0) * vec[3:4])
    kkr = k * vec[2:3]
    bd = (lax.broadcasted_iota(jnp.int32, (LANE, LANE), 0) // hd
          == lax.broadcasted_iota(jnp.int32, (LANE, LANE), 1) // hd)
    ones_bd = jnp.where(bd, 1.0, 0.0).astype(BF16)

    def segsum(x):
        return jnp.concatenate(
            [_dot(_bf(x[:, p * LANE:(p + 1) * LANE]), ones_bd) for p in range(npair)], axis=-1)

    kk = kkr * lax.rsqrt(jnp.maximum(segsum(kkr * kkr), 1e-24))
    b_s[...] = segsum(r * kdir * vec[4:5]) * v
    alpha = -kk
    beta = kk * a

    t1 = lax.broadcasted_iota(jnp.int32, (c, c), 0)
    s1 = lax.broadcasted_iota(jnp.int32, (c, c), 1)
    t2 = lax.broadcasted_iota(jnp.int32, (c, LANE), 0)
    lane2 = lax.broadcasted_iota(jnp.int32, (c, LANE), 1)
    s2 = lane2 % c
    if reverse:
        incl1, incl2, strict2 = s1 >= t1, s2 >= t2, s2 > t2
    else:
        incl1, incl2, strict2 = s1 <= t1, s2 <= t2, s2 < t2
    tri = jnp.where(incl1, 1.0, 0.0).astype(BF16)
    low = lane2 < hd
    high = jnp.logical_not(low)
    low_x = lax.broadcasted_iota(jnp.int32, (2 * c, LANE), 1) < hd
    strict_lo, strict_hi = strict2 & low, strict2 & high
    incl_lo, incl_hi = incl2 & low, incl2 & high
    eye2 = jnp.where(lax.broadcasted_iota(jnp.int32, (LANE, LANE), 0)
                     == lax.broadcasted_iota(jnp.int32, (LANE, LANE), 1), 1.0, 0.0)
    zeros_cv = jnp.zeros((c, LANE), F32)

    def stack(top, bot):
        return jnp.concatenate([top, bot], axis=0)

    def fold(z):
        return z[:c] + z[c:]

    chains = {}
    rounds = int(math.log2(c)) - 1

    def st_decay(ci):
        rows = slice(ci * c, (ci + 1) * c)
        lw = logw[rows]
        lc = _mask_dot(tri, lw)
        ltot = lc[0:1] if reverse else lc[c - 1:c]
        p_inv = jnp.exp2(-lc)
        p_all = jnp.exp2(ltot)
        ab = alpha[rows] * jnp.exp2(lc - lw)
        rb = r[rows] * jnp.exp2(lc)
        bt = beta[rows] * p_inv
        kt = kdir[rows] * p_inv
        bh = bt * p_all
        kh = kt * p_all
        chains[ci] = []
        for p in range(npair):
            sl = slice(p * LANE, (p + 1) * LANE)
            chains[ci].append(dict(p=p, ab=ab[:, sl], rb=rb[:, sl], bt=bt[:, sl], kt=kt[:, sl],
                                   bh=bh[:, sl], kh=kh[:, sl], v=v[rows, sl], p_all=p_all[:, sl]))

    def st_gram(ci):
        for d in chains[ci]:
            y01 = jnp.concatenate([jnp.where(low_x, stack(d['bt'], d['kt']), 0.0),
                                   jnp.where(low_x, 0.0, stack(d['kt'], d['bt']))], axis=0)
            d['g'] = _dot_nt(_bf(stack(d['ab'], d['rb'])), _bf(y01))

    def st_blocks(ci):
        for d in chains[ci]:
            g0t, g0b = d['g'][:c, :LANE], d['g'][c:, :LANE]
            g1t, g1b = d['g'][:c, LANE:], d['g'][c:, LANE:]
            d['a'] = stack(jnp.where(strict_lo, g0t, 0.0), jnp.where(strict_hi, g1t, 0.0))
            arb = stack(jnp.where(incl_lo, g0b, 0.0), jnp.where(incl_hi, g1b, 0.0))
            ark = stack(jnp.where(incl_hi, g0b, 0.0), jnp.where(incl_lo, g1b, 0.0))
            d['arbk'] = _bf(jnp.concatenate([arb, ark], axis=1))
            ak = stack(jnp.where(strict_hi, g0t, 0.0), jnp.where(strict_lo, g1t, 0.0))
            d['vx'] = stack(jnp.where(high, d['v'], 0.0), jnp.where(low, d['v'], 0.0))
            d['w'] = _dot(_bf(ak), _bf(d['vx']))
            del d['g']

    def st_square(ci):
        for d in chains[ci]:
            d['tm'] = eye2 + d['a']
            apb = _bf(d['a'])
            d['a'] = _dot(apb, apb)

    def st_round(ci):
        for d in chains[ci]:
            pt = _dot(_bf(d['a']), _bf(jnp.concatenate([d['a'], d['tm']], axis=1)))
            d['a'] = pt[:, :LANE]
            d['tm'] = d['tm'] + pt[:, LANE:]

    def st_last_round(ci):
        for d in chains[ci]:
            d['tm'] = d['tm'] + _dot(_bf(d['a']), _bf(d['tm']))

    def st_apply(ci):
        for d in chains[ci]:
            ab_st = stack(jnp.where(low, d['ab'], 0.0), jnp.where(high, d['ab'], 0.0))
            d['tz'] = _dot(_bf(d['tm']), _bf(jnp.concatenate([ab_st, d['w']], axis=1)))

    def st_out(ci):
        for d in chains[ci]:
            lower = jnp.concatenate([jnp.zeros((2 * c, LANE), F32), d['vx']], axis=1)
            yz = _dot(d['arbk'], _bf(stack(d['tz'], lower)))
            ta, tw = fold(d['tz'][:, :LANE]), fold(d['tz'][:, LANE:])
            d['ra'] = _bf(d['rb'] + fold(yz[:, :LANE]))
            d['yw'] = fold(yz[:, LANE:])
            lhs = stack(jnp.concatenate([ta, tw], axis=1), jnp.concatenate([zeros_cv, d['v']], axis=1))
            mn = _dot_tn(_bf(lhs), _bf(stack(d['bh'], d['kh'])))
            d['m'] = _bf(jnp.where(bd, mn[:LANE], 0.0))
            d['n'] = jnp.where(bd, mn[LANE:], 0.0)

    state = [s_ref[p] for p in range(npair)]

    def st_recur(ci):
        for d in chains[ci]:
            p = d['p']
            sp = state[p]
            spb = _bf(sp)
            y_s[ci * c:(ci + 1) * c, p * LANE:(p + 1) * LANE] = _dot_nt(d['ra'], spb) + d['yw']
            state[p] = sp * d['p_all'] + _dot(spb, d['m']) + d['n']

    stages = ([st_decay, st_gram, st_blocks, st_square] + [st_round] * (rounds - 1)
              + [st_last_round, st_apply, st_out])
    for stage in stages:
        for ci in range(nch):
            stage(ci)
    for ci in (reversed(range(nch)) if reverse else range(nch)):
        st_recur(ci)
    for p in range(npair):
        s_ref[p] = state[p]

    bonus = b_s[...]
    if not last:
        yo_ref[0] = y_s[...]
        bo_ref[0] = bonus
    else:
        yy = y0_ref[0] + y_s[...]
        mean = segsum(yy) * (1.0 / hd)
        yc = yy - mean
        var = segsum(yc * yc) * (1.0 / hd)
        yn = yc * lax.rsqrt(var + RWKV_GN_EPS) * vec[6:7] + vec[7:8]
        u_ref[0] = ((yn + b0_ref[0] + bonus) * _silu(gate_ref[0].astype(F32))).astype(BF16)


def _rwkv_scan(proj, tail, w2p, a2p, v2p, vec, proj0, y0, b0, *, reverse, vres, last, nxb):
    b, t, e4 = proj.shape
    e = e4 // 4
    tb = TOKEN_BLOCK
    lg = RWKV_LANES
    ng = e // lg
    nblk = t // tb
    assert nblk == nxb + 1, "the context prefix must be exactly one token block"
    if reverse:
        blk = lambda i: jnp.where(i == 0, nxb, nxb - i)
    else:
        blk = lambda i: jnp.where(i == 0, nxb, i - 1)

    def col(off):
        return pl.BlockSpec((1, tb, lg), lambda bb, g, i: (bb, blk(i), off * ng + g))

    wspec = pl.BlockSpec((LANE, lg), lambda bb, g, i: (0, g))
    in_specs = [col(0), col(1), col(2),
                pl.BlockSpec((1, tb, TAIL_WIDTH), lambda bb, g, i: (bb, blk(i), 0)),
                wspec, wspec]
    args = [proj, proj, proj, tail, w2p, a2p]
    if vres:
        in_specs.append(wspec)
        args.append(v2p)
    in_specs.append(pl.BlockSpec((8, lg), lambda bb, g, i: (0, g)))
    args.append(vec)
    if vres:
        in_specs.append(col(2))
        args.append(proj0)
    act = pl.BlockSpec((1, tb, lg), lambda bb, g, i: (bb, blk(i), g))
    if last:
        in_specs += [col(3), act, act]
        args += [proj, y0, b0]
        out_specs = act
        out_shape = jax.ShapeDtypeStruct((b, t, e), BF16)
    else:
        out_specs = [act, act]
        out_shape = [jax.ShapeDtypeStruct((b, t, e), F32)] * 2
    scratch = [pltpu.VMEM((lg // LANE, LANE, LANE), F32)] + [pltpu.VMEM((tb, lg), F32)] * 2
    return pl.pallas_call(
        functools.partial(_rwkv_scan_kernel, reverse=reverse, vres=vres, last=last),
        grid=(b, ng, nblk),
        in_specs=in_specs, out_specs=out_specs, out_shape=out_shape,
        scratch_shapes=scratch,
        compiler_params=_cparams("parallel", "parallel", "arbitrary"),
        name="rwkv_scan_bwd" if reverse else "rwkv_scan_fwd",
    )(*args)


def _rwkv_layer(x, mod, g_pre, g_post, w_out, p, proj0, *, nxb, nblk_out, next_norm=None):
    b, t, d = x.shape
    e = p['k_k'].shape[0]
    vres = 'v0' in p
    h, dl = _pre_norm(x, g_pre, mod, shift=True, nxb=nxb)
    h2, d2 = h.reshape(b * t, d), dl.reshape(b * t, d)
    w_in = p['w_in']
    proj = _project(h2, _bf(w_in[:, :4 * e]), out_dtype=BF16, d2=d2, mu=p['mu'][:4, None, :],
                    group_width=e).reshape(b, t, 4 * e)
    n_lo = w_in.shape[1] - 4 * e
    groups = [4] * 128 + [5] * 128 + [2] * (n_lo - 256)
    pad = TAIL_WIDTH - n_lo
    w_tail = jnp.pad(w_in[:, 4 * e:], ((0, 0), (0, pad)))
    mu_cols = jnp.pad(p['mu'][np.asarray(groups)].T, ((0, 0), (0, pad)))
    tail = _project_tail(h2, d2, w_tail, mu_cols).reshape(b, t, TAIL_WIDTH)

    def lora(w, row0):
        return _bf(jnp.pad(w, ((row0, LANE - row0 - w.shape[0]), (0, 0))))

    zero = jnp.zeros((e,), F32)
    y0 = b0 = None
    for z in range(2):
        vec = jnp.stack([p['w0'][z], p['a0'][z], p['k_k'], p['k_a'], p['r_k'].reshape(e),
                         p['v0'] if vres else zero, p['ln_w'], p['ln_b']])
        out = _rwkv_scan(proj, tail, lora(p['w2'][z], 64 * z), lora(p['a2'][z], 64 * z),
                         lora(p['v2'], 0) if vres else None, vec, proj0, y0, b0,
                         reverse=(z == 1), vres=vres, last=(z == 1), nxb=nxb)
        if z == 0:
            y0, b0 = out
    x_new = _out_residual(out, _bf(w_out), x, g_post, mod, nxb=nxb, nblk_out=nblk_out,
                          next_norm=next_norm)
    return x_new, proj


def _hgrn_scan_kernel(*refs, reverse, last, layer):
    it = iter(refs)
    q_ref, f_ref, i_ref, lb_ref = (next(it) for _ in range(4))
    if last:
        gate_ref, o0_ref, gn_ref, u_ref = (next(it) for _ in range(4))
    else:
        oo_ref = next(it)
    s_ref, gc_s, q_s, k_s, v_s, qg_s, kd_s, pt_s, o_s = it

    i = pl.program_id(2)
    tb, lg = q_s.shape
    nh = lg // HGRN_HEAD
    c = HGRN_SUB
    nsub = tb // c

    @pl.when(i == 0)
    def _():
        s_ref[...] = jnp.zeros_like(s_ref)

    logits = lb_ref[...]
    ex = jnp.exp(logits - jnp.max(logits, axis=0, keepdims=True))
    lb = jnp.sum(ex[1:layer + 1], axis=0, keepdims=True) / jnp.sum(ex, axis=0, keepdims=True)
    f = lb + (1.0 - lb) * _sigmoid(f_ref[0])
    g = jnp.log2(f)
    r1 = lax.broadcasted_iota(jnp.int32, (tb, tb), 0)
    c1 = lax.broadcasted_iota(jnp.int32, (tb, tb), 1)
    same = (r1 // c) == (c1 // c)
    before = (c1 >= r1) if reverse else (c1 <= r1)
    masks = jnp.concatenate([jnp.where(same & before, 1.0, 0.0), jnp.where(same, 1.0, 0.0)],
                            axis=0).astype(BF16)
    sums = _mask_dot(masks, g)
    gc, gtot = sums[:tb], sums[tb:]
    q = _silu(q_ref[0].astype(F32))
    kk = 1.0 - f
    gc_s[...] = gc
    q_s[...] = q
    k_s[...] = kk
    v_s[...] = i_ref[0].astype(F32)
    qg_s[...] = q * jnp.exp2(gc)
    kd_s[...] = kk * jnp.exp2(gtot - gc)
    pt_s[...] = jnp.exp2(gtot)

    trow = lax.broadcasted_iota(jnp.int32, (c, 1), 0)

    sls = [slice(hd * HGRN_HEAD, (hd + 1) * HGRN_HEAD) for hd in range(nh)]
    first, second = (slice(8, 16), slice(0, 8)) if reverse else (slice(0, 8), slice(8, 16))

    lane8 = lax.broadcasted_iota(jnp.int32, (8, HGRN_HEAD), 1)

    def pairwise(gcj, qj, kj):
        halves = [jnp.zeros((8, HGRN_HEAD), F32), jnp.zeros((8, HGRN_HEAD), F32)]
        for s in range(c):
            half = s // 8
            hs = slice(8 * half, 8 * half + 8)
            dlt = gcj[hs] - gcj[s:s + 1]
            if s != (8 * half + 7 if reverse else 8 * half):
                th = trow[hs]
                dlt = jnp.where((th <= s) if reverse else (th >= s), dlt, -1e30)
            w = qj[hs] * jnp.exp2(dlt) * kj[s:s + 1]
            halves[half] = jnp.where(lane8 == s, jnp.sum(w, axis=-1, keepdims=True), halves[half])
        return jnp.concatenate(halves, axis=0)

    def sub(jj, carry):
        subs = []
        for q in range(HGRN_UNROLL):
            step = jj * HGRN_UNROLL + q
            ji = (nsub - 1 - step) if reverse else step
            rows = pl.ds(pl.multiple_of(ji * c, c), c)
            subs.append(dict(rows=rows, gcs=[gc_s[rows, sl] for sl in sls],
                             qs=[q_s[rows, sl] for sl in sls], ks=[k_s[rows, sl] for sl in sls],
                             vs=[v_s[rows, sl] for sl in sls]))
        for d in subs:
            d['atts'] = []
            for gcj, qj, kj in zip(d['gcs'], d['qs'], d['ks']):
                gb = gcj[8:9] if reverse else gcj[7:8]
                qx = qj[second] * jnp.exp2(gcj[second] - gb)
                kx = kj[first] * jnp.exp2(gb - gcj[first])
                z8 = jnp.zeros_like(qx)
                q16 = jnp.concatenate([qx, z8] if reverse else [z8, qx], axis=0)
                k16 = jnp.concatenate([z8, kx] if reverse else [kx, z8], axis=0)
                d['atts'].append(_dot_nt(_bf(q16), _bf(k16)))
            d['upds'] = [_dot_tn(_bf(vj), _bf(kd_s[d['rows'], sl])) for sl, vj in zip(sls, d['vs'])]
        sts = [s_ref[hd] for hd in range(nh)]
        for d in subs:
            d['o_state'] = [_dot_nt(_bf(qg_s[d['rows'], sl]), _bf(st)) for sl, st in zip(sls, sts)]
            sts = [sts[hd] * pt_s[d['rows'], sls[hd]][0:1] + d['upds'][hd] for hd in range(nh)]
        for hd in range(nh):
            s_ref[hd] = sts[hd]
        for d in subs:
            for hd in range(nh):
                att = pairwise(d['gcs'][hd], d['qs'][hd], d['ks'][hd])[:, :c] + d['atts'][hd]
                o_s[d['rows'], sls[hd]] = d['o_state'][hd] + _dot(_bf(att), _bf(d['vs'][hd]))
        return carry

    lax.fori_loop(0, nsub // HGRN_UNROLL, sub, 0)

    if not last:
        oo_ref[0] = o_s[...]
    else:
        o = o0_ref[0] + o_s[...]
        gate = gate_ref[0].astype(F32)
        gn = gn_ref[...]
        for hd in range(nh):
            sl = slice(hd * HGRN_HEAD, (hd + 1) * HGRN_HEAD)
            oh = o[:, sl]
            ms = jnp.mean(oh * oh, axis=-1, keepdims=True)
            u_ref[0, :, sl] = (oh * lax.rsqrt(ms + NORM_EPS) * gn[:, sl] * _silu(gate[:, sl])).astype(BF16)


def _hgrn_scan(proj_qig, proj_f, lb_logits, gn, o0, *, reverse, last, nxb, layer):
    b, t, e3 = proj_qig.shape
    e = e3 // 3
    tb = TOKEN_BLOCK
    lg = HGRN_LANES
    ng = e // lg
    nblk = t // tb
    assert nblk == nxb + 1, "the context prefix must be exactly one token block"
    if reverse:
        blk = lambda i: jnp.where(i == 0, nxb, nxb - i)
    else:
        blk = lambda i: jnp.where(i == 0, nxb, i - 1)

    def col(off):
        return pl.BlockSpec((1, tb, lg), lambda bb, g, i: (bb, blk(i), off * ng + g))

    row = pl.BlockSpec((1, lg), lambda bb, g, i: (0, g))
    act = pl.BlockSpec((1, tb, lg), lambda bb, g, i: (bb, blk(i), g))
    in_specs = [col(0), col(1 if reverse else 0), col(1),
                pl.BlockSpec((lb_logits.shape[0], lg), lambda bb, g, i: (0, g))]
    args = [proj_qig, proj_f, proj_qig, lb_logits]
    if last:
        in_specs += [col(2), act, row]
        args += [proj_qig, o0, gn]
        out_shape = jax.ShapeDtypeStruct((b, t, e), BF16)
    else:
        out_shape = jax.ShapeDtypeStruct((b, t, e), F32)
    scratch = [pltpu.VMEM((lg // HGRN_HEAD, HGRN_HEAD, HGRN_HEAD), F32)] + [pltpu.VMEM((tb, lg), F32)] * 8
    return pl.pallas_call(
        functools.partial(_hgrn_scan_kernel, reverse=reverse, last=last, layer=layer),
        grid=(b, ng, nblk),
        in_specs=in_specs, out_specs=act, out_shape=out_shape,
        scratch_shapes=scratch,
        compiler_params=_cparams("parallel", "parallel", "arbitrary"),
        name="hgrn_scan_bwd" if reverse else "hgrn_scan_fwd",
    )(*args)


def _hgrn_layer(x, mod, g_pre, g_post, w_out, p, lb_logits, *, nxb, nblk_out, layer, h=None):
    b, t, d = x.shape
    e = lb_logits.shape[1]
    if h is None:
        h = _pre_norm(x, g_pre, mod, shift=False, nxb=nxb)
    h2 = h.reshape(b * t, d)
    w_in = p['w_in']
    w_qig = _bf(jnp.concatenate([w_in[:, :e], w_in[:, 3 * e:]], axis=1))
    proj_f = _project(h2, _bf(w_in[:, e:3 * e]), out_dtype=F32).reshape(b, t, 2 * e)
    proj_qig = _project(h2, w_qig, out_dtype=BF16).reshape(b, t, 3 * e)
    gn = jnp.tile(p['g_norm'], e // HGRN_HEAD)[None]
    o0 = _hgrn_scan(proj_qig, proj_f, lb_logits, gn, None, reverse=False, last=False, nxb=nxb,
                    layer=layer)
    u = _hgrn_scan(proj_qig, proj_f, lb_logits, gn, o0, reverse=True, last=True, nxb=nxb,
                   layer=layer)
    return _out_residual(u, _bf(w_out), x, g_post, mod, nxb=nxb, nblk_out=nblk_out)


def _hy_pre_kernel(*refs, nxb, nblk):
    cur = refs[0:4]
    prv = refs[4:7]
    nxt = refs[7:10]
    w = refs[10:13]
    bias = refs[13:16]
    u_ref, g0_ref = refs[16], refs[17]
    i = pl.program_id(1)
    tb = cur[0].shape[1]
    row = lax.broadcasted_iota(jnp.int32, (tb, 1), 0)
    has_prev = jnp.logical_and(i != 0, i != nxb)
    has_next = jnp.logical_and(i != nxb - 1, i != nblk - 1)

    hr = prv[0].shape[1]

    def conv(j):
        x = cur[j][0].astype(F32)
        before = prv[j][0].astype(F32)[hr - 1:hr]
        after = nxt[j][0].astype(F32)[0:1]
        up = jnp.where(row == 0, jnp.where(has_prev, before, 0.0), pltpu.roll(x, 1, 0))
        dn = jnp.where(row == tb - 1, jnp.where(has_next, after, 0.0), pltpu.roll(x, tb - 1, 0))
        wj = w[j][...]
        return wj[0:1] * up + wj[1:2] * x + wj[2:3] * dn + bias[j][...]

    u_ref[0] = (conv(2) * conv(1)).astype(u_ref.dtype)
    g0_ref[0] = (conv(0) * _silu(cur[3][0].astype(F32))).astype(g0_ref.dtype)


def _hy_pre(proj, conv_w, conv_b, *, nxb):
    b, t, e4 = proj.shape
    e = e4 // 4
    tb = TOKEN_BLOCK
    lg = 512
    ng = e // lg
    nblk = t // tb
    hr = 16
    hb = tb // hr
    nh = t // hr

    def col(off):
        return pl.BlockSpec((1, tb, lg), lambda bb, i, g: (bb, i, off * ng + g))

    def halo_prev(off):
        return pl.BlockSpec((1, hr, lg), lambda bb, i, g: (bb, jnp.maximum(i * hb - 1, 0), off * ng + g))

    def halo_next(off):
        return pl.BlockSpec((1, hr, lg), lambda bb, i, g: (bb, jnp.minimum((i + 1) * hb, nh - 1), off * ng + g))

    def wcol(rows, off):
        return pl.BlockSpec((rows, lg), lambda bb, i, g: (0, off * ng + g))

    in_specs = ([col(o) for o in range(4)] + [halo_prev(o) for o in range(3)]
                + [halo_next(o) for o in range(3)] + [wcol(3, o) for o in range(3)]
                + [wcol(1, o) for o in range(3)])
    args = [proj] * 10 + [conv_w] * 3 + [conv_b[None]] * 3
    act = pl.BlockSpec((1, tb, lg), lambda bb, i, g: (bb, i, g))
    return pl.pallas_call(
        functools.partial(_hy_pre_kernel, nxb=nxb, nblk=nblk),
        grid=(b, nblk, ng),
        in_specs=in_specs, out_specs=[act, act],
        out_shape=[jax.ShapeDtypeStruct((b, t, e), BF16)] * 2,
        compiler_params=_cparams("parallel", "parallel", "parallel"),
        name="hyena_short_conv",
    )(*args)


def _hy_filter_kernel(fv_ref, w1_ref, b1_ref, w2_ref, b2_ref, w3_ref, b3_ref, sf_ref,
                      w4f_ref, w4b_ref, dl_ref, hf_ref, hb_ref, *, length):
    tl = hf_ref.shape[0]
    n = (pl.program_id(0) * tl + lax.broadcasted_iota(jnp.int32, (tl, 1), 0)).astype(F32)
    t = n * (1.0 / (length - 1))
    lane = lax.broadcasted_iota(jnp.int32, (tl, LANE), 1)
    nb = (HYENA_EMB - 1) // 2
    ang = (2.0 * math.pi / length) * n * fv_ref[...]
    z = jnp.where(lane == 0, t,
                  jnp.where(lane <= nb, jnp.cos(ang),
                            jnp.where(lane <= 2 * nb, -jnp.sin(ang), 0.0)))
    sf = sf_ref[...]
    hdn = jnp.sin(sf * (_dot(z, w1_ref[...], HIGHEST) + b1_ref[...]))
    hdn = jnp.sin(sf * (_dot(hdn, w2_ref[...], HIGHEST) + b2_ref[...]))
    hdn = jnp.sin(sf * (_dot(hdn, w3_ref[...], HIGHEST) + b3_ref[...]))
    window = jnp.exp(-t * dl_ref[...])
    hf_ref[...] = _dot(hdn, w4f_ref[...], HIGHEST) * window
    hb_ref[...] = _dot(hdn, w4b_ref[...], HIGHEST) * window


def _hy_filters(length, p, e):
    tl = min(length, 1024)
    lg = 512
    ng = e // lg
    nb = (HYENA_EMB - 1) // 2
    freqs = np.linspace(1e-4, nb - 1, nb, dtype=np.float32)
    fv = np.zeros((1, LANE), np.float32)
    fv[0, 1:1 + nb] = freqs
    fv[0, 1 + nb:1 + 2 * nb] = freqs
    deltas = np.abs(np.linspace(math.log(HYENA_TARGET) / HYENA_SLOW_DECAY,
                                math.log(HYENA_TARGET) / HYENA_FAST_DECAY, e, dtype=np.float32))[None]
    fw = HYENA_FILTER_WIDTH
    w1 = jnp.pad(p['f_w1'], ((0, LANE - HYENA_EMB), (0, 0)))
    full = lambda shape: pl.BlockSpec(shape, lambda r, g: (0, 0))
    in_specs = [full((1, LANE)), full((LANE, fw)), full((1, fw)), full((fw, fw)), full((1, fw)),
                full((fw, fw)), full((1, fw)), full((1, fw)),
                pl.BlockSpec((fw, lg), lambda r, g: (0, g)),
                pl.BlockSpec((fw, lg), lambda r, g: (0, ng + g)),
                pl.BlockSpec((1, lg), lambda r, g: (0, g))]
    out = pl.BlockSpec((tl, lg), lambda r, g: (r, g))
    return pl.pallas_call(
        functools.partial(_hy_filter_kernel, length=length),
        grid=(length // tl, ng),
        in_specs=in_specs, out_specs=[out, out],
        out_shape=[jax.ShapeDtypeStruct((length, e), F32)] * 2,
        compiler_params=_cparams("parallel", "parallel"),
        name="hyena_filters",
    )(jnp.asarray(fv), w1, p['f_b1'][None], p['f_w2'], p['f_b2'][None], p['f_w3'], p['f_b3'][None],
      p['sin_freq'][None], p['f_w4'], p['f_w4'], jnp.asarray(deltas))


def _cmul(x, h, half):
    xr, xi = x[:half], x[half:]
    hr, hi = h[:half], h[half:]
    return jnp.concatenate([xr * hr - xi * hi, xr * hi + xi * hr], axis=0)


def _conj(x, half):
    return jnp.concatenate([x[:half], -x[half:]], axis=0)


def _tdot(tab, x, idx=None):
    get = (lambda ref: ref[...]) if idx is None else (lambda ref: ref[idx])
    xh = x.astype(BF16)
    out = _dot(get(tab[0]), xh)
    if len(tab) == 2:
        xl = (x - xh.astype(F32)).astype(BF16)
        out = out + _dot(get(tab[0]), xl) + _dot(get(tab[1]), xh)
    return out


def _split_tables(refs, count):
    per = len(FFT_TABLE_PARTS)
    return [refs[i * per:(i + 1) * per] for i in range(count)], refs[count * per:]


def _long_conv_direct_kernel(u_ref, g0_ref, hf_ref, hb_ref, fb_ref, *rest):
    (fd_t, fi_t), (o_ref, h_s) = _split_tables(rest, 2)
    half = fd_t[0].shape[0] // 2

    @pl.when(pl.program_id(1) == 0)
    def _():
        h_s[...] = _tdot(fd_t, hf_ref[...]) + _conj(_tdot(fd_t, hb_ref[...]), half)

    u = u_ref[0].astype(F32)
    y = _tdot(fi_t, _cmul(_tdot(fd_t, u), h_s[...], half))
    o_ref[0] = ((y + u * fb_ref[...]) * g0_ref[0].astype(F32)).astype(o_ref.dtype)


def _long_conv_kernel(u_ref, g0_ref, hf_ref, hb_ref, fb_ref, *rest):
    (t1_t, t1t_t, f2_t, f2t_t), (o_ref, x_s, a_s, h_s) = _split_tables(rest, 4)
    n2, rows1, n1h = t1_t[0].shape
    n1 = rows1 // 2
    un = min(FFT_UNROLL, n1, n2)
    assert n1 % un == 0 and n2 % un == 0
    xp = _odd_pitch(n2)
    ap = _odd_pitch(rows1)

    def a_block(m2):
        return pl.ds(pl.multiple_of(m2 * ap, 8), rows1)

    def stage1(src):
        def copy(m1, carry):
            x_s[pl.ds(pl.multiple_of(m1 * xp, 8), n2), :] = src(pl.ds(pl.multiple_of(m1 * n2, n2), n2))
            return carry
        lax.fori_loop(0, n1h, copy, 0, unroll=4)

        def body(j, carry):
            m2s = [j * un + q for q in range(un)]
            xs = [x_s[pl.ds(m2, n1h, stride=xp), :] for m2 in m2s]
            outs = [_tdot(t1_t, x, m2) for m2, x in zip(m2s, xs)]
            for m2, o in zip(m2s, outs):
                a_s[a_block(m2), :] = o
            return carry
        lax.fori_loop(0, n2 // un, body, 0)

    def stage2(k1s):
        zs = [jnp.concatenate([a_s[pl.ds(k1, n2, stride=ap), :],
                               a_s[pl.ds(n1 + k1, n2, stride=ap), :]], axis=0) for k1 in k1s]
        return [_tdot(f2_t, z) for z in zs]

    def spec_rows(k1):
        return pl.ds(pl.multiple_of(k1 * 2 * n2, 2 * n2), 2 * n2)

    @pl.when(pl.program_id(1) == 0)
    def _():
        stage1(lambda rows: hf_ref[rows, :])

        def spec_f(j, carry):
            k1s = [j * un + q for q in range(un)]
            for k1, x in zip(k1s, stage2(k1s)):
                h_s[spec_rows(k1), :] = x
            return carry
        lax.fori_loop(0, n1 // un, spec_f, 0)
        stage1(lambda rows: hb_ref[rows, :])

        def spec_b(j, carry):
            k1s = [j * un + q for q in range(un)]
            for k1, x in zip(k1s, stage2(k1s)):
                h_s[spec_rows(k1), :] = h_s[spec_rows(k1), :] + _conj(x, n2)
            return carry
        lax.fori_loop(0, n1 // un, spec_b, 0)

    stage1(lambda rows: u_ref[0, rows, :].astype(F32))

    def mid(j, carry):
        k1s = [j * un + q for q in range(un)]
        ys = [_cmul(x, h_s[spec_rows(k1), :], n2) for k1, x in zip(k1s, stage2(k1s))]
        zs = [_tdot(f2t_t, y) for y in ys]
        for k1, z in zip(k1s, zs):
            a_s[pl.ds(k1, n2, stride=ap), :] = z[:n2]
            a_s[pl.ds(n1 + k1, n2, stride=ap), :] = z[n2:]
        return carry
    lax.fori_loop(0, n1 // un, mid, 0)

    def inv1(j, carry):
        m2s = [j * un + q for q in range(un)]
        ys = [_tdot(t1t_t, a_s[a_block(m2), :], m2) for m2 in m2s]
        for m2, y in zip(m2s, ys):
            x_s[pl.ds(m2, n1h, stride=xp), :] = y
        return carry
    lax.fori_loop(0, n2 // un, inv1, 0)

    fb = fb_ref[...]

    def finish(m1, carry):
        rows = pl.ds(pl.multiple_of(m1 * n2, n2), n2)
        y = x_s[pl.ds(pl.multiple_of(m1 * xp, 8), n2), :]
        out = (y + u_ref[0, rows, :].astype(F32) * fb) * g0_ref[0, rows, :].astype(F32)
        o_ref[0, rows, :] = out.astype(o_ref.dtype)
        return carry
    lax.fori_loop(0, n1h, finish, 0, unroll=4)


def _odd_pitch(rows):
    return rows if (rows // 8) % 2 else rows + 8


def _fft_tables(length):
    n = 2 * length
    n2 = FFT_N2
    n1 = n // n2
    n1h = n1 // 2
    k1 = np.arange(n1, dtype=np.float64)[:, None]
    m1 = np.arange(n1h, dtype=np.float64)[None, :]
    t1 = []
    for m2 in range(n2):
        phi = 2.0 * np.pi * (k1 * m1 / n1 + k1 * m2 / n)
        t1.append(np.concatenate([np.cos(phi), -np.sin(phi)], axis=0))
    t1 = np.stack(t1)
    t1t = np.transpose(t1, (0, 2, 1)) / n
    k2 = np.arange(n2, dtype=np.float64)[:, None]
    m2 = np.arange(n2, dtype=np.float64)[None, :]
    th = 2.0 * np.pi * k2 * m2 / n2
    mr, mi = np.cos(th), -np.sin(th)
    f2 = np.block([[mr, -mi], [mi, mr]])
    return _hi_lo(t1) + _hi_lo(t1t) + _hi_lo(f2) + _hi_lo(f2.T)


def _hi_lo(a):
    a32 = jnp.asarray(a.astype(np.float32))
    hi = a32.astype(BF16)
    parts = {"hi": hi, "lo": (a32 - hi.astype(F32)).astype(BF16)}
    return [parts[name] for name in FFT_TABLE_PARTS]


def _dft_tables(length):
    n = 2 * length
    k = np.arange(n, dtype=np.float64)[:, None]
    m = np.arange(length, dtype=np.float64)[None, :]
    phi = 2.0 * np.pi * k * m / n
    fd = np.concatenate([np.cos(phi), -np.sin(phi)], axis=0)
    return _hi_lo(fd) + _hi_lo(fd.T / n)


def _long_conv(u, g0, hf, hb, fbias, *, length, blk_index):
    b, t, e = u.shape
    lg = FFT_LANES
    ng = e // lg
    seq_spec = pl.BlockSpec((1, length, lg), lambda g, bb: (bb, blk_index, g))
    filt_spec = pl.BlockSpec((length, lg), lambda g, bb: (0, g))
    row_spec = pl.BlockSpec((1, lg), lambda g, bb: (0, g))
    out_spec = pl.BlockSpec((1, length, lg), lambda g, bb: (bb, 0, g))
    common = dict(
        grid=(ng, b),
        out_specs=out_spec,
        out_shape=jax.ShapeDtypeStruct((b, length, e), BF16),
        compiler_params=_cparams("parallel", "arbitrary"),
    )
    const = lambda a: pl.BlockSpec(a.shape, lambda g, bb: (0,) * a.ndim)
    if length <= TOKEN_BLOCK:
        tabs = _dft_tables(length)
        return pl.pallas_call(
            _long_conv_direct_kernel,
            in_specs=[seq_spec, seq_spec, filt_spec, filt_spec, row_spec] + [const(a) for a in tabs],
            scratch_shapes=[pltpu.VMEM((4 * length, lg), F32)],
            name="hyena_long_conv_ctx", **common,
        )(u, g0, hf, hb, fbias, *tabs)
    tabs = _fft_tables(length)
    n2 = FFT_N2
    n1 = 2 * length // n2
    return pl.pallas_call(
        _long_conv_kernel,
        in_specs=[seq_spec, seq_spec, filt_spec, filt_spec, row_spec] + [const(a) for a in tabs],
        scratch_shapes=[pltpu.VMEM((n1 // 2 * _odd_pitch(n2), lg), F32),
                        pltpu.VMEM((n2 * _odd_pitch(2 * n1), lg), F32),
                        pltpu.VMEM((4 * length, lg), F32)],
        name="hyena_long_conv", **common,
    )(u, g0, hf, hb, fbias, *tabs)


def _hyena_layer(x, mod, g_pre, g_post, w_out, p, *, nxb, nblk_out, h=None, next_norm=None):
    b, t, d = x.shape
    e = p['filter_bias'].shape[0]
    seq = nxb * TOKEN_BLOCK
    ctx_len = t - seq
    if h is None:
        h = _pre_norm(x, g_pre, mod, shift=False, nxb=nxb)
    proj = _project(h.reshape(b * t, d), _bf(p['w_in']), out_dtype=BF16).reshape(b, t, 4 * e)
    u, g0 = _hy_pre(proj, p['conv_w'], p['conv_b'], nxb=nxb)
    fbias = p['filter_bias'][None]
    hf, hb = _hy_filters(seq, p, e)
    yx = _long_conv(u, g0, hf, hb, fbias, length=seq, blk_index=0)
    hf, hb = _hy_filters(ctx_len, p, e)
    yc = _long_conv(u, g0, hf, hb, fbias, length=ctx_len, blk_index=seq // ctx_len)
    y = jnp.concatenate([yx, yc], axis=1)
    return _out_residual(y, _bf(w_out), x, g_post, mod, nxb=nxb, nblk_out=nblk_out,
                         next_norm=next_norm)


def _modulation(c, c_ctx, ada_w, ada_b):
    b, d = c.shape
    depth = ada_w.shape[0]
    rows = -(-(b + 1) // 8) * 8
    cstack = jnp.zeros((rows, d), F32).at[:b].set(c).at[b].set(c_ctx)
    ada = _ada_all(cstack, ada_w, ada_b).reshape(depth, rows, 3, d)
    lat = ada[:, :b]
    cx = jnp.broadcast_to(ada[:, b:b + 1], lat.shape)
    return jnp.stack([lat, cx], axis=2)


def kernel(x, c, ctx, c_ctx, ada_w, ada_b, norm_pre, norm_post, w_out,
           l0_w_in, l0_mu, l0_w0, l0_w2, l0_a0, l0_a2, l0_k_k, l0_k_a, l0_r_k, l0_ln_w, l0_ln_b,
           l1_w_in, l1_conv_w, l1_conv_b, l1_f_w1, l1_f_b1, l1_f_w2, l1_f_b2, l1_f_w3, l1_f_b3,
           l1_f_w4, l1_sin_freq, l1_filter_bias,
           l2_w_in, l2_g_norm, hgrn_lb_logits,
           l3_w_in, l3_mu, l3_w0, l3_w2, l3_a0, l3_a2, l3_k_k, l3_k_a, l3_r_k, l3_ln_w, l3_ln_b,
           l3_v0, l3_v2):
    rwkv0 = dict(w_in=l0_w_in, mu=l0_mu, w0=l0_w0, w2=l0_w2, a0=l0_a0, a2=l0_a2, k_k=l0_k_k,
                 k_a=l0_k_a, r_k=l0_r_k, ln_w=l0_ln_w, ln_b=l0_ln_b)
    hyena1 = dict(w_in=l1_w_in, conv_w=l1_conv_w, conv_b=l1_conv_b, f_w1=l1_f_w1, f_b1=l1_f_b1,
                  f_w2=l1_f_w2, f_b2=l1_f_b2, f_w3=l1_f_w3, f_b3=l1_f_b3, f_w4=l1_f_w4,
                  sin_freq=l1_sin_freq, filter_bias=l1_filter_bias)
    hgrn2 = dict(w_in=l2_w_in, g_norm=l2_g_norm)
    rwkv3 = dict(w_in=l3_w_in, mu=l3_mu, w0=l3_w0, w2=l3_w2, a0=l3_a0, a2=l3_a2, k_k=l3_k_k,
                 k_a=l3_k_a, r_k=l3_r_k, ln_w=l3_ln_w, ln_b=l3_ln_b, v0=l3_v0, v2=l3_v2)
    b, seq, d = x.shape
    assert ctx.shape[1] == TOKEN_BLOCK and seq % TOKEN_BLOCK == 0
    nxb = seq // TOKEN_BLOCK
    mods = _modulation(c, c_ctx, ada_w, ada_b)
    xa = jnp.concatenate([x, ctx], axis=1)
    (xa, h1), proj0 = _rwkv_layer(xa, mods[0], norm_pre[0][None], norm_post[0][None], w_out[0],
                                  rwkv0, None, nxb=nxb, nblk_out=nxb + 1,
                                  next_norm=(norm_pre[1][None], mods[1]))
    xa, h2 = _hyena_layer(xa, mods[1], norm_pre[1][None], norm_post[1][None], w_out[1], hyena1,
                          nxb=nxb, nblk_out=nxb + 1, h=h1, next_norm=(norm_pre[2][None], mods[2]))
    xa = _hgrn_layer(xa, mods[2], norm_pre[2][None], norm_post[2][None], w_out[2], hgrn2,
                     hgrn_lb_logits, nxb=nxb, nblk_out=nxb + 1, layer=2, h=h2)
    xa, _ = _rwkv_layer(xa, mods[3], norm_pre[3][None], norm_post[3][None], w_out[3], rwkv3,
                        proj0, nxb=nxb, nblk_out=nxb)
    return xa
```

```python
import functools
import math

import jax
import jax.numpy as jnp
import numpy as np
from jax import lax
from jax.experimental import pallas as pl
from jax.experimental.pallas import tpu as pltpu

F32 = jnp.float32
BF16 = jnp.bfloat16
HIGHEST = lax.Precision.HIGHEST

NORM_EPS = 1e-6
GRID_W = 64
TOKEN_BLOCK = 256
LANE = 128
VMEM_LIMIT = 56 * 1024 * 1024

RWKV_HEAD = 64
RWKV_CHUNK = 64
RWKV_LANES = 1024
RWKV_GN_EPS = 64e-5
TAIL_WIDTH = 384

HGRN_HEAD = 128
HGRN_SUB = 16
HGRN_LANES = 1024
HGRN_UNROLL = 16

HYENA_EMB = 33
HYENA_FILTER_WIDTH = 64
HYENA_FAST_DECAY = 0.3
HYENA_SLOW_DECAY = 1.5
HYENA_TARGET = 1e-2
FFT_N2 = 64
FFT_LANES = 128
FFT_TABLE_PARTS = ("hi",)
FFT_UNROLL = 16


def _cparams(*sem):
    return pltpu.CompilerParams(dimension_semantics=sem, vmem_limit_bytes=VMEM_LIMIT)


def _dot(a, b, precision=None):
    return jnp.dot(a, b, preferred_element_type=F32, precision=precision)


def _dot_nt(a, b, precision=None):
    return lax.dot_general(a, b, (((1,), (1,)), ((), ())),
                           preferred_element_type=F32, precision=precision)


def _dot_tn(a, b):
    return lax.dot_general(a, b, (((0,), (0,)), ((), ())), preferred_element_type=F32)


def _bf(x):
    return x.astype(BF16)


def _split_dot(x, w_bf16):
    hi = x.astype(BF16)
    lo = (x - hi.astype(F32)).astype(BF16)
    return _dot(hi, w_bf16) + _dot(lo, w_bf16)


def _mask_dot(m_bf16, x):
    hi = x.astype(BF16)
    lo = (x - hi.astype(F32)).astype(BF16)
    return _dot(m_bf16, hi) + _dot(m_bf16, lo)


def _tree_sum(terms):
    terms = list(terms)
    while len(terms) > 1:
        terms = [a + b for a, b in zip(terms[0::2], terms[1::2])] + (terms[-1:] if len(terms) % 2 else [])
    return terms[0]


def _sigmoid(x):
    return 1.0 / (1.0 + jnp.exp(-x))


def _silu(x):
    return x * _sigmoid(x)


def _ada_kernel(c_ref, w_ref, b_ref, o_ref):
    o_ref[0] = _dot(_silu(c_ref[...]), w_ref[0], HIGHEST) + b_ref[0]


def _ada_all(cstack, ada_w, ada_b):
    depth, d, d3 = ada_w.shape
    rows = cstack.shape[0]
    nt = d3 // d
    return pl.pallas_call(
        _ada_kernel,
        grid=(depth, nt),
        in_specs=[
            pl.BlockSpec((rows, d), lambda l, j: (0, 0)),
            pl.BlockSpec((1, d, d), lambda l, j: (l, 0, j)),
            pl.BlockSpec((1, 1, d), lambda l, j: (l, 0, j)),
        ],
        out_specs=pl.BlockSpec((1, rows, d), lambda l, j: (l, 0, j)),
        out_shape=jax.ShapeDtypeStruct((depth, rows, d3), F32),
        compiler_params=_cparams("parallel", "parallel"),
        name="adaln",
    )(cstack, ada_w, ada_b.reshape(depth, 1, d3))


def _norm_kernel(*refs, shift, nxb):
    if shift:
        x_ref, xp_ref, xn_ref, g_ref, mod_ref, h_ref, d_ref = refs
    else:
        x_ref, g_ref, mod_ref, h_ref = refs
    i = pl.program_id(1)
    g = g_ref[...]
    shift_v = mod_ref[0, 0, 0:1, :]
    scale1p = 1.0 + mod_ref[0, 0, 1:2, :]

    def nrm(x):
        ms = jnp.mean(x * x, axis=-1, keepdims=True)
        return x * lax.rsqrt(ms + NORM_EPS) * g * scale1p + shift_v

    h = nrm(x_ref[0])
    h_ref[0] = h.astype(BF16)
    if not shift:
        return
    tb, d = h.shape
    q = d // 4
    row = lax.broadcasted_iota(jnp.int32, (tb, 1), 0)

    @pl.when(i < nxb)
    def _():
        col = row % GRID_W
        left = jnp.where(col > 0, pltpu.roll(h[:, 0:q], 1, 0), 0.0)
        right = jnp.where(col < GRID_W - 1, pltpu.roll(h[:, q:2 * q], tb - 1, 0), 0.0)
        hp = nrm(xp_ref[0])[:, 2 * q:3 * q]
        hn = nrm(xn_ref[0])[:, 3 * q:]
        hp = jnp.where(i > 0, hp, 0.0)
        hn = jnp.where(i < nxb - 1, hn, 0.0)
        up = jnp.concatenate([hp, h[:tb - GRID_W, 2 * q:3 * q]], axis=0)
        down = jnp.concatenate([h[GRID_W:, 3 * q:], hn], axis=0)
        hs = jnp.concatenate([left, right, up, down], axis=-1)
        d_ref[0] = (hs - h).astype(BF16)

    @pl.when(i >= nxb)
    def _():
        half = d // 2
        prev = jnp.where(row > 0, pltpu.roll(h[:, :half], 1, 0), 0.0)
        nxt = jnp.where(row < tb - 1, pltpu.roll(h[:, half:], tb - 1, 0), 0.0)
        hs = jnp.concatenate([prev, nxt], axis=-1)
        d_ref[0] = (hs - h).astype(BF16)


def _pre_norm(x, g, mod, *, shift, nxb):
    b, t, d = x.shape
    tb = TOKEN_BLOCK
    nblk = t // tb
    hb = tb // GRID_W
    nhalo = t // GRID_W
    seg = lambda i: jnp.where(i < nxb, 0, 1)
    x_spec = pl.BlockSpec((1, tb, d), lambda bb, i: (bb, i, 0))
    g_spec = pl.BlockSpec((1, d), lambda bb, i: (0, 0))
    mod_spec = pl.BlockSpec((1, 1, 3, d), lambda bb, i: (bb, seg(i), 0, 0))
    out_spec = pl.BlockSpec((1, tb, d), lambda bb, i: (bb, i, 0))
    if shift:
        in_specs = [
            x_spec,
            pl.BlockSpec((1, GRID_W, d), lambda bb, i: (bb, jnp.maximum(i * hb - 1, 0), 0)),
            pl.BlockSpec((1, GRID_W, d), lambda bb, i: (bb, jnp.minimum((i + 1) * hb, nhalo - 1), 0)),
            g_spec, mod_spec,
        ]
        args = (x, x, x, g, mod)
        out_specs = [out_spec, out_spec]
        out_shape = [jax.ShapeDtypeStruct((b, t, d), BF16)] * 2
    else:
        in_specs = [x_spec, g_spec, mod_spec]
        args = (x, g, mod)
        out_specs = out_spec
        out_shape = jax.ShapeDtypeStruct((b, t, d), BF16)
    return pl.pallas_call(
        functools.partial(_norm_kernel, shift=shift, nxb=nxb),
        grid=(b, nblk),
        in_specs=in_specs,
        out_specs=out_specs,
        out_shape=out_shape,
        compiler_params=_cparams("parallel", "parallel"),
        name="pre_norm_shift" if shift else "pre_norm",
    )(*args)


def _row_tile(m):
    for tm in (1024, 512, 256):
        if m % tm == 0:
            return tm
    raise ValueError(f"token count {m} is not a multiple of {TOKEN_BLOCK}")


def _proj_lerp_kernel(h_ref, d_ref, mu_ref, w_ref, o_ref, lhs_ref, *, tiles_per_group):
    j = pl.program_id(1)

    @pl.when(j % tiles_per_group == 0)
    def _():
        lhs_ref[...] = (h_ref[...].astype(F32) + mu_ref[0] * d_ref[...].astype(F32)).astype(BF16)

    o_ref[...] = _dot(lhs_ref[...], w_ref[...]).astype(o_ref.dtype)


def _proj_kernel(h_ref, w_ref, o_ref):
    o_ref[...] = _dot(h_ref[...], w_ref[...]).astype(o_ref.dtype)


def _project(h2, w_bf16, *, out_dtype, d2=None, mu=None, group_width=None):
    m, d = h2.shape
    n = w_bf16.shape[1]
    tm = _row_tile(m)
    tn = 1024
    lhs_spec = pl.BlockSpec((tm, d), lambda i, j: (i, 0))
    w_spec = pl.BlockSpec((d, tn), lambda i, j: (0, j))
    o_spec = pl.BlockSpec((tm, tn), lambda i, j: (i, j))
    if d2 is None:
        return pl.pallas_call(
            _proj_kernel, grid=(m // tm, n // tn),
            in_specs=[lhs_spec, w_spec], out_specs=o_spec,
            out_shape=jax.ShapeDtypeStruct((m, n), out_dtype),
            compiler_params=_cparams("parallel", "parallel"),
            name="project",
        )(h2, w_bf16)
    tpg = group_width // tn
    return pl.pallas_call(
        functools.partial(_proj_lerp_kernel, tiles_per_group=tpg),
        grid=(m // tm, n // tn),
        in_specs=[lhs_spec, lhs_spec,
                  pl.BlockSpec((1, 1, d), lambda i, j: (j // tpg, 0, 0)),
                  w_spec],
        out_specs=o_spec,
        out_shape=jax.ShapeDtypeStruct((m, n), out_dtype),
        scratch_shapes=[pltpu.VMEM((tm, d), BF16)],
        compiler_params=_cparams("parallel", "arbitrary"),
        name="project_lerp",
    )(h2, d2, mu, w_bf16)


def _tail_kernel(h_ref, d_ref, w_ref, mu_ref, o_ref):
    w = w_ref[...]
    o_ref[...] = _dot(h_ref[...], _bf(w)) + _dot(d_ref[...], _bf(w * mu_ref[...]))


def _project_tail(h2, d2, w_tail, mu_cols):
    m, d = h2.shape
    n = w_tail.shape[1]
    tm = _row_tile(m)
    lhs_spec = pl.BlockSpec((tm, d), lambda i: (i, 0))
    w_spec = pl.BlockSpec((d, n), lambda i: (0, 0))
    return pl.pallas_call(
        _tail_kernel, grid=(m // tm,),
        in_specs=[lhs_spec, lhs_spec, w_spec, w_spec],
        out_specs=pl.BlockSpec((tm, n), lambda i: (i, 0)),
        out_shape=jax.ShapeDtypeStruct((m, n), F32),
        compiler_params=_cparams("parallel"),
        name="project_tail",
    )(h2, d2, w_tail, mu_cols)


def _out_kernel(u_ref, w_ref, x_ref, g_ref, mod_ref, *rest):
    y = _dot(_bf(u_ref[0]), w_ref[...])
    ms = jnp.mean(y * y, axis=-1, keepdims=True)
    yn = y * lax.rsqrt(ms + NORM_EPS) * g_ref[...]
    x_new = x_ref[0] + yn * mod_ref[0, 0, 2:3, :]
    if len(rest) == 1:
        rest[0][0] = x_new
        return
    gn_ref, modn_ref, o_ref, h_ref = rest
    o_ref[0] = x_new
    msn = jnp.mean(x_new * x_new, axis=-1, keepdims=True)
    h = x_new * lax.rsqrt(msn + NORM_EPS) * gn_ref[...] * (1.0 + modn_ref[0, 0, 1:2, :]) + modn_ref[0, 0, 0:1, :]
    h_ref[0] = h.astype(BF16)


def _out_residual(u, w_bf16, x, g, mod, *, nxb, nblk_out, next_norm=None):
    b, t, e = u.shape
    d = x.shape[-1]
    tb = TOKEN_BLOCK
    seg = lambda i: jnp.where(i < nxb, 0, 1)
    act = pl.BlockSpec((1, tb, d), lambda bb, i: (bb, i, 0))
    gain = pl.BlockSpec((1, d), lambda bb, i: (0, 0))
    mods = pl.BlockSpec((1, 1, 3, d), lambda bb, i: (bb, seg(i), 0, 0))
    in_specs = [pl.BlockSpec((1, tb, e), lambda bb, i: (bb, i, 0)),
                pl.BlockSpec((e, d), lambda bb, i: (0, 0)),
                act, gain, mods]
    args = [u, w_bf16, x, g, mod]
    out_specs = act
    out_shape = jax.ShapeDtypeStruct((b, nblk_out * tb, d), F32)
    if next_norm is not None:
        in_specs += [gain, mods]
        args += list(next_norm)
        out_specs = [act, act]
        out_shape = [out_shape, jax.ShapeDtypeStruct((b, nblk_out * tb, d), BF16)]
    return pl.pallas_call(
        _out_kernel, grid=(b, nblk_out),
        in_specs=in_specs, out_specs=out_specs, out_shape=out_shape,
        compiler_params=_cparams("parallel", "parallel"),
        name="out_residual",
    )(*args)


def _rwkv_scan_kernel(*refs, reverse, vres, last):
    it = iter(refs)
    r_ref, k_ref, v_ref, lo_ref, w2_ref, a2_ref = (next(it) for _ in range(6))
    v2_ref = next(it) if vres else None
    vec_ref = next(it)
    vf_ref = next(it) if vres else None
    if last:
        gate_ref, y0_ref, b0_ref, u_ref = (next(it) for _ in range(4))
    else:
        yo_ref, bo_ref = next(it), next(it)
    s_ref, y_s = it

    i = pl.program_id(2)
    tb, lg = y_s.shape
    npair = lg // LANE
    c = RWKV_CHUNK
    nch = tb // c
    hd = RWKV_HEAD

    @pl.when(i == 0)
    def _():
        s_ref[...] = jnp.zeros_like(s_ref)

    vec = vec_ref[...]
    lo = lo_ref[0]
    r = r_ref[0].astype(F32)
    k = k_ref[0].astype(F32)
    v = v_ref[0].astype(F32)
    zt = vec[0:1] + _dot(_bf(jnp.tanh(lo[:, :LANE])), w2_ref[...])
    logw = -(math.exp(-0.5) * math.log2(math.e)) * _sigmoid(zt)
    a = _sigmoid(vec[1:2] + _dot(_bf(lo[:, LANE:2 * LANE]), a2_ref[...]))
    if vres:
        v = v + (vf_ref[0].astype(F32) - v) * _sigmoid(vec[5:6] + _dot(_bf(lo[:, 2 * LANE:]), v2_ref[...]))
    kdir = k * (1.0 + (a - 1.0) * vec[3:4])
    kkr = k * vec[2:3]
    bd = (lax.broadcasted_iota(jnp.int32, (LANE, LANE), 0) // hd
          == lax.broadcasted_iota(jnp.int32, (LANE, LANE), 1) // hd)
    ones_bd = jnp.where(bd, 1.0, 0.0).astype(BF16)

    def segsum(x):
        return jnp.concatenate(
            [_dot(_bf(x[:, p * LANE:(p + 1) * LANE]), ones_bd) for p in range(npair)], axis=-1)

    kk = kkr * lax.rsqrt(jnp.maximum(segsum(kkr * kkr), 1e-24))
    alpha = -kk
    beta = kk * a

    t1 = lax.broadcasted_iota(jnp.int32, (c, c), 0)
    s1 = lax.broadcasted_iota(jnp.int32, (c, c), 1)
    t2 = lax.broadcasted_iota(jnp.int32, (c, LANE), 0)
    lane2 = lax.broadcasted_iota(jnp.int32, (c, LANE), 1)
    s2 = lane2 % c
    if reverse:
        incl1, incl2, strict2 = s1 >= t1, s2 >= t2, s2 > t2
    else:
        incl1, incl2, strict2 = s1 <= t1, s2 <= t2, s2 < t2
    tri = jnp.where(incl1, 1.0, 0.0).astype(BF16)
    low = lane2 < hd
    high = jnp.logical_not(low)
    low_x = lax.broadcasted_iota(jnp.int32, (2 * c, LANE), 1) < hd
    strict_lo, strict_hi = strict2 & low, strict2 & high
    incl_lo, incl_hi = incl2 & low, incl2 & high
    eye2 = jnp.where(lax.broadcasted_iota(jnp.int32, (LANE, LANE), 0)
                     == lax.broadcasted_iota(jnp.int32, (LANE, LANE), 1), 1.0, 0.0)
    zeros_cv = jnp.zeros((c, LANE), F32)

    def stack(top, bot):
        return jnp.concatenate([top, bot], axis=0)

    def fold(z):
        return z[:c] + z[c:]

    chains = {}
    rounds = int(math.log2(c)) - 1

    def st_decay(ci):
        rows = slice(ci * c, (ci + 1) * c)
        lw = logw[rows]
        lc = _mask_dot(tri, lw)
        ltot = lc[0:1] if reverse else lc[c - 1:c]
        p_inv = jnp.exp2(-lc)
        p_all = jnp.exp2(ltot)
        ab = alpha[rows] * jnp.exp2(lc - lw)
        rb = r[rows] * jnp.exp2(lc)
        bt = beta[rows] * p_inv
        kt = kdir[rows] * p_inv
        bh = bt * p_all
        kh = kt * p_all
        chains[ci] = []
        for p in range(npair):
            sl = slice(p * LANE, (p + 1) * LANE)
            chains[ci].append(dict(p=p, ab=ab[:, sl], rb=rb[:, sl], bt=bt[:, sl], kt=kt[:, sl],
                                   bh=bh[:, sl], kh=kh[:, sl], v=v[rows, sl], p_all=p_all[:, sl]))

    def st_gram(ci):
        for d in chains[ci]:
            y01 = jnp.concatenate([jnp.where(low_x, stack(d['bt'], d['kt']), 0.0),
                                   jnp.where(low_x, 0.0, stack(d['kt'], d['bt']))], axis=0)
            d['g'] = _dot_nt(_bf(stack(d['ab'], d['rb'])), _bf(y01))

    def st_blocks(ci):
        for d in chains[ci]:
            g0t, g0b = d['g'][:c, :LANE], d['g'][c:, :LANE]
            g1t, g1b = d['g'][:c, LANE:], d['g'][c:, LANE:]
            d['a'] = stack(jnp.where(strict_lo, g0t, 0.0), jnp.where(strict_hi, g1t, 0.0))
            arb = stack(jnp.where(incl_lo, g0b, 0.0), jnp.where(incl_hi, g1b, 0.0))
            ark = stack(jnp.where(incl_hi, g0b, 0.0), jnp.where(incl_lo, g1b, 0.0))
            d['arbk'] = _bf(jnp.concatenate([arb, ark], axis=1))
            ak = stack(jnp.where(strict_hi, g0t, 0.0), jnp.where(strict_lo, g1t, 0.0))
            d['vx'] = stack(jnp.where(high, d['v'], 0.0), jnp.where(low, d['v'], 0.0))
            d['w'] = _dot(_bf(ak), _bf(d['vx']))
            del d['g']

    def st_square(ci):
        for d in chains[ci]:
            d['tm'] = eye2 + d['a']
            apb = _bf(d['a'])
            d['a'] = _dot(apb, apb)

    def st_round(ci):
        for d in chains[ci]:
            pt = _dot(_bf(d['a']), _bf(jnp.concatenate([d['a'], d['tm']], axis=1)))
            d['a'] = pt[:, :LANE]
            d['tm'] = d['tm'] + pt[:, LANE:]

    def st_last_round(ci):
        for d in chains[ci]:
            d['tm'] = d['tm'] + _dot(_bf(d['a']), _bf(d['tm']))

    def st_apply(ci):
        for d in chains[ci]:
            ab_st = stack(jnp.where(low, d['ab'], 0.0), jnp.where(high, d['ab'], 0.0))
            d['tz'] = _dot(_bf(d['tm']), _bf(jnp.concatenate([ab_st, d['w']], axis=1)))

    def st_out(ci):
        for d in chains[ci]:
            lower = jnp.concatenate([jnp.zeros((2 * c, LANE), F32), d['vx']], axis=1)
            yz = _dot(d['arbk'], _bf(stack(d['tz'], lower)))
            ta, tw = fold(d['tz'][:, :LANE]), fold(d['tz'][:, LANE:])
            d['ra'] = _bf(d['rb'] + fold(yz[:, :LANE]))
            d['yw'] = fold(yz[:, LANE:])
            lhs = stack(jnp.concatenate([ta, tw], axis=1), jnp.concatenate([zeros_cv, d['v']], axis=1))
            mn = _dot_tn(_bf(lhs), _bf(stack(d['bh'], d['kh'])))
            d['m'] = _bf(jnp.where(bd, mn[:LANE], 0.0))
            d['n'] = jnp.where(bd, mn[LANE:], 0.0)

    state = [s_ref[p] for p in range(npair)]

    def st_recur(ci):
        for d in chains[ci]:
            p = d['p']
            sp = state[p]
            spb = _bf(sp)
            y_s[ci * c:(ci + 1) * c, p * LANE:(p + 1) * LANE] = _dot_nt(d['ra'], spb) + d['yw']
            state[p] = sp * d['p_all'] + _dot(spb, d['m']) + d['n']

    stages = ([st_decay, st_gram, st_blocks, st_square] + [st_round] * (rounds - 1)
              + [st_last_round, st_apply, st_out])
    for stage in stages:
        for ci in range(nch):
            stage(ci)
    for ci in (reversed(range(nch)) if reverse else range(nch)):
        st_recur(ci)
    for p in range(npair):
        s_ref[p] = state[p]

    bonus = segsum(r * kdir * vec[4:5]) * v
    if not last:
        yo_ref[0] = y_s[...]
        bo_ref[0] = bonus
    else:
        yy = y0_ref[0] + y_s[...]
        mean = segsum(yy) * (1.0 / hd)
        yc = yy - mean
        var = segsum(yc * yc) * (1.0 / hd)
        yn = yc * lax.rsqrt(var + RWKV_GN_EPS) * vec[6:7] + vec[7:8]
        u_ref[0] = ((yn + b0_ref[0] + bonus) * _silu(gate_ref[0].astype(F32))).astype(BF16)


def _rwkv_scan(proj, tail, w2p, a2p, v2p, vec, proj0, y0, b0, *, reverse, vres, last, nxb):
    b, t, e4 = proj.shape
    e = e4 // 4
    tb = TOKEN_BLOCK
    lg = RWKV_LANES
    ng = e // lg
    nblk = t // tb
    assert nblk == nxb + 1, "the context prefix must be exactly one token block"
    if reverse:
        blk = lambda i: jnp.where(i == 0, nxb, nxb - i)
    else:
        blk = lambda i: jnp.where(i == 0, nxb, i - 1)

    def col(off):
        return pl.BlockSpec((1, tb, lg), lambda bb, g, i: (bb, blk(i), off * ng + g))

    wspec = pl.BlockSpec((LANE, lg), lambda bb, g, i: (0, g))
    in_specs = [col(0), col(1), col(2),
                pl.BlockSpec((1, tb, TAIL_WIDTH), lambda bb, g, i: (bb, blk(i), 0)),
                wspec, wspec]
    args = [proj, proj, proj, tail, w2p, a2p]
    if vres:
        in_specs.append(wspec)
        args.append(v2p)
    in_specs.append(pl.BlockSpec((8, lg), lambda bb, g, i: (0, g)))
    args.append(vec)
    if vres:
        in_specs.append(col(2))
        args.append(proj0)
    act = pl.BlockSpec((1, tb, lg), lambda bb, g, i: (bb, blk(i), g))
    if last:
        in_specs += [col(3), act, act]
        args += [proj, y0, b0]
        out_specs = act
        out_shape = jax.ShapeDtypeStruct((b, t, e), BF16)
    else:
        out_specs = [act, act]
        out_shape = [jax.ShapeDtypeStruct((b, t, e), F32)] * 2
    scratch = [pltpu.VMEM((lg // LANE, LANE, LANE), F32), pltpu.VMEM((tb, lg), F32)]
    return pl.pallas_call(
        functools.partial(_rwkv_scan_kernel, reverse=reverse, vres=vres, last=last),
        grid=(b, ng, nblk),
        in_specs=in_specs, out_specs=out_specs, out_shape=out_shape,
        scratch_shapes=scratch,
        compiler_params=_cparams("parallel", "parallel", "arbitrary"),
        name="rwkv_scan_bwd" if reverse else "rwkv_scan_fwd",
    )(*args)


def _rwkv_layer(x, mod, g_pre, g_post, w_out, p, proj0, *, nxb, nblk_out, next_norm=None):
    b, t, d = x.shape
    e = p['k_k'].shape[0]
    vres = 'v0' in p
    h, dl = _pre_norm(x, g_pre, mod, shift=True, nxb=nxb)
    h2, d2 = h.reshape(b * t, d), dl.reshape(b * t, d)
    w_in = p['w_in']
    proj = _project(h2, _bf(w_in[:, :4 * e]), out_dtype=BF16, d2=d2, mu=p['mu'][:4, None, :],
                    group_width=e).reshape(b, t, 4 * e)
    n_lo = w_in.shape[1] - 4 * e
    groups = [4] * 128 + [5] * 128 + [2] * (n_lo - 256)
    pad = TAIL_WIDTH - n_lo
    w_tail = jnp.pad(w_in[:, 4 * e:], ((0, 0), (0, pad)))
    mu_cols = jnp.pad(p['mu'][np.asarray(groups)].T, ((0, 0), (0, pad)))
    tail = _project_tail(h2, d2, w_tail, mu_cols).reshape(b, t, TAIL_WIDTH)

    def lora(w, row0):
        return _bf(jnp.pad(w, ((row0, LANE - row0 - w.shape[0]), (0, 0))))

    zero = jnp.zeros((e,), F32)
    y0 = b0 = None
    for z in range(2):
        vec = jnp.stack([p['w0'][z], p['a0'][z], p['k_k'], p['k_a'], p['r_k'].reshape(e),
                         p['v0'] if vres else zero, p['ln_w'], p['ln_b']])
        out = _rwkv_scan(proj, tail, lora(p['w2'][z], 64 * z), lora(p['a2'][z], 64 * z),
                         lora(p['v2'], 0) if vres else None, vec, proj0, y0, b0,
                         reverse=(z == 1), vres=vres, last=(z == 1), nxb=nxb)
        if z == 0:
            y0, b0 = out
    x_new = _out_residual(out, _bf(w_out), x, g_post, mod, nxb=nxb, nblk_out=nblk_out,
                          next_norm=next_norm)
    return x_new, proj


def _hgrn_scan_kernel(*refs, reverse, last, layer):
    it = iter(refs)
    q_ref, f_ref, i_ref, lb_ref = (next(it) for _ in range(4))
    if last:
        gate_ref, o0_ref, gn_ref, u_ref = (next(it) for _ in range(4))
    else:
        oo_ref = next(it)
    s_ref, gc_s, q_s, k_s, v_s, qg_s, kd_s, pt_s, o_s = it

    i = pl.program_id(2)
    tb, lg = q_s.shape
    nh = lg // HGRN_HEAD
    c = HGRN_SUB
    nsub = tb // c

    @pl.when(i == 0)
    def _():
        s_ref[...] = jnp.zeros_like(s_ref)

    logits = lb_ref[...]
    ex = jnp.exp(logits - jnp.max(logits, axis=0, keepdims=True))
    lb = jnp.sum(ex[1:layer + 1], axis=0, keepdims=True) / jnp.sum(ex, axis=0, keepdims=True)
    f = lb + (1.0 - lb) * _sigmoid(f_ref[0])
    g = jnp.log2(f)
    r1 = lax.broadcasted_iota(jnp.int32, (tb, tb), 0)
    c1 = lax.broadcasted_iota(jnp.int32, (tb, tb), 1)
    same = (r1 // c) == (c1 // c)
    before = (c1 >= r1) if reverse else (c1 <= r1)
    masks = jnp.concatenate([jnp.where(same & before, 1.0, 0.0), jnp.where(same, 1.0, 0.0)],
                            axis=0).astype(BF16)
    sums = _mask_dot(masks, g)
    gc, gtot = sums[:tb], sums[tb:]
    q = _silu(q_ref[0].astype(F32))
    kk = 1.0 - f
    gc_s[...] = gc
    q_s[...] = q
    k_s[...] = kk
    v_s[...] = i_ref[0].astype(F32)
    qg_s[...] = q * jnp.exp2(gc)
    kd_s[...] = kk * jnp.exp2(gtot - gc)
    pt_s[...] = jnp.exp2(gtot)

    trow = lax.broadcasted_iota(jnp.int32, (c, 1), 0)

    sls = [slice(hd * HGRN_HEAD, (hd + 1) * HGRN_HEAD) for hd in range(nh)]
    first, second = (slice(8, 16), slice(0, 8)) if reverse else (slice(0, 8), slice(8, 16))

    lane8 = lax.broadcasted_iota(jnp.int32, (8, HGRN_HEAD), 1)

    def pairwise(gcj, qj, kj):
        halves = [jnp.zeros((8, HGRN_HEAD), F32), jnp.zeros((8, HGRN_HEAD), F32)]
        for s in range(c):
            half = s // 8
            hs = slice(8 * half, 8 * half + 8)
            dlt = gcj[hs] - gcj[s:s + 1]
            if s != (8 * half + 7 if reverse else 8 * half):
                th = trow[hs]
                dlt = jnp.where((th <= s) if reverse else (th >= s), dlt, -1e30)
            w = qj[hs] * jnp.exp2(dlt) * kj[s:s + 1]
            halves[half] = jnp.where(lane8 == s, jnp.sum(w, axis=-1, keepdims=True), halves[half])
        return jnp.concatenate(halves, axis=0)

    def sub(jj, carry):
        subs = []
        for q in range(HGRN_UNROLL):
            step = jj * HGRN_UNROLL + q
            ji = (nsub - 1 - step) if reverse else step
            rows = pl.ds(pl.multiple_of(ji * c, c), c)
            subs.append(dict(rows=rows, gcs=[gc_s[rows, sl] for sl in sls],
                             qs=[q_s[rows, sl] for sl in sls], ks=[k_s[rows, sl] for sl in sls],
                             vs=[v_s[rows, sl] for sl in sls]))
        for d in subs:
            d['atts'] = []
            for gcj, qj, kj in zip(d['gcs'], d['qs'], d['ks']):
                gb = gcj[8:9] if reverse else gcj[7:8]
                qx = qj[second] * jnp.exp2(gcj[second] - gb)
                kx = kj[first] * jnp.exp2(gb - gcj[first])
                z8 = jnp.zeros_like(qx)
                q16 = jnp.concatenate([qx, z8] if reverse else [z8, qx], axis=0)
                k16 = jnp.concatenate([z8, kx] if reverse else [kx, z8], axis=0)
                d['atts'].append(_dot_nt(_bf(q16), _bf(k16)))
            d['upds'] = [_dot_tn(_bf(vj), _bf(kd_s[d['rows'], sl])) for sl, vj in zip(sls, d['vs'])]
        sts = [s_ref[hd] for hd in range(nh)]
        for d in subs:
            d['o_state'] = [_dot_nt(_bf(qg_s[d['rows'], sl]), _bf(st)) for sl, st in zip(sls, sts)]
            sts = [sts[hd] * pt_s[d['rows'], sls[hd]][0:1] + d['upds'][hd] for hd in range(nh)]
        for hd in range(nh):
            s_ref[hd] = sts[hd]
        for d in subs:
            for hd in range(nh):
                att = pairwise(d['gcs'][hd], d['qs'][hd], d['ks'][hd])[:, :c] + d['atts'][hd]
                o_s[d['rows'], sls[hd]] = d['o_state'][hd] + _dot(_bf(att), _bf(d['vs'][hd]))
        return carry

    lax.fori_loop(0, nsub // HGRN_UNROLL, sub, 0)

    if not last:
        oo_ref[0] = o_s[...]
    else:
        o = o0_ref[0] + o_s[...]
        gate = gate_ref[0].astype(F32)
        gn = gn_ref[...]
        for hd in range(nh):
            sl = slice(hd * HGRN_HEAD, (hd + 1) * HGRN_HEAD)
            oh = o[:, sl]
            ms = jnp.mean(oh * oh, axis=-1, keepdims=True)
            u_ref[0, :, sl] = (oh * lax.rsqrt(ms + NORM_EPS) * gn[:, sl] * _silu(gate[:, sl])).astype(BF16)


def _hgrn_scan(proj_qig, proj_f, lb_logits, gn, o0, *, reverse, last, nxb, layer):
    b, t, e3 = proj_qig.shape
    e = e3 // 3
    tb = TOKEN_BLOCK
    lg = HGRN_LANES
    ng = e // lg
    nblk = t // tb
    assert nblk == nxb + 1, "the context prefix must be exactly one token block"
    if reverse:
        blk = lambda i: jnp.where(i == 0, nxb, nxb - i)
    else:
        blk = lambda i: jnp.where(i == 0, nxb, i - 1)

    def col(off):
        return pl.BlockSpec((1, tb, lg), lambda bb, g, i: (bb, blk(i), off * ng + g))

    row = pl.BlockSpec((1, lg), lambda bb, g, i: (0, g))
    act = pl.BlockSpec((1, tb, lg), lambda bb, g, i: (bb, blk(i), g))
    in_specs = [col(0), col(1 if reverse else 0), col(1),
                pl.BlockSpec((lb_logits.shape[0], lg), lambda bb, g, i: (0, g))]
    args = [proj_qig, proj_f, proj_qig, lb_logits]
    if last:
        in_specs += [col(2), act, row]
        args += [proj_qig, o0, gn]
        out_shape = jax.ShapeDtypeStruct((b, t, e), BF16)
    else:
        out_shape = jax.ShapeDtypeStruct((b, t, e), F32)
    scratch = [pltpu.VMEM((lg // HGRN_HEAD, HGRN_HEAD, HGRN_HEAD), F32)] + [pltpu.VMEM((tb, lg), F32)] * 8
    return pl.pallas_call(
        functools.partial(_hgrn_scan_kernel, reverse=reverse, last=last, layer=layer),
        grid=(b, ng, nblk),
        in_specs=in_specs, out_specs=act, out_shape=out_shape,
        scratch_shapes=scratch,
        compiler_params=_cparams("parallel", "parallel", "arbitrary"),
        name="hgrn_scan_bwd" if reverse else "hgrn_scan_fwd",
    )(*args)


def _hgrn_layer(x, mod, g_pre, g_post, w_out, p, lb_logits, *, nxb, nblk_out, layer, h=None):
    b, t, d = x.shape
    e = lb_logits.shape[1]
    if h is None:
        h = _pre_norm(x, g_pre, mod, shift=False, nxb=nxb)
    h2 = h.reshape(b * t, d)
    w_in = p['w_in']
    w_qig = _bf(jnp.concatenate([w_in[:, :e], w_in[:, 3 * e:]], axis=1))
    proj_f = _project(h2, _bf(w_in[:, e:3 * e]), out_dtype=F32).reshape(b, t, 2 * e)
    proj_qig = _project(h2, w_qig, out_dtype=BF16).reshape(b, t, 3 * e)
    gn = jnp.tile(p['g_norm'], e // HGRN_HEAD)[None]
    o0 = _hgrn_scan(proj_qig, proj_f, lb_logits, gn, None, reverse=False, last=False, nxb=nxb,
                    layer=layer)
    u = _hgrn_scan(proj_qig, proj_f, lb_logits, gn, o0, reverse=True, last=True, nxb=nxb,
                   layer=layer)
    return _out_residual(u, _bf(w_out), x, g_post, mod, nxb=nxb, nblk_out=nblk_out)


def _hy_pre_kernel(*refs, nxb, nblk):
    cur = refs[0:4]
    prv = refs[4:7]
    nxt = refs[7:10]
    w = refs[10:13]
    bias = refs[13:16]
    u_ref, g0_ref = refs[16], refs[17]
    i = pl.program_id(1)
    tb = cur[0].shape[1]
    row = lax.broadcasted_iota(jnp.int32, (tb, 1), 0)
    has_prev = jnp.logical_and(i != 0, i != nxb)
    has_next = jnp.logical_and(i != nxb - 1, i != nblk - 1)

    hr = prv[0].shape[1]

    def conv(j):
        x = cur[j][0].astype(F32)
        before = prv[j][0].astype(F32)[hr - 1:hr]
        after = nxt[j][0].astype(F32)[0:1]
        up = jnp.where(row == 0, jnp.where(has_prev, before, 0.0), pltpu.roll(x, 1, 0))
        dn = jnp.where(row == tb - 1, jnp.where(has_next, after, 0.0), pltpu.roll(x, tb - 1, 0))
        wj = w[j][...]
        return wj[0:1] * up + wj[1:2] * x + wj[2:3] * dn + bias[j][...]

    u_ref[0] = (conv(2) * conv(1)).astype(u_ref.dtype)
    g0_ref[0] = (conv(0) * _silu(cur[3][0].astype(F32))).astype(g0_ref.dtype)


def _hy_pre(proj, conv_w, conv_b, *, nxb):
    b, t, e4 = proj.shape
    e = e4 // 4
    tb = TOKEN_BLOCK
    lg = 512
    ng = e // lg
    nblk = t // tb
    hr = 16
    hb = tb // hr
    nh = t // hr

    def col(off):
        return pl.BlockSpec((1, tb, lg), lambda bb, i, g: (bb, i, off * ng + g))

    def halo_prev(off):
        return pl.BlockSpec((1, hr, lg), lambda bb, i, g: (bb, jnp.maximum(i * hb - 1, 0), off * ng + g))

    def halo_next(off):
        return pl.BlockSpec((1, hr, lg), lambda bb, i, g: (bb, jnp.minimum((i + 1) * hb, nh - 1), off * ng + g))

    def wcol(rows, off):
        return pl.BlockSpec((rows, lg), lambda bb, i, g: (0, off * ng + g))

    in_specs = ([col(o) for o in range(4)] + [halo_prev(o) for o in range(3)]
                + [halo_next(o) for o in range(3)] + [wcol(3, o) for o in range(3)]
                + [wcol(1, o) for o in range(3)])
    args = [proj] * 10 + [conv_w] * 3 + [conv_b[None]] * 3
    act = pl.BlockSpec((1, tb, lg), lambda bb, i, g: (bb, i, g))
    return pl.pallas_call(
        functools.partial(_hy_pre_kernel, nxb=nxb, nblk=nblk),
        grid=(b, nblk, ng),
        in_specs=in_specs, out_specs=[act, act],
        out_shape=[jax.ShapeDtypeStruct((b, t, e), BF16)] * 2,
        compiler_params=_cparams("parallel", "parallel", "parallel"),
        name="hyena_short_conv",
    )(*args)


def _hy_filter_kernel(fv_ref, w1_ref, b1_ref, w2_ref, b2_ref, w3_ref, b3_ref, sf_ref,
                      w4f_ref, w4b_ref, dl_ref, hf_ref, hb_ref, *, length):
    tl = hf_ref.shape[0]
    n = (pl.program_id(0) * tl + lax.broadcasted_iota(jnp.int32, (tl, 1), 0)).astype(F32)
    t = n * (1.0 / (length - 1))
    lane = lax.broadcasted_iota(jnp.int32, (tl, LANE), 1)
    nb = (HYENA_EMB - 1) // 2
    ang = (2.0 * math.pi / length) * n * fv_ref[...]
    z = jnp.where(lane == 0, t,
                  jnp.where(lane <= nb, jnp.cos(ang),
                            jnp.where(lane <= 2 * nb, -jnp.sin(ang), 0.0)))
    sf = sf_ref[...]
    hdn = jnp.sin(sf * (_dot(z, w1_ref[...], HIGHEST) + b1_ref[...]))
    hdn = jnp.sin(sf * (_dot(hdn, w2_ref[...], HIGHEST) + b2_ref[...]))
    hdn = jnp.sin(sf * (_dot(hdn, w3_ref[...], HIGHEST) + b3_ref[...]))
    window = jnp.exp(-t * dl_ref[...])
    hf_ref[...] = _dot(hdn, w4f_ref[...], HIGHEST) * window
    hb_ref[...] = _dot(hdn, w4b_ref[...], HIGHEST) * window


def _hy_filters(length, p, e):
    tl = min(length, 1024)
    lg = 512
    ng = e // lg
    nb = (HYENA_EMB - 1) // 2
    freqs = np.linspace(1e-4, nb - 1, nb, dtype=np.float32)
    fv = np.zeros((1, LANE), np.float32)
    fv[0, 1:1 + nb] = freqs
    fv[0, 1 + nb:1 + 2 * nb] = freqs
    deltas = np.abs(np.linspace(math.log(HYENA_TARGET) / HYENA_SLOW_DECAY,
                                math.log(HYENA_TARGET) / HYENA_FAST_DECAY, e, dtype=np.float32))[None]
    fw = HYENA_FILTER_WIDTH
    w1 = jnp.pad(p['f_w1'], ((0, LANE - HYENA_EMB), (0, 0)))
    full = lambda shape: pl.BlockSpec(shape, lambda r, g: (0, 0))
    in_specs = [full((1, LANE)), full((LANE, fw)), full((1, fw)), full((fw, fw)), full((1, fw)),
                full((fw, fw)), full((1, fw)), full((1, fw)),
                pl.BlockSpec((fw, lg), lambda r, g: (0, g)),
                pl.BlockSpec((fw, lg), lambda r, g: (0, ng + g)),
                pl.BlockSpec((1, lg), lambda r, g: (0, g))]
    out = pl.BlockSpec((tl, lg), lambda r, g: (r, g))
    return pl.pallas_call(
        functools.partial(_hy_filter_kernel, length=length),
        grid=(length // tl, ng),
        in_specs=in_specs, out_specs=[out, out],
        out_shape=[jax.ShapeDtypeStruct((length, e), F32)] * 2,
        compiler_params=_cparams("parallel", "parallel"),
        name="hyena_filters",
    )(jnp.asarray(fv), w1, p['f_b1'][None], p['f_w2'], p['f_b2'][None], p['f_w3'], p['f_b3'][None],
      p['sin_freq'][None], p['f_w4'], p['f_w4'], jnp.asarray(deltas))


def _cmul(x, h, half):
    xr, xi = x[:half], x[half:]
    hr, hi = h[:half], h[half:]
    return jnp.concatenate([xr * hr - xi * hi, xr * hi + xi * hr], axis=0)


def _conj(x, half):
    return jnp.concatenate([x[:half], -x[half:]], axis=0)


def _tdot(tab, x, idx=None):
    get = (lambda ref: ref[...]) if idx is None else (lambda ref: ref[idx])
    xh = x.astype(BF16)
    out = _dot(get(tab[0]), xh)
    if len(tab) == 2:
        xl = (x - xh.astype(F32)).astype(BF16)
        out = out + _dot(get(tab[0]), xl) + _dot(get(tab[1]), xh)
    return out


def _split_tables(refs, count):
    per = len(FFT_TABLE_PARTS)
    return [refs[i * per:(i + 1) * per] for i in range(count)], refs[count * per:]


def _long_conv_direct_kernel(u_ref, g0_ref, hf_ref, hb_ref, fb_ref, *rest):
    (fd_t, fi_t), (o_ref, h_s) = _split_tables(rest, 2)
    half = fd_t[0].shape[0] // 2

    @pl.when(pl.program_id(1) == 0)
    def _():
        h_s[...] = _tdot(fd_t, hf_ref[...]) + _conj(_tdot(fd_t, hb_ref[...]), half)

    u = u_ref[0].astype(F32)
    y = _tdot(fi_t, _cmul(_tdot(fd_t, u), h_s[...], half))
    o_ref[0] = ((y + u * fb_ref[...]) * g0_ref[0].astype(F32)).astype(o_ref.dtype)


def _long_conv_kernel(u_ref, g0_ref, hf_ref, hb_ref, fb_ref, *rest):
    (t1_t, t1t_t, f2_t, f2t_t), (o_ref, x_s, a_s, h_s) = _split_tables(rest, 4)
    n2, rows1, n1h = t1_t[0].shape
    n1 = rows1 // 2
    un = min(FFT_UNROLL, n1, n2)
    assert n1 % un == 0 and n2 % un == 0
    xp = _odd_pitch(n2)
    ap = _odd_pitch(rows1)

    def a_block(m2):
        return pl.ds(pl.multiple_of(m2 * ap, 8), rows1)

    def stage1(src):
        def copy(m1, carry):
            x_s[pl.ds(pl.multiple_of(m1 * xp, 8), n2), :] = src(pl.ds(pl.multiple_of(m1 * n2, n2), n2))
            return carry
        lax.fori_loop(0, n1h, copy, 0, unroll=4)

        def body(j, carry):
            m2s = [j * un + q for q in range(un)]
            xs = [x_s[pl.ds(m2, n1h, stride=xp), :] for m2 in m2s]
            outs = [_tdot(t1_t, x, m2) for m2, x in zip(m2s, xs)]
            for m2, o in zip(m2s, outs):
                a_s[a_block(m2), :] = o
            return carry
        lax.fori_loop(0, n2 // un, body, 0)

    def stage2(k1s):
        zs = [jnp.concatenate([a_s[pl.ds(k1, n2, stride=ap), :],
                               a_s[pl.ds(n1 + k1, n2, stride=ap), :]], axis=0) for k1 in k1s]
        return [_tdot(f2_t, z) for z in zs]

    def spec_rows(k1):
        return pl.ds(pl.multiple_of(k1 * 2 * n2, 2 * n2), 2 * n2)

    @pl.when(pl.program_id(1) == 0)
    def _():
        stage1(lambda rows: hf_ref[rows, :])

        def spec_f(j, carry):
            k1s = [j * un + q for q in range(un)]
            for k1, x in zip(k1s, stage2(k1s)):
                h_s[spec_rows(k1), :] = x
            return carry
        lax.fori_loop(0, n1 // un, spec_f, 0)
        stage1(lambda rows: hb_ref[rows, :])

        def spec_b(j, carry):
            k1s = [j * un + q for q in range(un)]
            for k1, x in zip(k1s, stage2(k1s)):
                h_s[spec_rows(k1), :] = h_s[spec_rows(k1), :] + _conj(x, n2)
            return carry
        lax.fori_loop(0, n1 // un, spec_b, 0)

    stage1(lambda rows: u_ref[0, rows, :].astype(F32))

    def mid(j, carry):
        k1s = [j * un + q for q in range(un)]
        ys = [_cmul(x, h_s[spec_rows(k1), :], n2) for k1, x in zip(k1s, stage2(k1s))]
        zs = [_tdot(f2t_t, y) for y in ys]
        for k1, z in zip(k1s, zs):
            a_s[pl.ds(k1, n2, stride=ap), :] = z[:n2]
            a_s[pl.ds(n1 + k1, n2, stride=ap), :] = z[n2:]
        return carry
    lax.fori_loop(0, n1 // un, mid, 0)

    def inv1(j, carry):
        m2s = [j * un + q for q in range(un)]
        ys = [_tdot(t1t_t, a_s[a_block(m2), :], m2) for m2 in m2s]
        for m2, y in zip(m2s, ys):
            x_s[pl.ds(m2, n1h, stride=xp), :] = y
        return carry
    lax.fori_loop(0, n2 // un, inv1, 0)

    fb = fb_ref[...]

    def finish(m1, carry):
        rows = pl.ds(pl.multiple_of(m1 * n2, n2), n2)
        y = x_s[pl.ds(pl.multiple_of(m1 * xp, 8), n2), :]
        out = (y + u_ref[0, rows, :].astype(F32) * fb) * g0_ref[0, rows, :].astype(F32)
        o_ref[0, rows, :] = out.astype(o_ref.dtype)
        return carry
    lax.fori_loop(0, n1h, finish, 0, unroll=4)


def _odd_pitch(rows):
    return rows if (rows // 8) % 2 else rows + 8


def _fft_tables(length):
    n = 2 * length
    n2 = FFT_N2
    n1 = n // n2
    n1h = n1 // 2
    k1 = np.arange(n1, dtype=np.float64)[:, None]
    m1 = np.arange(n1h, dtype=np.float64)[None, :]
    t1 = []
    for m2 in range(n2):
        phi = 2.0 * np.pi * (k1 * m1 / n1 + k1 * m2 / n)
        t1.append(np.concatenate([np.cos(phi), -np.sin(phi)], axis=0))
    t1 = np.stack(t1)
    t1t = np.transpose(t1, (0, 2, 1)) / n
    k2 = np.arange(n2, dtype=np.float64)[:, None]
    m2 = np.arange(n2, dtype=np.float64)[None, :]
    th = 2.0 * np.pi * k2 * m2 / n2
    mr, mi = np.cos(th), -np.sin(th)
    f2 = np.block([[mr, -mi], [mi, mr]])
    return _hi_lo(t1) + _hi_lo(t1t) + _hi_lo(f2) + _hi_lo(f2.T)


def _hi_lo(a):
    a32 = jnp.asarray(a.astype(np.float32))
    hi = a32.astype(BF16)
    parts = {"hi": hi, "lo": (a32 - hi.astype(F32)).astype(BF16)}
    return [parts[name] for name in FFT_TABLE_PARTS]


def _dft_tables(length):
    n = 2 * length
    k = np.arange(n, dtype=np.float64)[:, None]
    m = np.arange(length, dtype=np.float64)[None, :]
    phi = 2.0 * np.pi * k * m / n
    fd = np.concatenate([np.cos(phi), -np.sin(phi)], axis=0)
    return _hi_lo(fd) + _hi_lo(fd.T / n)


def _long_conv(u, g0, hf, hb, fbias, *, length, blk_index):
    b, t, e = u.shape
    lg = FFT_LANES
    ng = e // lg
    seq_spec = pl.BlockSpec((1, length, lg), lambda g, bb: (bb, blk_index, g))
    filt_spec = pl.BlockSpec((length, lg), lambda g, bb: (0, g))
    row_spec = pl.BlockSpec((1, lg), lambda g, bb: (0, g))
    out_spec = pl.BlockSpec((1, length, lg), lambda g, bb: (bb, 0, g))
    common = dict(
        grid=(ng, b),
        out_specs=out_spec,
        out_shape=jax.ShapeDtypeStruct((b, length, e), BF16),
        compiler_params=_cparams("parallel", "arbitrary"),
    )
    const = lambda a: pl.BlockSpec(a.shape, lambda g, bb: (0,) * a.ndim)
    if length <= TOKEN_BLOCK:
        tabs = _dft_tables(length)
        return pl.pallas_call(
            _long_conv_direct_kernel,
            in_specs=[seq_spec, seq_spec, filt_spec, filt_spec, row_spec] + [const(a) for a in tabs],
            scratch_shapes=[pltpu.VMEM((4 * length, lg), F32)],
            name="hyena_long_conv_ctx", **common,
        )(u, g0, hf, hb, fbias, *tabs)
    tabs = _fft_tables(length)
    n2 = FFT_N2
    n1 = 2 * length // n2
    return pl.pallas_call(
        _long_conv_kernel,
        in_specs=[seq_spec, seq_spec, filt_spec, filt_spec, row_spec] + [const(a) for a in tabs],
        scratch_shapes=[pltpu.VMEM((n1 // 2 * _odd_pitch(n2), lg), F32),
                        pltpu.VMEM((n2 * _odd_pitch(2 * n1), lg), F32),
                        pltpu.VMEM((4 * length, lg), F32)],
        name="hyena_long_conv", **common,
    )(u, g0, hf, hb, fbias, *tabs)


def _hyena_layer(x, mod, g_pre, g_post, w_out, p, *, nxb, nblk_out, h=None, next_norm=None):
    b, t, d = x.shape
    e = p['filter_bias'].shape[0]
    seq = nxb * TOKEN_BLOCK
    ctx_len = t - seq
    if h is None:
        h = _pre_norm(x, g_pre, mod, shift=False, nxb=nxb)
    proj = _project(h.reshape(b * t, d), _bf(p['w_in']), out_dtype=BF16).reshape(b, t, 4 * e)
    u, g0 = _hy_pre(proj, p['conv_w'], p['conv_b'], nxb=nxb)
    fbias = p['filter_bias'][None]
    hf, hb = _hy_filters(seq, p, e)
    yx = _long_conv(u, g0, hf, hb, fbias, length=seq, blk_index=0)
    hf, hb = _hy_filters(ctx_len, p, e)
    yc = _long_conv(u, g0, hf, hb, fbias, length=ctx_len, blk_index=seq // ctx_len)
    y = jnp.concatenate([yx, yc], axis=1)
    return _out_residual(y, _bf(w_out), x, g_post, mod, nxb=nxb, nblk_out=nblk_out,
                         next_norm=next_norm)


def _modulation(c, c_ctx, ada_w, ada_b):
    b, d = c.shape
    depth = ada_w.shape[0]
    rows = -(-(b + 1) // 8) * 8
    cstack = jnp.zeros((rows, d), F32).at[:b].set(c).at[b].set(c_ctx)
    ada = _ada_all(cstack, ada_w, ada_b).reshape(depth, rows, 3, d)
    lat = ada[:, :b]
    cx = jnp.broadcast_to(ada[:, b:b + 1], lat.shape)
    return jnp.stack([lat, cx], axis=2)


def kernel(x, c, ctx, c_ctx, ada_w, ada_b, norm_pre, norm_post, w_out,
           l0_w_in, l0_mu, l0_w0, l0_w2, l0_a0, l0_a2, l0_k_k, l0_k_a, l0_r_k, l0_ln_w, l0_ln_b,
           l1_w_in, l1_conv_w, l1_conv_b, l1_f_w1, l1_f_b1, l1_f_w2, l1_f_b2, l1_f_w3, l1_f_b3,
           l1_f_w4, l1_sin_freq, l1_filter_bias,
           l2_w_in, l2_g_norm, hgrn_lb_logits,
           l3_w_in, l3_mu, l3_w0, l3_w2, l3_a0, l3_a2, l3_k_k, l3_k_a, l3_r_k, l3_ln_w, l3_ln_b,
           l3_v0, l3_v2):
    rwkv0 = dict(w_in=l0_w_in, mu=l0_mu, w0=l0_w0, w2=l0_w2, a0=l0_a0, a2=l0_a2, k_k=l0_k_k,
                 k_a=l0_k_a, r_k=l0_r_k, ln_w=l0_ln_w, ln_b=l0_ln_b)
    hyena1 = dict(w_in=l1_w_in, conv_w=l1_conv_w, conv_b=l1_conv_b, f_w1=l1_f_w1, f_b1=l1_f_b1,
                  f_w2=l1_f_w2, f_b2=l1_f_b2, f_w3=l1_f_w3, f_b3=l1_f_b3, f_w4=l1_f_w4,
                  sin_freq=l1_sin_freq, filter_bias=l1_filter_bias)
    hgrn2 = dict(w_in=l2_w_in, g_norm=l2_g_norm)
    rwkv3 = dict(w_in=l3_w_in, mu=l3_mu, w0=l3_w0, w2=l3_w2, a0=l3_a0, a2=l3_a2, k_k=l3_k_k,
                 k_a=l3_k_a, r_k=l3_r_k, ln_w=l3_ln_w, ln_b=l3_ln_b, v0=l3_v0, v2=l3_v2)
    b, seq, d = x.shape
    assert ctx.shape[1] == TOKEN_BLOCK and seq % TOKEN_BLOCK == 0
    nxb = seq // TOKEN_BLOCK
    mods = _modulation(c, c_ctx, ada_w, ada_b)
    xa = jnp.concatenate([x, ctx], axis=1)
    (xa, h1), proj0 = _rwkv_layer(xa, mods[0], norm_pre[0][None], norm_post[0][None], w_out[0],
                                  rwkv0, None, nxb=nxb, nblk_out=nxb + 1,
                                  next_norm=(norm_pre[1][None], mods[1]))
    xa, h2 = _hyena_layer(xa, mods[1], norm_pre[1][None], norm_post[1][None], w_out[1], hyena1,
                          nxb=nxb, nblk_out=nxb + 1, h=h1, next_norm=(norm_pre[2][None], mods[2]))
    xa = _hgrn_layer(xa, mods[2], norm_pre[2][None], norm_post[2][None], w_out[2], hgrn2,
                     hgrn_lb_logits, nxb=nxb, nblk_out=nxb + 1, layer=2, h=h2)
    xa, _ = _rwkv_layer(xa, mods[3], norm_pre[3][None], norm_post[3][None], w_out[3], rwkv3,
                        proj0, nxb=nxb, nblk_out=nxb)
    return xa
```
